```python
import math
import jax, jax.numpy as jnp
from jax import lax
import numpy as np

D_MODEL = 2048
BATCH = 2
SEQ = 4096
DEPTH = 4

N_MIXERS = 3
N_S5 = (DEPTH + 2) // 3
N_SSD = (DEPTH + 1) // 3
N_RET = DEPTH // 3

DN_ALPHA = (2 * DEPTH) ** 0.25
DN_BETA = (8 * DEPTH) ** -0.25
LN_EPS = 1e-5

D_FF = 4 * D_MODEL

S5_GROUP = 16
S5_GROUPS = D_MODEL // S5_GROUP
S5_STATE = 64
S5_DT_MIN = 1e-3
S5_DT_MAX = 1e-1

SSD_EXPAND = 2
SSD_D_INNER = SSD_EXPAND * D_MODEL
SSD_HEADDIM = 64
SSD_HEADS = SSD_D_INNER // SSD_HEADDIM
SSD_GROUPS = 8
SSD_STATE = 128
SSD_CONV = 4
SSD_CHUNK = 128
SSD_CONV_DIM = SSD_D_INNER + 2 * SSD_GROUPS * SSD_STATE
SSD_IN_DIM = SSD_D_INNER + SSD_CONV_DIM + SSD_HEADS

RET_HEADS = 8
RET_DK = D_MODEL // RET_HEADS
RET_DV = 2 * D_MODEL // RET_HEADS
RET_CHUNK = 128
RET_IN_DIM = 2 * D_MODEL + 4 * D_MODEL
RET_ROPE_BASE = 10000.0

kernel_name = 'hybrid_s5_ssd_retention_deepnorm'


def _layer_norm(x, g, b):
    xf = x.astype(jnp.float32)
    mu = jnp.mean(xf, axis=-1, keepdims=True)
    var = jnp.mean(jnp.square(xf - mu), axis=-1, keepdims=True)
    return ((xf - mu) * lax.rsqrt(var + LN_EPS)).astype(x.dtype) * g + b


def _mlp(x, w1, w2):
    h = jnp.square(jax.nn.relu(x @ w1))
    return h @ w2


def _s5_combine(left, right):
    ar1, ai1, br1, bi1 = left
    ar2, ai2, br2, bi2 = right
    return (ar2 * ar1 - ai2 * ai1,
            ar2 * ai1 + ai2 * ar1,
            ar2 * br1 - ai2 * bi1 + br2,
            ar2 * bi1 + ai2 * br1 + bi2)


def _s5_mixer(x, w_in, lam_re, lam_im, log_dt, b_re, b_im, c_re, c_im, d_skip, w_out, w_gate):
    bsz, seq, _ = x.shape
    f32 = jnp.float32
    u = (x @ w_in).astype(f32)
    ug = u.reshape(bsz, seq, S5_GROUPS, S5_GROUP)
    lr = lam_re.astype(f32)
    li = lam_im.astype(f32)
    dt = jnp.exp(log_dt.astype(f32))[:, None]
    mag = jnp.exp(lr * dt)
    ar = mag * jnp.cos(li * dt)
    ai = mag * jnp.sin(li * dt)
    den = lr * lr + li * li
    zr = ((ar - 1.0) * lr + ai * li) / den
    zi = (ai * lr - (ar - 1.0) * li) / den
    br_ = b_re.astype(f32)
    bi_ = b_im.astype(f32)
    bbr = zr[..., None] * br_ - zi[..., None] * bi_
    bbi = zr[..., None] * bi_ + zi[..., None] * br_
    bu_r = jnp.einsum('blgc,gpc->blgp', ug, bbr)
    bu_i = jnp.einsum('blgc,gpc->blgp', ug, bbi)
    a_r = jnp.broadcast_to(ar, (1, seq) + ar.shape)
    a_i = jnp.broadcast_to(ai, (1, seq) + ai.shape)
    _, _, s_r, s_i = lax.associative_scan(_s5_combine, (a_r, a_i, bu_r, bu_i), axis=1)
    y = (jnp.einsum('blgp,gcp->blgc', s_r, c_re.astype(f32))
         - jnp.einsum('blgp,gcp->blgc', s_i, c_im.astype(f32)))
    y = y.reshape(bsz, seq, D_MODEL) + d_skip.astype(f32) * u
    h = jax.nn.gelu(y).astype(x.dtype)
    return (h @ w_out) * jax.nn.sigmoid(h @ w_gate)


def _causal_depthwise_conv(x, w, b):
    rhs = w.astype(x.dtype)[:, None, :]
    out = lax.conv_general_dilated(x, rhs, window_strides=(1,), padding=[(SSD_CONV - 1, 0)],
                                   dimension_numbers=('NWC', 'WIO', 'NWC'),
                                   feature_group_count=x.shape[-1])
    return out + b


def _ssd_chunked(xdt, adt, bm, cm):
    bsz, seq = xdt.shape[:2]
    nc = seq // SSD_CHUNK
    r = SSD_HEADS // SSD_GROUPS
    xc = xdt.reshape(bsz, nc, SSD_CHUNK, SSD_GROUPS, r, SSD_HEADDIM)
    bc = bm.reshape(bsz, nc, SSD_CHUNK, SSD_GROUPS, SSD_STATE)
    cc = cm.reshape(bsz, nc, SSD_CHUNK, SSD_GROUPS, SSD_STATE)
    a = adt.reshape(bsz, nc, SSD_CHUNK, SSD_GROUPS, r).transpose(0, 3, 4, 1, 2)
    a_cs = jnp.cumsum(a, axis=-1)
    causal = jnp.tril(jnp.ones((SSD_CHUNK, SSD_CHUNK), dtype=bool))
    seg = a_cs[..., :, None] - a_cs[..., None, :]
    lmat = jnp.exp(jnp.where(causal, seg, -jnp.inf))
    cb = jnp.einsum('bclgn,bcsgn->bgcls', cc, bc)
    y_diag = jnp.einsum('bgrcls,bcsgrp->bclgrp', cb[:, :, None] * lmat, xc)
    decay_to_end = jnp.exp(a_cs[..., -1:] - a_cs).transpose(0, 3, 4, 1, 2)
    states = jnp.einsum('bcsgn,bcsgrp->bcgrpn', bc, xc * decay_to_end[..., None])
    tot = a_cs[..., -1]
    cum_tot = jnp.cumsum(tot, axis=-1)
    excl = cum_tot - tot
    strict = jnp.tril(jnp.ones((nc, nc), dtype=bool), -1)
    m = jnp.exp(jnp.where(strict, excl[..., :, None] - cum_tot[..., None, :], -jnp.inf))
    enter = jnp.einsum('bgrzc,bcgrpn->bzgrpn', m, states)
    decay_in = jnp.exp(a_cs).transpose(0, 3, 4, 1, 2)
    y_off = jnp.einsum('bclgn,bcgrpn->bclgrp', cc, enter) * decay_in[..., None]
    return (y_diag + y_off).reshape(bsz, seq, SSD_HEADS, SSD_HEADDIM)


def _ssd_mixer(x, w_in, conv_w, conv_b, dt_bias, a_log, d_skip, norm_w, w_out):
    bsz, seq, _ = x.shape
    f32 = jnp.float32
    proj = x @ w_in
    z, xbc, dt = jnp.split(proj, [SSD_D_INNER, SSD_D_INNER + SSD_CONV_DIM], axis=-1)
    xbc = jax.nn.silu(_causal_depthwise_conv(xbc, conv_w, conv_b)).astype(f32)
    xs, bm, cm = jnp.split(xbc, [SSD_D_INNER, SSD_D_INNER + SSD_GROUPS * SSD_STATE], axis=-1)
    xs = xs.reshape(bsz, seq, SSD_HEADS, SSD_HEADDIM)
    bm = bm.reshape(bsz, seq, SSD_GROUPS, SSD_STATE)
    cm = cm.reshape(bsz, seq, SSD_GROUPS, SSD_STATE)
    dt = jax.nn.softplus(dt.astype(f32) + dt_bias.astype(f32))
    a = -jnp.exp(a_log.astype(f32))
    y = _ssd_chunked(xs * dt[..., None], dt * a, bm, cm)
    y = y + d_skip.astype(f32)[:, None] * xs
    y = y.reshape(bsz, seq, SSD_D_INNER) * jax.nn.silu(z.astype(f32))
    yg = y.reshape(bsz, seq, SSD_GROUPS, SSD_D_INNER // SSD_GROUPS)
    yg = yg * lax.rsqrt(jnp.mean(jnp.square(yg), axis=-1, keepdims=True) + LN_EPS)
    y = yg.reshape(bsz, seq, SSD_D_INNER) * norm_w.astype(f32)
    return y.astype(x.dtype) @ w_out


def _rotate(t, cos, sin):
    t1, t2 = jnp.split(t, 2, axis=-1)
    c = cos[None, :, None, :]
    s = sin[None, :, None, :]
    return jnp.concatenate([t1 * c - t2 * s, t1 * s + t2 * c], axis=-1)


def _retention_mixer(x, w_in, w_out):
    bsz, seq, _ = x.shape
    f32 = jnp.float32
    proj = x @ w_in
    q, k, v, g = jnp.split(proj, [D_MODEL, 2 * D_MODEL, 4 * D_MODEL], axis=-1)
    q = q.astype(f32).reshape(bsz, seq, RET_HEADS, RET_DK)
    k = k.astype(f32).reshape(bsz, seq, RET_HEADS, RET_DK) * (RET_DK ** -0.5)
    v = v.astype(f32).reshape(bsz, seq, RET_HEADS, RET_DV)
    theta = 1.0 / (RET_ROPE_BASE ** jnp.linspace(0.0, 1.0, RET_DK // 2, dtype=f32))
    ang = jnp.arange(seq, dtype=f32)[:, None] * theta[None, :]
    cos, sin = jnp.cos(ang), jnp.sin(ang)
    q = _rotate(q, cos, sin)
    k = _rotate(k, cos, sin)
    lg = jnp.log1p(-jnp.exp2(-5.0 - jnp.arange(RET_HEADS, dtype=f32)))
    pos = jnp.arange(RET_CHUNK, dtype=f32)
    diff = pos[:, None] - pos[None, :]
    dmat = jnp.where(diff >= 0, jnp.exp(jnp.maximum(diff, 0.0)[None] * lg[:, None, None]), 0.0)
    xi = jnp.exp((pos[None, :] + 1.0) * lg[:, None])
    zeta = jnp.exp((RET_CHUNK - 1.0 - pos[None, :]) * lg[:, None])
    chunk_decay = jnp.exp(RET_CHUNK * lg)
    nc = seq // RET_CHUNK
    def to_chunks(t):
        return t.reshape(bsz, nc, RET_CHUNK, RET_HEADS, t.shape[-1]).transpose(1, 0, 3, 2, 4)
    qs, ks, vs = to_chunks(q), to_chunks(k), to_chunks(v)
    def step(state, qkv):
        qc, kc, vc = qkv
        scores = jnp.einsum('bhnd,bhmd->bhnm', qc, kc) * dmat
        inner = jnp.einsum('bhnm,bhme->bhne', scores, vc)
        cross = jnp.einsum('bhnd,bhde->bhne', qc, state) * xi[:, :, None]
        state = (chunk_decay[:, None, None] * state
                 + jnp.einsum('bhmd,bhme->bhde', kc * zeta[:, :, None], vc))
        return state, inner + cross
    state0 = jnp.zeros((bsz, RET_HEADS, RET_DK, RET_DV), f32)
    _, out = lax.scan(step, state0, (qs, ks, vs))
    out = out.transpose(1, 0, 3, 2, 4).reshape(bsz, seq, RET_HEADS, RET_DV)
    mu = jnp.mean(out, axis=-1, keepdims=True)
    var = jnp.mean(jnp.square(out - mu), axis=-1, keepdims=True)
    out = ((out - mu) * lax.rsqrt(var + LN_EPS)).reshape(bsz, seq, 2 * D_MODEL)
    y = jax.nn.silu(g.astype(f32)) * out
    return y.astype(x.dtype) @ w_out


def setup_inputs(seed: int = 0) -> dict:
    key = jax.random.key(seed)
    ks = iter(jax.random.split(key, 32))
    f32 = jnp.float32
    def nrm(shape, scale):
        return jax.random.normal(next(ks), shape, f32) * scale
    def unif(shape, lo, hi):
        return jax.random.uniform(next(ks), shape, f32, lo, hi)
    x = nrm((BATCH, SEQ, D_MODEL), 1.0)
    ln1_g = 1.0 + nrm((DEPTH, D_MODEL), 0.02)
    ln1_b = nrm((DEPTH, D_MODEL), 0.02)
    ln2_g = 1.0 + nrm((DEPTH, D_MODEL), 0.02)
    ln2_b = nrm((DEPTH, D_MODEL), 0.02)
    mlp_w1 = nrm((DEPTH, D_MODEL, D_FF), D_MODEL ** -0.5)
    mlp_w2 = nrm((DEPTH, D_FF, D_MODEL), DN_BETA * D_FF ** -0.5)
    s5_w_in = nrm((N_S5, D_MODEL, D_MODEL), D_MODEL ** -0.5)
    s5_lam_re = -0.5 + nrm((N_S5, S5_GROUPS, S5_STATE), 0.01)
    s5_lam_im = (jnp.pi * jnp.arange(S5_STATE, dtype=f32)) + nrm((N_S5, S5_GROUPS, S5_STATE), 0.01)
    s5_log_dt = unif((N_S5, S5_GROUPS), math.log(S5_DT_MIN), math.log(S5_DT_MAX))
    s5_b_re = nrm((N_S5, S5_GROUPS, S5_STATE, S5_GROUP), (2 * S5_GROUP) ** -0.5)
    s5_b_im = nrm((N_S5, S5_GROUPS, S5_STATE, S5_GROUP), (2 * S5_GROUP) ** -0.5)
    s5_c_re = nrm((N_S5, S5_GROUPS, S5_GROUP, S5_STATE), (2 * S5_STATE) ** -0.5)
    s5_c_im = nrm((N_S5, S5_GROUPS, S5_GROUP, S5_STATE), (2 * S5_STATE) ** -0.5)
    s5_d = nrm((N_S5, D_MODEL), 1.0)
    s5_w_out = nrm((N_S5, D_MODEL, D_MODEL), DN_BETA * D_MODEL ** -0.5)
    s5_w_gate = nrm((N_S5, D_MODEL, D_MODEL), D_MODEL ** -0.5)
    ssd_w_in = nrm((N_SSD, D_MODEL, SSD_IN_DIM), D_MODEL ** -0.5)
    ssd_conv_w = nrm((N_SSD, SSD_CONV, SSD_CONV_DIM), SSD_CONV ** -0.5)
    ssd_conv_b = nrm((N_SSD, SSD_CONV_DIM), 0.02)
    ssd_dt0 = jnp.exp(unif((N_SSD, SSD_HEADS), math.log(1e-3), math.log(1e-1)))
    ssd_dt_bias = ssd_dt0 + jnp.log(-jnp.expm1(-ssd_dt0))
    ssd_a_log = jnp.log(unif((N_SSD, SSD_HEADS), 1.0, 16.0))
    ssd_d = 1.0 + nrm((N_SSD, SSD_HEADS), 0.1)
    ssd_norm_w = 1.0 + nrm((N_SSD, SSD_D_INNER), 0.02)
    ssd_w_out = nrm((N_SSD, SSD_D_INNER, D_MODEL), DN_BETA * SSD_D_INNER ** -0.5)
    ret_w_in = nrm((N_RET, D_MODEL, RET_IN_DIM), D_MODEL ** -0.5)
    ret_w_out = nrm((N_RET, 2 * D_MODEL, D_MODEL), DN_BETA * (2 * D_MODEL) ** -0.5)
    return {'x': x, 'ln1_g': ln1_g, 'ln1_b': ln1_b, 'ln2_g': ln2_g, 'ln2_b': ln2_b,
            'mlp_w1': mlp_w1, 'mlp_w2': mlp_w2,
            's5_w_in': s5_w_in, 's5_lam_re': s5_lam_re, 's5_lam_im': s5_lam_im, 's5_log_dt': s5_log_dt,
            's5_b_re': s5_b_re, 's5_b_im': s5_b_im, 's5_c_re': s5_c_re, 's5_c_im': s5_c_im,
            's5_d': s5_d, 's5_w_out': s5_w_out, 's5_w_gate': s5_w_gate,
            'ssd_w_in': ssd_w_in, 'ssd_conv_w': ssd_conv_w, 'ssd_conv_b': ssd_conv_b,
            'ssd_dt_bias': ssd_dt_bias, 'ssd_a_log': ssd_a_log, 'ssd_d': ssd_d,
            'ssd_norm_w': ssd_norm_w, 'ssd_w_out': ssd_w_out,
            'ret_w_in': ret_w_in, 'ret_w_out': ret_w_out}


def reference(x, ln1_g, ln1_b, ln2_g, ln2_b, mlp_w1, mlp_w2,
              s5_w_in, s5_lam_re, s5_lam_im, s5_log_dt, s5_b_re, s5_b_im, s5_c_re, s5_c_im,
              s5_d, s5_w_out, s5_w_gate,
              ssd_w_in, ssd_conv_w, ssd_conv_b, ssd_dt_bias, ssd_a_log, ssd_d, ssd_norm_w, ssd_w_out,
              ret_w_in, ret_w_out):
    for i in range(DEPTH):
        kind = i % N_MIXERS
        j = i // N_MIXERS
        if kind == 0:
            f = _s5_mixer(x, s5_w_in[j], s5_lam_re[j], s5_lam_im[j], s5_log_dt[j], s5_b_re[j], s5_b_im[j],
                          s5_c_re[j], s5_c_im[j], s5_d[j], s5_w_out[j], s5_w_gate[j])
        elif kind == 1:
            f = _ssd_mixer(x, ssd_w_in[j], ssd_conv_w[j], ssd_conv_b[j], ssd_dt_bias[j], ssd_a_log[j],
                           ssd_d[j], ssd_norm_w[j], ssd_w_out[j])
        else:
            f = _retention_mixer(x, ret_w_in[j], ret_w_out[j])
        x = _layer_norm(DN_ALPHA * x + f.astype(x.dtype), ln1_g[i], ln1_b[i])
        x = _layer_norm(DN_ALPHA * x + _mlp(x, mlp_w1[i], mlp_w2[i]).astype(x.dtype), ln2_g[i], ln2_b[i])
    return x
```

```python
import functools
import math

import jax
import jax.numpy as jnp
from jax import lax
from jax.experimental import pallas as pl
from jax.experimental.pallas import tpu as pltpu

F32 = jnp.float32
MXU_DTYPE = jnp.bfloat16
HI = lax.Precision.HIGHEST

LANES = 128
VMEM_LIMIT_BYTES = 58 * 1024 * 1024

D_MODEL = 2048
DEPTH = 4
DN_ALPHA = (2 * DEPTH) ** 0.25
LN_EPS = 1e-5
D_FF = 4 * D_MODEL

S5_GROUP = 16
S5_GROUPS = D_MODEL // S5_GROUP
S5_STATE = 64
S5_CHUNK = 16
S5_SETS = D_MODEL // LANES
S5_SET_GROUPS = LANES // S5_GROUP
S5_SET_STATE = S5_SET_GROUPS * S5_STATE

SSD_D_INNER = 2 * D_MODEL
SSD_HEADDIM = 64
SSD_HEADS = SSD_D_INNER // SSD_HEADDIM
SSD_GROUPS = 8
SSD_STATE = 128
SSD_CONV = 4
SSD_CHUNK = 128
SSD_BC = 2 * SSD_GROUPS * SSD_STATE
SSD_CONV_DIM = SSD_D_INNER + SSD_BC
SSD_IN_DIM = SSD_D_INNER + SSD_CONV_DIM + SSD_HEADS
SSD_IN_PAD = SSD_D_INNER + SSD_CONV_DIM + LANES
SSD_GROUP_DIM = SSD_D_INNER // SSD_GROUPS
SSD_HEADS_PER_GROUP = SSD_HEADS // SSD_GROUPS

RET_HEADS = 8
RET_DK = D_MODEL // RET_HEADS
RET_DV = 2 * D_MODEL // RET_HEADS
RET_CHUNK = 128
RET_IN_DIM = 6 * D_MODEL
RET_ROPE_BASE = 10000.0


def _params(*semantics):
    return pltpu.CompilerParams(dimension_semantics=semantics, vmem_limit_bytes=VMEM_LIMIT_BYTES)


def _layer_norm(v, g, b):
    mu = jnp.mean(v, axis=-1, keepdims=True)
    d = v - mu
    var = jnp.mean(d * d, axis=-1, keepdims=True)
    return d * lax.rsqrt(var + LN_EPS) * g + b


def _silu(v):
    return v * jax.nn.sigmoid(v)


def _mm(a, b):
    return jnp.dot(a, b, preferred_element_type=F32)


def _mm_nt(a, b, precision=None):
    return lax.dot_general(a, b, (((1,), (1,)), ((), ())), preferred_element_type=F32, precision=precision)


def _mm_tn(a, b):
    return lax.dot_general(a, b, (((0,), (0,)), ((), ())), preferred_element_type=F32)


def _proj_kernel(x_ref, w_ref, o_ref, xb_ref):
    @pl.when(pl.program_id(1) == 0)
    def _():
        xb_ref[...] = x_ref[...].astype(MXU_DTYPE)

    o_ref[...] = _mm(xb_ref[...], w_ref[...])


def _proj(x, w, tm, tn):
    m, k = x.shape
    n = w.shape[1]
    return pl.pallas_call(
        _proj_kernel,
        grid=(m // tm, n // tn),
        in_specs=[pl.BlockSpec((tm, k), lambda i, j: (i, 0)),
                  pl.BlockSpec((k, tn), lambda i, j: (0, j))],
        out_specs=pl.BlockSpec((tm, tn), lambda i, j: (i, j)),
        out_shape=jax.ShapeDtypeStruct((m, n), F32),
        scratch_shapes=[pltpu.VMEM((tm, k), MXU_DTYPE)],
        compiler_params=_params("parallel", "arbitrary"),
        name="proj",
    )(x, w)


def _out_ln_kernel(y_ref, w_ref, x_ref, g_ref, b_ref, o_ref, acc_ref, *, nk):
    kk = pl.program_id(1)

    @pl.when(kk == 0)
    def _():
        acc_ref[...] = jnp.zeros_like(acc_ref)

    acc_ref[...] += _mm(y_ref[...], w_ref[...])

    @pl.when(kk == nk - 1)
    def _():
        o_ref[...] = _layer_norm(DN_ALPHA * x_ref[...] + acc_ref[...], g_ref[...], b_ref[...])


def _out_ln(y, w, x, g, b, tm, tk):
    m, k = y.shape
    d = w.shape[1]
    nk = k // tk
    return pl.pallas_call(
        functools.partial(_out_ln_kernel, nk=nk),
        grid=(m // tm, nk),
        in_specs=[pl.BlockSpec((tm, tk), lambda i, j: (i, j)),
                  pl.BlockSpec((tk, d), lambda i, j: (j, 0)),
                  pl.BlockSpec((tm, d), lambda i, j: (i, 0)),
                  pl.BlockSpec((1, d), lambda i, j: (0, 0)),
                  pl.BlockSpec((1, d), lambda i, j: (0, 0))],
        out_specs=pl.BlockSpec((tm, d), lambda i, j: (i, 0)),
        out_shape=jax.ShapeDtypeStruct((m, d), F32),
        scratch_shapes=[pltpu.VMEM((tm, d), F32)],
        compiler_params=_params("parallel", "arbitrary"),
        name="out_ln",
    )(y, w, x, g, b)


def _mlp_kernel(x_ref, w1_ref, w2_ref, g_ref, b_ref, o_ref, xb_ref, acc_ref, *, nf):
    j = pl.program_id(1)

    @pl.when(j == 0)
    def _():
        xb_ref[...] = x_ref[...].astype(MXU_DTYPE)
        acc_ref[...] = jnp.zeros_like(acc_ref)

    h = _mm(xb_ref[...], w1_ref[...])
    h = jnp.square(jnp.maximum(h, 0.0)).astype(MXU_DTYPE)
    acc_ref[...] += _mm(h, w2_ref[...])

    @pl.when(j == nf - 1)
    def _():
        o_ref[...] = _layer_norm(DN_ALPHA * x_ref[...] + acc_ref[...], g_ref[...], b_ref[...])


def _mlp(x, w1, w2, g, b, tm, tf):
    m, d = x.shape
    f = w1.shape[1]
    nf = f // tf
    return pl.pallas_call(
        functools.partial(_mlp_kernel, nf=nf),
        grid=(m // tm, nf),
        in_specs=[pl.BlockSpec((tm, d), lambda i, j: (i, 0)),
                  pl.BlockSpec((d, tf), lambda i, j: (0, j)),
                  pl.BlockSpec((tf, d), lambda i, j: (j, 0)),
                  pl.BlockSpec((1, d), lambda i, j: (0, 0)),
                  pl.BlockSpec((1, d), lambda i, j: (0, 0))],
        out_specs=pl.BlockSpec((tm, d), lambda i, j: (i, 0)),
        out_shape=jax.ShapeDtypeStruct((m, d), F32),
        scratch_shapes=[pltpu.VMEM((tm, d), MXU_DTYPE), pltpu.VMEM((tm, d), F32)],
        compiler_params=_params("parallel", "arbitrary"),
        name="mlp",
    )(x, w1, w2, g, b)


def _s5_in_kernel(x_ref, w_ref, o_ref):
    u = _mm(x_ref[...].astype(MXU_DTYPE), w_ref[...])
    for gs in range(S5_SETS):
        o_ref[gs] = u[:, gs * LANES:(gs + 1) * LANES]


def _s5_in(x2, w):
    nchunks = x2.shape[0]
    return pl.pallas_call(
        _s5_in_kernel,
        grid=(S5_CHUNK,),
        in_specs=[pl.BlockSpec((nchunks, D_MODEL), lambda t: (0, t)),
                  pl.BlockSpec((D_MODEL, D_MODEL), lambda t: (0, 0))],
        out_specs=pl.BlockSpec((S5_SETS, nchunks, LANES), lambda t: (0, 0, t)),
        out_shape=jax.ShapeDtypeStruct((S5_SETS, nchunks, S5_CHUNK * LANES), F32),
        compiler_params=_params("parallel"),
        name="s5_in",
    )(x2, w)


def _s5_core_kernel(u_ref, lb_ref, rc_ref, cc_ref, a16_ref, d_ref, o_ref,
                    wt_ref, win_ref, wout_ref, v_ref, sp_ref, *, batch, chunks_per_batch):
    nst = S5_SET_STATE
    kdim = S5_CHUNK * LANES
    lb = lb_ref[0]

    drev = lax.dot_general(lb, rc_ref[0], (((1,), (0,)), ((), ())),
                           preferred_element_type=F32, precision=HI)
    row = lax.broadcasted_iota(jnp.int32, (kdim, LANES), 0)
    lane = lax.broadcasted_iota(jnp.int32, (kdim, LANES), 1)
    row_group = (row % LANES) // S5_GROUP
    drev = jnp.where(row_group == lane // S5_GROUP, drev, 0.0).astype(MXU_DTYPE)
    for t in range(S5_CHUNK):
        used = (t + 1) * LANES
        wt_ref[0:used, t * LANES:(t + 1) * LANES] = drev[kdim - used:, :]
        if used < kdim:
            wt_ref[used:, t * LANES:(t + 1) * LANES] = jnp.zeros((kdim - used, LANES), MXU_DTYPE)

    swapped = pltpu.roll(lb, S5_STATE, axis=1)
    low = lane < S5_STATE
    re_dup = jnp.where(low, lb, swapped)
    im_dup = jnp.where(low, swapped, lb)
    for v in range(nst // LANES):
        sel = row_group == 2 * v + (lane >= S5_STATE).astype(jnp.int32)
        win_ref[:, v * LANES:(v + 1) * LANES] = jnp.where(sel, re_dup, 0.0).astype(MXU_DTYPE)
        win_ref[:, nst + v * LANES:nst + (v + 1) * LANES] = jnp.where(sel, im_dup, 0.0).astype(MXU_DTYPE)

    cc = cc_ref[0]
    lane_group = (lax.broadcasted_iota(jnp.int32, (S5_STATE, kdim), 1) % LANES) // S5_GROUP
    for ri in range(2):
        part = cc[ri * S5_STATE:(ri + 1) * S5_STATE, :]
        for g in range(S5_SET_GROUPS):
            r0 = ri * nst + g * S5_STATE
            wout_ref[r0:r0 + S5_STATE, :] = jnp.where(lane_group == g, part, 0.0).astype(MXU_DTYPE)

    u = u_ref[0]
    xb = u.astype(MXU_DTYPE)
    vin = _mm(xb, win_ref[...])
    ntile = nst // LANES
    for k in range(2 * ntile):
        v_ref[k] = vin[:, k * LANES:(k + 1) * LANES]

    a_re = [a16_ref[0, :, k * LANES:(k + 1) * LANES] for k in range(ntile)]
    a_im = [a16_ref[0, :, nst + k * LANES:nst + (k + 1) * LANES] for k in range(ntile)]

    def step(c, carry):
        rows = pl.ds(c, batch, stride=chunks_per_batch)
        new = []
        for k in range(ntile):
            s_re, s_im = carry[k], carry[ntile + k]
            sp_ref[k, rows, :] = s_re
            sp_ref[ntile + k, rows, :] = s_im
            new.append((a_re[k] * s_re - a_im[k] * s_im + v_ref[k, rows, :],
                        a_re[k] * s_im + a_im[k] * s_re + v_ref[ntile + k, rows, :]))
        return tuple(n[0] for n in new) + tuple(n[1] for n in new)

    zero = jnp.zeros((batch, LANES), F32)
    lax.fori_loop(0, chunks_per_batch, step, (zero,) * (2 * ntile))

    sp = jnp.concatenate([sp_ref[k] for k in range(2 * ntile)], axis=1)
    y = _mm(xb, wt_ref[...]) + _mm(sp.astype(MXU_DTYPE), wout_ref[...])
    y = y + d_ref[0] * u
    o_ref[0] = jax.nn.gelu(y).astype(o_ref.dtype)


def _s5_core(ug, lb, rc, cc, a16, drow, batch):
    nsets, nchunks, kdim = ug.shape
    kern = functools.partial(_s5_core_kernel, batch=batch, chunks_per_batch=nchunks // batch)
    return pl.pallas_call(
        kern,
        grid=(nsets,),
        in_specs=[pl.BlockSpec((1, nchunks, kdim), lambda s: (s, 0, 0)),
                  pl.BlockSpec((1, kdim, LANES), lambda s: (s, 0, 0)),
                  pl.BlockSpec((1, LANES, LANES), lambda s: (s, 0, 0)),
                  pl.BlockSpec((1, LANES, kdim), lambda s: (s, 0, 0)),
                  pl.BlockSpec((1, 1, 2 * S5_SET_STATE), lambda s: (s, 0, 0)),
                  pl.BlockSpec((1, 1, kdim), lambda s: (s, 0, 0))],
        out_specs=pl.BlockSpec((1, nchunks, kdim), lambda s: (s, 0, 0)),
        out_shape=jax.ShapeDtypeStruct((nsets, nchunks, kdim), MXU_DTYPE),
        scratch_shapes=[pltpu.VMEM((kdim, kdim), MXU_DTYPE),
                        pltpu.VMEM((kdim, 2 * S5_SET_STATE), MXU_DTYPE),
                        pltpu.VMEM((2 * S5_SET_STATE, kdim), MXU_DTYPE),
                        pltpu.VMEM((2 * S5_SET_STATE // LANES, nchunks, LANES), F32),
                        pltpu.VMEM((2 * S5_SET_STATE // LANES, nchunks, LANES), F32)],
        compiler_params=_params("parallel"),
        name="s5_core",
    )(ug, lb, rc, cc, a16, drow)


def _s5_out_kernel(h_ref, wo_ref, wg_ref, x_ref, g_ref, b_ref, o_ref):
    h = jnp.concatenate([h_ref[gs] for gs in range(S5_SETS)], axis=1)
    f = _mm(h, wo_ref[...]) * jax.nn.sigmoid(_mm(h, wg_ref[...]))
    o_ref[...] = _layer_norm(DN_ALPHA * x_ref[...] + f, g_ref[...], b_ref[...])


def _s5_out(hg, wo, wg, x2, g, b):
    nchunks = x2.shape[0]
    return pl.pallas_call(
        _s5_out_kernel,
        grid=(S5_CHUNK,),
        in_specs=[pl.BlockSpec((S5_SETS, nchunks, LANES), lambda t: (0, 0, t)),
                  pl.BlockSpec((D_MODEL, D_MODEL), lambda t: (0, 0)),
                  pl.BlockSpec((D_MODEL, D_MODEL), lambda t: (0, 0)),
                  pl.BlockSpec((nchunks, D_MODEL), lambda t: (0, t)),
                  pl.BlockSpec((1, D_MODEL), lambda t: (0, 0)),
                  pl.BlockSpec((1, D_MODEL), lambda t: (0, 0))],
        out_specs=pl.BlockSpec((nchunks, D_MODEL), lambda t: (0, t)),
        out_shape=jax.ShapeDtypeStruct(x2.shape, F32),
        compiler_params=_params("parallel"),
        name="s5_out",
    )(hg, wo, wg, x2, g, b)


def _s5_tables(lam_re, lam_im, log_dt, b_re, b_im, c_re, c_im, d_skip):
    lr = lam_re.astype(F32)
    li = lam_im.astype(F32)
    dt = jnp.exp(log_dt.astype(F32))[:, None]
    mag = jnp.exp(lr * dt)
    ar = mag * jnp.cos(li * dt)
    ai = mag * jnp.sin(li * dt)
    den = lr * lr + li * li
    zr = ((ar - 1.0) * lr + ai * li) / den
    zi = (ai * lr - (ar - 1.0) * li) / den
    br_ = b_re.astype(F32)
    bi_ = b_im.astype(F32)
    bbr = zr[..., None] * br_ - zi[..., None] * bi_
    bbi = zr[..., None] * bi_ + zi[..., None] * br_
    k = jnp.arange(S5_CHUNK + 1, dtype=F32)[:, None, None]
    pmag = jnp.exp(k * (lr * dt))
    pr = pmag * jnp.cos(k * (li * dt))
    pi = pmag * jnp.sin(k * (li * dt))

    prd = pr[S5_CHUNK - 1::-1][..., None]
    pid = pi[S5_CHUNK - 1::-1][..., None]
    ab_re = prd * bbr - pid * bbi
    ab_im = prd * bbi + pid * bbr
    ab = jnp.stack([ab_re, ab_im], axis=2)
    ab = ab.transpose(0, 1, 4, 2, 3)
    ab = ab.reshape(S5_CHUNK, S5_SETS, S5_SET_GROUPS, S5_GROUP, 2 * S5_STATE)
    lb = ab.transpose(1, 0, 2, 3, 4).reshape(S5_SETS, S5_CHUNK * LANES, 2 * S5_STATE)

    cr = c_re.astype(F32)
    ci = c_im.astype(F32)
    def rows_p(t):
        return t.reshape(S5_SETS, S5_SET_GROUPS, S5_GROUP, S5_STATE).transpose(0, 3, 1, 2).reshape(
            S5_SETS, S5_STATE, LANES)
    rc = jnp.concatenate([rows_p(cr), -rows_p(ci)], axis=1)

    prs = pr[1:][:, :, None, :]
    pis = pi[1:][:, :, None, :]
    ca_re = cr[None] * prs - ci[None] * pis
    ca_im = cr[None] * pis + ci[None] * prs
    def rows_p_steps(t):
        t = t.reshape(S5_CHUNK, S5_SETS, S5_SET_GROUPS, S5_GROUP, S5_STATE)
        return t.transpose(1, 4, 0, 2, 3).reshape(S5_SETS, S5_STATE, S5_CHUNK * LANES)
    cc = jnp.concatenate([rows_p_steps(ca_re), -rows_p_steps(ca_im)], axis=1)

    a16 = jnp.concatenate([pr[S5_CHUNK].reshape(S5_SETS, 1, S5_SET_STATE),
                           pi[S5_CHUNK].reshape(S5_SETS, 1, S5_SET_STATE)], axis=2)
    drow = jnp.tile(d_skip.astype(F32).reshape(S5_SETS, 1, LANES), (1, 1, S5_CHUNK))
    return lb, rc, cc, a16, drow


def _s5_layer(x, batch, w_in, lam_re, lam_im, log_dt, b_re, b_im, c_re, c_im, d_skip, w_out, w_gate, g, b):
    m = x.shape[0]
    x2 = x.reshape(m // S5_CHUNK, S5_CHUNK * D_MODEL)
    lb, rc, cc, a16, drow = _s5_tables(lam_re, lam_im, log_dt, b_re, b_im, c_re, c_im, d_skip)
    ug = _s5_in(x2, w_in.astype(MXU_DTYPE))
    hg = _s5_core(ug, lb, rc, cc, a16, drow, batch)
    out = _s5_out(hg, w_out.astype(MXU_DTYPE), w_gate.astype(MXU_DTYPE), x2, g, b)
    return out.reshape(m, D_MODEL)


def _split_hi_lo(v):
    hi = v.astype(MXU_DTYPE)
    lo = (v - hi.astype(F32)).astype(MXU_DTYPE)
    return jnp.concatenate([hi, lo], axis=1)


def _ssd_core_kernel(z_ref, xs_ref, bc_ref, dt_ref, cwx_ref, cwb_ref, cbx_ref, cbb_ref,
                     dtb_ref, a_ref, dsk_ref, nw_ref, e2_ref, o_ref,
                     xbuf_ref, bbuf_ref, st_ref, yd_ref):
    q = SSD_CHUNK
    pad = 8

    @pl.when(pl.program_id(1) == 0)
    def _():
        xbuf_ref[0:pad, :] = jnp.zeros((pad, SSD_D_INNER), F32)
        bbuf_ref[0:pad, :] = jnp.zeros((pad, SSD_BC), F32)
        st_ref[...] = jnp.zeros_like(st_ref)

    xbuf_ref[pad:, :] = xs_ref[...]
    bbuf_ref[pad:, :] = bc_ref[...]

    def conv(buf_ref, w_ref, b_ref):
        acc = b_ref[...] + w_ref[0:1, :] * buf_ref[pl.ds(pad - 3, q), :]
        for k in range(1, SSD_CONV):
            acc = acc + w_ref[k:k + 1, :] * buf_ref[pl.ds(pad - 3 + k, q), :]
        return _silu(acc)

    xs = conv(xbuf_ref, cwx_ref, cbx_ref)
    bc = conv(bbuf_ref, cwb_ref, cbb_ref)
    xbuf_ref[0:pad, :] = xbuf_ref[q:q + pad, :]
    bbuf_ref[0:pad, :] = bbuf_ref[q:q + pad, :]

    dtr = dt_ref[...] + dtb_ref[...]
    dt = jnp.maximum(dtr, 0.0) + jnp.log1p(jnp.exp(-jnp.abs(dtr)))
    adt = dt * a_ref[...]
    ri = lax.broadcasted_iota(jnp.int32, (q, q), 0)
    ci = lax.broadcasted_iota(jnp.int32, (q, q), 1)
    causal = ri >= ci
    tri = causal.astype(F32)
    cs = jnp.dot(tri, adt, preferred_element_type=F32, precision=HI)
    eye = (ri == ci).astype(F32)
    cs_rows = _mm_nt(eye, cs, precision=HI)
    dt_rows = _mm_nt(eye, dt, precision=HI)
    tot = cs[q - 1:q, :]
    w_col = dt * jnp.exp(tot - cs)
    din = jnp.exp(cs)
    w_x = _mm(_split_hi_lo(w_col), e2_ref[...])
    din_x = _mm(_split_hi_lo(din), e2_ref[...])

    lane = lax.broadcasted_iota(jnp.int32, (q, LANES), 1)
    low = lane < SSD_HEADDIM

    xw = (xs * w_x).astype(MXU_DTYPE)
    xsb = xs.astype(MXU_DTYPE)
    for g in range(SSD_GROUPS):
        bg = bc[:, g * SSD_STATE:(g + 1) * SSD_STATE].astype(MXU_DTYPE)
        cg = bc[:, SSD_GROUPS * SSD_STATE + g * SSD_STATE:
                SSD_GROUPS * SSD_STATE + (g + 1) * SSD_STATE].astype(MXU_DTYPE)
        cb = _mm_nt(cg, bg)
        for pair in range(SSD_HEADS_PER_GROUP // 2):
            h0 = g * SSD_HEADS_PER_GROUP + 2 * pair
            ms = []
            for h in (h0, h0 + 1):
                seg = cs[:, h:h + 1] - cs_rows[h:h + 1, :]
                lmat = jnp.exp(jnp.where(causal, seg, -jnp.inf))
                ms.append((cb * lmat * dt_rows[h:h + 1, :]).astype(MXU_DTYPE))
            lhs = jnp.concatenate(ms, axis=1)
            xp = xsb[:, h0 * SSD_HEADDIM:(h0 + 2) * SSD_HEADDIM]
            zero = jnp.zeros_like(xp)
            rhs = jnp.concatenate([jnp.where(low, xp, zero), jnp.where(low, zero, xp)], axis=0)
            yd_ref[:, h0 * SSD_HEADDIM:(h0 + 2) * SSD_HEADDIM] = _mm(lhs, rhs)
        gsl = slice(g * SSD_GROUP_DIM, (g + 1) * SSD_GROUP_DIM)
        st = st_ref[g]
        y_off = _mm(cg, st.astype(MXU_DTYPE)) * din_x[:, gsl]
        yd_ref[:, gsl] = yd_ref[:, gsl] + y_off
        st_ref[g] = din_x[q - 1:q, gsl] * st + _mm_tn(bg, xw[:, gsl])

    y = yd_ref[...] + dsk_ref[...] * xs
    y = y * _silu(z_ref[...])
    for g in range(SSD_GROUPS):
        gsl = slice(g * SSD_GROUP_DIM, (g + 1) * SSD_GROUP_DIM)
        yg = y[:, gsl]
        yg = yg * lax.rsqrt(jnp.mean(yg * yg, axis=-1, keepdims=True) + LN_EPS)
        o_ref[:, gsl] = (yg * nw_ref[:, gsl]).astype(o_ref.dtype)


def _ssd_core(proj, batch, conv_w, conv_b, dt_bias, a_log, d_skip, norm_w):
    m = proj.shape[0]
    q = SSD_CHUNK
    ncb = m // batch // q
    di = SSD_D_INNER
    cw = conv_w.astype(F32)
    cbias = conv_b.astype(F32)[None, :]
    padh = LANES - SSD_HEADS
    dtb = jnp.pad(dt_bias.astype(F32), (0, padh))[None, :]
    a = jnp.pad(-jnp.exp(a_log.astype(F32)), (0, padh))[None, :]
    dsk = jnp.repeat(d_skip.astype(F32), SSD_HEADDIM)[None, :]
    nw = norm_w.astype(F32)[None, :]
    expand = (jnp.arange(LANES)[:, None] == (jnp.arange(di) // SSD_HEADDIM)[None, :]).astype(MXU_DTYPE)
    e2 = jnp.concatenate([expand, expand], axis=0)

    row = lambda b, c: b * ncb + c
    full = lambda shape: pl.BlockSpec(shape, lambda b, c: (0, 0))
    return pl.pallas_call(
        _ssd_core_kernel,
        grid=(batch, ncb),
        in_specs=[pl.BlockSpec((q, di), lambda b, c: (row(b, c), 0)),
                  pl.BlockSpec((q, di), lambda b, c: (row(b, c), 1)),
                  pl.BlockSpec((q, SSD_BC), lambda b, c: (row(b, c), 2 * di // SSD_BC)),
                  pl.BlockSpec((q, LANES), lambda b, c: (row(b, c), (2 * di + SSD_BC) // LANES)),
                  full((SSD_CONV, di)), full((SSD_CONV, SSD_BC)), full((1, di)), full((1, SSD_BC)),
                  full((1, LANES)), full((1, LANES)), full((1, di)), full((1, di)),
                  full((2 * LANES, di))],
        out_specs=pl.BlockSpec((q, di), lambda b, c: (row(b, c), 0)),
        out_shape=jax.ShapeDtypeStruct((m, di), MXU_DTYPE),
        scratch_shapes=[pltpu.VMEM((q + 8, di), F32), pltpu.VMEM((q + 8, SSD_BC), F32),
                        pltpu.VMEM((SSD_GROUPS, SSD_STATE, SSD_GROUP_DIM), F32),
                        pltpu.VMEM((q, di), F32)],
        compiler_params=_params("parallel", "arbitrary"),
        name="ssd_core",
    )(proj, proj, proj, proj, cw[:, :di], cw[:, di:], cbias[:, :di], cbias[:, di:], dtb, a, dsk, nw, e2)


def _ssd_layer(x, batch, w_in, conv_w, conv_b, dt_bias, a_log, d_skip, norm_w, w_out, g, b):
    w_pad = jnp.pad(w_in.astype(MXU_DTYPE), ((0, 0), (0, SSD_IN_PAD - SSD_IN_DIM)))
    proj = _proj(x, w_pad, tm=min(1024, x.shape[0]), tn=SSD_IN_PAD // 9)
    y = _ssd_core(proj, batch, conv_w, conv_b, dt_bias, a_log, d_skip, norm_w)
    return _out_ln(y, w_out.astype(MXU_DTYPE), x, g, b, tm=512, tk=2048)


def _ret_core_kernel(cd_ref, q_ref, k_ref, v_ref, g_ref, cos_ref, sin_ref, dmat_ref, xi_ref, zeta_ref,
                     o_ref, st_ref):
    @pl.when(pl.program_id(1) == 0)
    def _():
        st_ref[...] = jnp.zeros_like(st_ref)

    cos = cos_ref[...]
    sin = sin_ref[...]
    half = RET_DK // 2

    def rotate(t):
        t1 = t[:, :half]
        t2 = t[:, half:]
        return jnp.concatenate([t1 * cos - t2 * sin, t1 * sin + t2 * cos], axis=1)

    for h in range(RET_HEADS):
        qh = rotate(q_ref[:, h * RET_DK:(h + 1) * RET_DK])
        kh = rotate(k_ref[:, h * RET_DK:(h + 1) * RET_DK] * (RET_DK ** -0.5))
        vh = v_ref[:, h * RET_DV:(h + 1) * RET_DV].astype(MXU_DTYPE)
        qb = qh.astype(MXU_DTYPE)
        scores = _mm_nt(qb, kh.astype(MXU_DTYPE)) * dmat_ref[h]
        inner = _mm(scores.astype(MXU_DTYPE), vh)
        st = st_ref[h]
        xi = jnp.concatenate([xi_ref[h]] * (RET_DV // LANES), axis=1)
        cross = _mm(qb, st.astype(MXU_DTYPE)) * xi
        kz = kh * jnp.concatenate([zeta_ref[h]] * (RET_DK // LANES), axis=1)
        st_ref[h] = cd_ref[h] * st + _mm_tn(kz.astype(MXU_DTYPE), vh)
        out = inner + cross
        mu = jnp.mean(out, axis=-1, keepdims=True)
        d = out - mu
        var = jnp.mean(d * d, axis=-1, keepdims=True)
        out = d * lax.rsqrt(var + LN_EPS)
        gate = _silu(g_ref[:, h * RET_DV:(h + 1) * RET_DV])
        o_ref[:, h * RET_DV:(h + 1) * RET_DV] = (gate * out).astype(o_ref.dtype)


def _ret_core(proj, batch):
    m = proj.shape[0]
    seq = m // batch
    q = RET_CHUNK
    ncb = seq // q
    theta = 1.0 / (RET_ROPE_BASE ** jnp.linspace(0.0, 1.0, RET_DK // 2, dtype=F32))
    ang = jnp.arange(seq, dtype=F32)[:, None] * theta[None, :]
    cos, sin = jnp.cos(ang), jnp.sin(ang)
    lg = jnp.log1p(-jnp.exp2(-5.0 - jnp.arange(RET_HEADS, dtype=F32)))
    pos = jnp.arange(q, dtype=F32)
    diff = pos[:, None] - pos[None, :]
    dmat = jnp.where(diff >= 0, jnp.exp(jnp.maximum(diff, 0.0)[None] * lg[:, None, None]), 0.0)
    xi = jnp.exp((pos[None, :] + 1.0) * lg[:, None])
    zeta = jnp.exp((q - 1.0 - pos[None, :]) * lg[:, None])
    xi_b = jnp.broadcast_to(xi[:, :, None], (RET_HEADS, q, LANES))
    zeta_b = jnp.broadcast_to(zeta[:, :, None], (RET_HEADS, q, LANES))
    chunk_decay = jnp.exp(q * lg)

    row = lambda b, c: b * ncb + c
    d = D_MODEL
    return pl.pallas_call(
        _ret_core_kernel,
        grid=(batch, ncb),
        in_specs=[pl.BlockSpec(memory_space=pltpu.SMEM),
                  pl.BlockSpec((q, d), lambda b, c: (row(b, c), 0)),
                  pl.BlockSpec((q, d), lambda b, c: (row(b, c), 1)),
                  pl.BlockSpec((q, 2 * d), lambda b, c: (row(b, c), 1)),
                  pl.BlockSpec((q, 2 * d), lambda b, c: (row(b, c), 2)),
                  pl.BlockSpec((q, RET_DK // 2), lambda b, c: (c, 0)),
                  pl.BlockSpec((q, RET_DK // 2), lambda b, c: (c, 0)),
                  pl.BlockSpec((RET_HEADS, q, q), lambda b, c: (0, 0, 0)),
                  pl.BlockSpec((RET_HEADS, q, LANES), lambda b, c: (0, 0, 0)),
                  pl.BlockSpec((RET_HEADS, q, LANES), lambda b, c: (0, 0, 0))],
        out_specs=pl.BlockSpec((q, 2 * d), lambda b, c: (row(b, c), 0)),
        out_shape=jax.ShapeDtypeStruct((m, 2 * d), MXU_DTYPE),
        scratch_shapes=[pltpu.VMEM((RET_HEADS, RET_DK, RET_DV), F32)],
        compiler_params=_params("parallel", "arbitrary"),
        name="ret_core",
    )(chunk_decay, proj, proj, proj, proj, cos, sin, dmat, xi_b, zeta_b)


def _ret_layer(x, batch, w_in, w_out, g, b):
    proj = _proj(x, w_in.astype(MXU_DTYPE), tm=min(1024, x.shape[0]), tn=1024)
    y = _ret_core(proj, batch)
    return _out_ln(y, w_out.astype(MXU_DTYPE), x, g, b, tm=512, tk=2048)


def kernel(x, ln1_g, ln1_b, ln2_g, ln2_b, mlp_w1, mlp_w2, s5_w_in, s5_lam_re, s5_lam_im, s5_log_dt, s5_b_re, s5_b_im, s5_c_re, s5_c_im, s5_d, s5_w_out, s5_w_gate, ssd_w_in, ssd_conv_w, ssd_conv_b, ssd_dt_bias, ssd_a_log, ssd_d, ssd_norm_w, ssd_w_out, ret_w_in, ret_w_out):
    batch, seq, d = x.shape
    h = x.reshape(batch * seq, d)
    for i in range(DEPTH):
        kind = i % 3
        j = i // 3
        g1 = ln1_g[i][None, :]
        b1 = ln1_b[i][None, :]
        if kind == 0:
            h = _s5_layer(h, batch, s5_w_in[j], s5_lam_re[j], s5_lam_im[j], s5_log_dt[j], s5_b_re[j],
                          s5_b_im[j], s5_c_re[j], s5_c_im[j], s5_d[j], s5_w_out[j], s5_w_gate[j], g1, b1)
        elif kind == 1:
            h = _ssd_layer(h, batch, ssd_w_in[j], ssd_conv_w[j], ssd_conv_b[j], ssd_dt_bias[j], ssd_a_log[j],
                           ssd_d[j], ssd_norm_w[j], ssd_w_out[j], g1, b1)
        else:
            h = _ret_layer(h, batch, ret_w_in[j], ret_w_out[j], g1, b1)
        h = _mlp(h, mlp_w1[i].astype(MXU_DTYPE), mlp_w2[i].astype(MXU_DTYPE),
                 ln2_g[i][None, :], ln2_b[i][None, :], tm=512, tf=1024)
    return h.reshape(batch, seq, d)
```

```python
import functools
import math

import jax
import jax.numpy as jnp
from jax import lax
from jax.experimental import pallas as pl
from jax.experimental.pallas import tpu as pltpu

F32 = jnp.float32
MXU_DTYPE = jnp.bfloat16
HI = lax.Precision.HIGHEST

LANES = 128
VMEM_LIMIT_BYTES = 58 * 1024 * 1024

D_MODEL = 2048
DEPTH = 4
DN_ALPHA = (2 * DEPTH) ** 0.25
LN_EPS = 1e-5
D_FF = 4 * D_MODEL
MLP_OUT_SLABS = 4

S5_GROUP = 16
S5_GROUPS = D_MODEL // S5_GROUP
S5_STATE = 64
S5_CHUNK = 16
S5_SETS = D_MODEL // LANES
S5_SET_GROUPS = LANES // S5_GROUP
S5_SET_STATE = S5_SET_GROUPS * S5_STATE

SSD_D_INNER = 2 * D_MODEL
SSD_HEADDIM = 64
SSD_HEADS = SSD_D_INNER // SSD_HEADDIM
SSD_GROUPS = 8
SSD_STATE = 128
SSD_CONV = 4
SSD_CHUNK = 128
SSD_BC = 2 * SSD_GROUPS * SSD_STATE
SSD_CONV_DIM = SSD_D_INNER + SSD_BC
SSD_IN_DIM = SSD_D_INNER + SSD_CONV_DIM + SSD_HEADS
SSD_IN_PAD = SSD_D_INNER + SSD_CONV_DIM + LANES
SSD_GROUP_DIM = SSD_D_INNER // SSD_GROUPS
SSD_HEADS_PER_GROUP = SSD_HEADS // SSD_GROUPS

RET_HEADS = 8
RET_DK = D_MODEL // RET_HEADS
RET_DV = 2 * D_MODEL // RET_HEADS
RET_CHUNK = 128
RET_IN_DIM = 6 * D_MODEL
RET_ROPE_BASE = 10000.0


def _params(*semantics):
    return pltpu.CompilerParams(dimension_semantics=semantics, vmem_limit_bytes=VMEM_LIMIT_BYTES)


def _layer_norm(v, g, b):
    mu = jnp.mean(v, axis=-1, keepdims=True)
    d = v - mu
    var = jnp.mean(d * d, axis=-1, keepdims=True)
    return d * lax.rsqrt(var + LN_EPS) * g + b


def _silu(v):
    return v * jax.nn.sigmoid(v)


def _mm(a, b):
    return jnp.dot(a, b, preferred_element_type=F32)


def _mm_nt(a, b, precision=None):
    return lax.dot_general(a, b, (((1,), (1,)), ((), ())), preferred_element_type=F32, precision=precision)


def _mm_tn(a, b):
    return lax.dot_general(a, b, (((0,), (0,)), ((), ())), preferred_element_type=F32)


def _proj_kernel(x_ref, w_ref, o_ref, xb_ref):
    @pl.when(pl.program_id(1) == 0)
    def _():
        xb_ref[...] = x_ref[...].astype(MXU_DTYPE)

    o_ref[...] = _mm(xb_ref[...], w_ref[...].astype(MXU_DTYPE))


def _proj(x, w_stack, layer, n, tm, tn):
    m, k = x.shape
    return pl.pallas_call(
        _proj_kernel,
        grid=(m // tm, n // tn),
        in_specs=[pl.BlockSpec((tm, k), lambda i, j: (i, 0)),
                  pl.BlockSpec((None, k, tn), lambda i, j: (layer, 0, j))],
        out_specs=pl.BlockSpec((tm, tn), lambda i, j: (i, j)),
        out_shape=jax.ShapeDtypeStruct((m, n), F32),
        scratch_shapes=[pltpu.VMEM((tm, k), MXU_DTYPE)],
        compiler_params=_params("parallel", "arbitrary"),
        name="proj",
    )(x, w_stack)


def _out_ln_kernel(y_ref, w_ref, x_ref, g_ref, b_ref, o_ref):
    f = _mm(y_ref[...], w_ref[...])
    o_ref[...] = _layer_norm(DN_ALPHA * x_ref[...] + f, g_ref[...], b_ref[...])


def _resident(shape):
    return pl.BlockSpec(shape, lambda *_: (0,) * len(shape), pipeline_mode=pl.Buffered(1))


def _out_ln(y, w, x, g, b, tm):
    m, k = y.shape
    d = w.shape[1]
    return pl.pallas_call(
        _out_ln_kernel,
        grid=(m // tm,),
        in_specs=[pl.BlockSpec((tm, k), lambda i: (i, 0)),
                  _resident((k, d)),
                  pl.BlockSpec((tm, d), lambda i: (i, 0)),
                  _resident((1, d)),
                  _resident((1, d))],
        out_specs=pl.BlockSpec((tm, d), lambda i: (i, 0)),
        out_shape=jax.ShapeDtypeStruct((m, d), F32),
        compiler_params=_params("parallel"),
        name="out_ln",
    )(y, w, x, g, b)


def _mlp_kernel(x_ref, w1_ref, w2_ref, g_ref, b_ref, o_ref, xb_ref, *, nf):
    j = pl.program_id(1)

    @pl.when(j == 0)
    def _():
        xb_ref[...] = x_ref[...].astype(MXU_DTYPE)

    h = _mm(xb_ref[...], w1_ref[...].astype(MXU_DTYPE))
    h = jnp.square(jnp.maximum(h, 0.0)).astype(MXU_DTYPE)
    d = o_ref.shape[1]
    slab = d // MLP_OUT_SLABS
    for s in range(MLP_OUT_SLABS):
        cols = slice(s * slab, (s + 1) * slab)
        f = _mm(h, w2_ref[:, cols].astype(MXU_DTYPE))

        @pl.when(j == 0)
        def _():
            o_ref[:, cols] = f

        @pl.when(j > 0)
        def _():
            o_ref[:, cols] += f

    @pl.when(j == nf - 1)
    def _():
        o_ref[...] = _layer_norm(DN_ALPHA * x_ref[...] + o_ref[...], g_ref[...], b_ref[...])


def _mlp(x, w1_stack, w2_stack, layer, g, b, tm, tf):
    m, d = x.shape
    f = w1_stack.shape[2]
    nf = f // tf
    return pl.pallas_call(
        functools.partial(_mlp_kernel, nf=nf),
        grid=(m // tm, nf),
        in_specs=[pl.BlockSpec((tm, d), lambda i, j: (i, 0), pipeline_mode=pl.Buffered(1)),
                  pl.BlockSpec((None, d, tf), lambda i, j: (layer, 0, j)),
                  pl.BlockSpec((None, tf, d), lambda i, j: (layer, j, 0)),
                  _resident((1, d)),
                  _resident((1, d))],
        out_specs=pl.BlockSpec((tm, d), lambda i, j: (i, 0)),
        out_shape=jax.ShapeDtypeStruct((m, d), F32),
        scratch_shapes=[pltpu.VMEM((tm, d), MXU_DTYPE)],
        compiler_params=_params("parallel", "arbitrary"),
        name="mlp",
    )(x, w1_stack, w2_stack, g, b)


def _s5_in_kernel(x_ref, w_ref, o_ref):
    u = _mm(x_ref[...].astype(MXU_DTYPE), w_ref[...])
    for gs in range(S5_SETS):
        o_ref[gs] = u[:, gs * LANES:(gs + 1) * LANES]


def _s5_in(x, w, tm):
    m = x.shape[0]
    return pl.pallas_call(
        _s5_in_kernel,
        grid=(m // tm,),
        in_specs=[pl.BlockSpec((tm, D_MODEL), lambda i: (i, 0)),
                  _resident((D_MODEL, D_MODEL))],
        out_specs=pl.BlockSpec((S5_SETS, tm, LANES), lambda i: (0, i, 0)),
        out_shape=jax.ShapeDtypeStruct((S5_SETS, m, LANES), F32),
        compiler_params=_params("parallel"),
        name="s5_in",
    )(x, w)


def _s5_core_kernel(u_ref, lb_ref, rc_ref, cc_ref, a16_ref, d_ref, o_ref,
                    wt_ref, win_ref, wout_ref, v_ref, sp_ref, *, batch, chunks_per_batch):
    nst = S5_SET_STATE
    kdim = S5_CHUNK * LANES
    lb = lb_ref[0]

    drev = lax.dot_general(lb, rc_ref[0], (((1,), (0,)), ((), ())),
                           preferred_element_type=F32, precision=HI)
    row = lax.broadcasted_iota(jnp.int32, (kdim, LANES), 0)
    lane = lax.broadcasted_iota(jnp.int32, (kdim, LANES), 1)
    row_group = (row % LANES) // S5_GROUP
    drev = jnp.where(row_group == lane // S5_GROUP, drev, 0.0).astype(MXU_DTYPE)
    for t in range(S5_CHUNK):
        used = (t + 1) * LANES
        wt_ref[0:used, t * LANES:(t + 1) * LANES] = drev[kdim - used:, :]
        if used < kdim:
            wt_ref[used:, t * LANES:(t + 1) * LANES] = jnp.zeros((kdim - used, LANES), MXU_DTYPE)

    swapped = pltpu.roll(lb, S5_STATE, axis=1)
    low = lane < S5_STATE
    re_dup = jnp.where(low, lb, swapped)
    im_dup = jnp.where(low, swapped, lb)
    for v in range(nst // LANES):
        sel = row_group == 2 * v + (lane >= S5_STATE).astype(jnp.int32)
        win_ref[:, v * LANES:(v + 1) * LANES] = jnp.where(sel, re_dup, 0.0).astype(MXU_DTYPE)
        win_ref[:, nst + v * LANES:nst + (v + 1) * LANES] = jnp.where(sel, im_dup, 0.0).astype(MXU_DTYPE)

    cc = cc_ref[0]
    lane_group = (lax.broadcasted_iota(jnp.int32, (S5_STATE, kdim), 1) % LANES) // S5_GROUP
    for ri in range(2):
        part = cc[ri * S5_STATE:(ri + 1) * S5_STATE, :]
        for g in range(S5_SET_GROUPS):
            r0 = ri * nst + g * S5_STATE
            wout_ref[r0:r0 + S5_STATE, :] = jnp.where(lane_group == g, part, 0.0).astype(MXU_DTYPE)

    nchunks = batch * chunks_per_batch
    u = jnp.concatenate([u_ref[pl.ds(t, nchunks, stride=S5_CHUNK), :] for t in range(S5_CHUNK)], axis=1)
    xb = u.astype(MXU_DTYPE)
    vin = _mm(xb, win_ref[...])
    ntile = nst // LANES
    for k in range(2 * ntile):
        v_ref[k] = vin[:, k * LANES:(k + 1) * LANES]

    a_re = [a16_ref[0, :, k * LANES:(k + 1) * LANES] for k in range(ntile)]
    a_im = [a16_ref[0, :, nst + k * LANES:nst + (k + 1) * LANES] for k in range(ntile)]

    def step(c, carry):
        rows = pl.ds(c, batch, stride=chunks_per_batch)
        new = []
        for k in range(ntile):
            s_re, s_im = carry[k], carry[ntile + k]
            sp_ref[k, rows, :] = s_re
            sp_ref[ntile + k, rows, :] = s_im
            new.append((a_re[k] * s_re - a_im[k] * s_im + v_ref[k, rows, :],
                        a_re[k] * s_im + a_im[k] * s_re + v_ref[ntile + k, rows, :]))
        return tuple(n[0] for n in new) + tuple(n[1] for n in new)

    zero = jnp.zeros((batch, LANES), F32)
    lax.fori_loop(0, chunks_per_batch, step, (zero,) * (2 * ntile))

    sp = jnp.concatenate([sp_ref[k] for k in range(2 * ntile)], axis=1)
    y = _mm(xb, wt_ref[...]) + _mm(sp.astype(MXU_DTYPE), wout_ref[...])
    y = y + d_ref[0] * u
    h = jax.nn.gelu(y)
    for t in range(S5_CHUNK):
        o_ref[pl.ds(t, nchunks, stride=S5_CHUNK), :] = h[:, t * LANES:(t + 1) * LANES]


def _s5_core(ug, lb, rc, cc, a16, drow, batch):
    nsets, m, _ = ug.shape
    nchunks = m // S5_CHUNK
    kdim = S5_CHUNK * LANES
    kern = functools.partial(_s5_core_kernel, batch=batch, chunks_per_batch=nchunks // batch)
    return pl.pallas_call(
        kern,
        grid=(nsets,),
        in_specs=[pl.BlockSpec((None, m, LANES), lambda s: (s, 0, 0)),
                  pl.BlockSpec((1, kdim, LANES), lambda s: (s, 0, 0)),
                  pl.BlockSpec((1, LANES, LANES), lambda s: (s, 0, 0)),
                  pl.BlockSpec((1, LANES, kdim), lambda s: (s, 0, 0)),
                  pl.BlockSpec((1, 1, 2 * S5_SET_STATE), lambda s: (s, 0, 0)),
                  pl.BlockSpec((1, 1, kdim), lambda s: (s, 0, 0))],
        out_specs=pl.BlockSpec((None, m, LANES), lambda s: (s, 0, 0)),
        out_shape=jax.ShapeDtypeStruct((nsets, m, LANES), F32),
        scratch_shapes=[pltpu.VMEM((kdim, kdim), MXU_DTYPE),
                        pltpu.VMEM((kdim, 2 * S5_SET_STATE), MXU_DTYPE),
                        pltpu.VMEM((2 * S5_SET_STATE, kdim), MXU_DTYPE),
                        pltpu.VMEM((2 * S5_SET_STATE // LANES, nchunks, LANES), F32),
                        pltpu.VMEM((2 * S5_SET_STATE // LANES, nchunks, LANES), F32)],
        compiler_params=_params("parallel"),
        name="s5_core",
    )(ug, lb, rc, cc, a16, drow)


def _s5_out_kernel(h_ref, wo_ref, wg_ref, x_ref, g_ref, b_ref, o_ref):
    h = jnp.concatenate([h_ref[gs] for gs in range(S5_SETS)], axis=1).astype(MXU_DTYPE)
    f = _mm(h, wo_ref[...]) * jax.nn.sigmoid(_mm(h, wg_ref[...]))
    o_ref[...] = _layer_norm(DN_ALPHA * x_ref[...] + f, g_ref[...], b_ref[...])


def _s5_out(hg, wo, wg, x, g, b, tm):
    m = x.shape[0]
    return pl.pallas_call(
        _s5_out_kernel,
        grid=(m // tm,),
        in_specs=[pl.BlockSpec((S5_SETS, tm, LANES), lambda i: (0, i, 0)),
                  _resident((D_MODEL, D_MODEL)),
                  _resident((D_MODEL, D_MODEL)),
                  pl.BlockSpec((tm, D_MODEL), lambda i: (i, 0)),
                  _resident((1, D_MODEL)),
                  _resident((1, D_MODEL))],
        out_specs=pl.BlockSpec((tm, D_MODEL), lambda i: (i, 0)),
        out_shape=jax.ShapeDtypeStruct(x.shape, F32),
        compiler_params=_params("parallel"),
        name="s5_out",
    )(hg, wo, wg, x, g, b)


def _s5_tables(lam_re, lam_im, log_dt, b_re, b_im, c_re, c_im, d_skip):
    lr = lam_re.astype(F32)
    li = lam_im.astype(F32)
    dt = jnp.exp(log_dt.astype(F32))[:, None]
    mag = jnp.exp(lr * dt)
    ar = mag * jnp.cos(li * dt)
    ai = mag * jnp.sin(li * dt)
    den = lr * lr + li * li
    zr = ((ar - 1.0) * lr + ai * li) / den
    zi = (ai * lr - (ar - 1.0) * li) / den
    br_ = b_re.astype(F32)
    bi_ = b_im.astype(F32)
    bbr = zr[..., None] * br_ - zi[..., None] * bi_
    bbi = zr[..., None] * bi_ + zi[..., None] * br_
    k = jnp.arange(S5_CHUNK + 1, dtype=F32)[:, None, None]
    pmag = jnp.exp(k * (lr * dt))
    pr = pmag * jnp.cos(k * (li * dt))
    pi = pmag * jnp.sin(k * (li * dt))

    bt_re = jnp.swapaxes(bbr, 1, 2)
    bt_im = jnp.swapaxes(bbi, 1, 2)
    shape_b = (S5_SETS, 1, S5_SET_GROUPS, S5_GROUP, 2 * S5_STATE)
    b_with_re = jnp.concatenate([bt_re, bt_im], axis=-1).reshape(shape_b)
    b_with_im = jnp.concatenate([-bt_im, bt_re], axis=-1).reshape(shape_b)
    def power_rows(t):
        t = jnp.concatenate([t, t], axis=-1).reshape(S5_CHUNK, S5_SETS, S5_SET_GROUPS, 1, 2 * S5_STATE)
        return t.transpose(1, 0, 2, 3, 4)
    lb = power_rows(pr[S5_CHUNK - 1::-1]) * b_with_re + power_rows(pi[S5_CHUNK - 1::-1]) * b_with_im
    lb = lb.reshape(S5_SETS, S5_CHUNK * LANES, 2 * S5_STATE)

    def rows_p(t):
        return t.astype(F32).reshape(S5_SETS, S5_SET_GROUPS, S5_GROUP, S5_STATE).transpose(0, 3, 1, 2)
    ct_re = rows_p(c_re)
    ct_im = rows_p(c_im)
    rc = jnp.concatenate([ct_re, -ct_im], axis=1).reshape(S5_SETS, 2 * S5_STATE, LANES)

    def power_lanes(t):
        return t.reshape(S5_CHUNK, S5_SETS, S5_SET_GROUPS, S5_STATE).transpose(1, 3, 0, 2)[..., None]
    prs = power_lanes(pr[1:])
    pis = power_lanes(pi[1:])
    ca_re = ct_re[:, :, None] * prs - ct_im[:, :, None] * pis
    ca_im = ct_re[:, :, None] * pis + ct_im[:, :, None] * prs
    cc = jnp.concatenate([ca_re, -ca_im], axis=1).reshape(S5_SETS, 2 * S5_STATE, S5_CHUNK * LANES)

    a16 = jnp.concatenate([pr[S5_CHUNK].reshape(S5_SETS, 1, S5_SET_STATE),
                           pi[S5_CHUNK].reshape(S5_SETS, 1, S5_SET_STATE)], axis=2)
    drow = jnp.tile(d_skip.astype(F32).reshape(S5_SETS, 1, LANES), (1, 1, S5_CHUNK))
    return lb, rc, cc, a16, drow


def _s5_layer(x, batch, w_in, lam_re, lam_im, log_dt, b_re, b_im, c_re, c_im, d_skip, w_out, w_gate, g, b):
    lb, rc, cc, a16, drow = _s5_tables(lam_re, lam_im, log_dt, b_re, b_im, c_re, c_im, d_skip)
    ug = _s5_in(x, w_in.astype(MXU_DTYPE), tm=512)
    hg = _s5_core(ug, lb, rc, cc, a16, drow, batch)
    return _s5_out(hg, w_out.astype(MXU_DTYPE), w_gate.astype(MXU_DTYPE), x, g, b, tm=512)


def _split_hi_lo(v):
    hi = v.astype(MXU_DTYPE)
    lo = (v - hi.astype(F32)).astype(MXU_DTYPE)
    return jnp.concatenate([hi, lo], axis=1)


def _ssd_core_kernel(z_ref, xs_ref, bc_ref, xin_ref, wdt_ref, cwx_ref, cwb_ref, cbx_ref, cbb_ref,
                     dtb_ref, a_ref, dsk_ref, nw_ref, e2_ref, o_ref,
                     xbuf_ref, bbuf_ref, st_ref, yd_ref):
    q = SSD_CHUNK
    pad = 8

    @pl.when(pl.program_id(1) == 0)
    def _():
        xbuf_ref[0:pad, :] = jnp.zeros((pad, SSD_D_INNER), F32)
        bbuf_ref[0:pad, :] = jnp.zeros((pad, SSD_BC), F32)
        st_ref[...] = jnp.zeros_like(st_ref)

    xbuf_ref[pad:, :] = xs_ref[...]
    bbuf_ref[pad:, :] = bc_ref[...]

    def conv(buf_ref, w_ref, b_ref):
        acc = b_ref[...] + w_ref[0:1, :] * buf_ref[pl.ds(pad - 3, q), :]
        for k in range(1, SSD_CONV):
            acc = acc + w_ref[k:k + 1, :] * buf_ref[pl.ds(pad - 3 + k, q), :]
        return _silu(acc)

    xs = conv(xbuf_ref, cwx_ref, cbx_ref)
    bc = conv(bbuf_ref, cwb_ref, cbb_ref)
    xbuf_ref[0:pad, :] = xbuf_ref[q:q + pad, :]
    bbuf_ref[0:pad, :] = bbuf_ref[q:q + pad, :]

    dtr = _mm(xin_ref[...].astype(MXU_DTYPE), wdt_ref[...]) + dtb_ref[...]
    dt = jnp.maximum(dtr, 0.0) + jnp.log1p(jnp.exp(-jnp.abs(dtr)))
    adt = dt * a_ref[...]
    ri = lax.broadcasted_iota(jnp.int32, (q, q), 0)
    ci = lax.broadcasted_iota(jnp.int32, (q, q), 1)
    causal = ri >= ci
    tri = causal.astype(F32)
    cs = jnp.dot(tri, adt, preferred_element_type=F32, precision=HI)
    eye = (ri == ci).astype(F32)
    cs_rows = _mm_nt(eye, cs, precision=HI)
    dt_rows = _mm_nt(eye, dt, precision=HI)
    tot = cs[q - 1:q, :]
    w_col = dt * jnp.exp(tot - cs)
    din = jnp.exp(cs)
    w_x = _mm(_split_hi_lo(w_col), e2_ref[...])
    din_x = _mm(_split_hi_lo(din), e2_ref[...])

    lane = lax.broadcasted_iota(jnp.int32, (q, LANES), 1)
    low = lane < SSD_HEADDIM

    xw = (xs * w_x).astype(MXU_DTYPE)
    xsb = xs.astype(MXU_DTYPE)
    for g in range(SSD_GROUPS):
        bg = bc[:, g * SSD_STATE:(g + 1) * SSD_STATE].astype(MXU_DTYPE)
        cg = bc[:, SSD_GROUPS * SSD_STATE + g * SSD_STATE:
                SSD_GROUPS * SSD_STATE + (g + 1) * SSD_STATE].astype(MXU_DTYPE)
        cb = _mm_nt(cg, bg)
        for pair in range(SSD_HEADS_PER_GROUP // 2):
            h0 = g * SSD_HEADS_PER_GROUP + 2 * pair
            ms = []
            for h in (h0, h0 + 1):
                seg = cs[:, h:h + 1] - cs_rows[h:h + 1, :]
                lmat = jnp.exp(jnp.where(causal, seg, -jnp.inf))
                ms.append((cb * lmat * dt_rows[h:h + 1, :]).astype(MXU_DTYPE))
            lhs = jnp.concatenate(ms, axis=1)
            xp = xsb[:, h0 * SSD_HEADDIM:(h0 + 2) * SSD_HEADDIM]
            zero = jnp.zeros_like(xp)
            rhs = jnp.concatenate([jnp.where(low, xp, zero), jnp.where(low, zero, xp)], axis=0)
            yd_ref[:, h0 * SSD_HEADDIM:(h0 + 2) * SSD_HEADDIM] = _mm(lhs, rhs)
        gsl = slice(g * SSD_GROUP_DIM, (g + 1) * SSD_GROUP_DIM)
        st = st_ref[g]
        y_off = _mm(cg, st.astype(MXU_DTYPE)) * din_x[:, gsl]
        yd_ref[:, gsl] = yd_ref[:, gsl] + y_off
        st_ref[g] = din_x[q - 1:q, gsl] * st + _mm_tn(bg, xw[:, gsl])

    y = yd_ref[...] + dsk_ref[...] * xs
    y = y * _silu(z_ref[...])
    for g in range(SSD_GROUPS):
        gsl = slice(g * SSD_GROUP_DIM, (g + 1) * SSD_GROUP_DIM)
        yg = y[:, gsl]
        yg = yg * lax.rsqrt(jnp.mean(yg * yg, axis=-1, keepdims=True) + LN_EPS)
        o_ref[:, gsl] = (yg * nw_ref[:, gsl]).astype(o_ref.dtype)


def _ssd_core(proj, x, w_dt, batch, conv_w, conv_b, dt_bias, a_log, d_skip, norm_w):
    m = proj.shape[0]
    q = SSD_CHUNK
    ncb = m // batch // q
    di = SSD_D_INNER
    wdt = jnp.pad(w_dt.astype(MXU_DTYPE), ((0, 0), (0, LANES - SSD_HEADS)))
    cw = conv_w.astype(F32)
    cbias = conv_b.astype(F32)[None, :]
    padh = LANES - SSD_HEADS
    dtb = jnp.pad(dt_bias.astype(F32), (0, padh))[None, :]
    a = jnp.pad(-jnp.exp(a_log.astype(F32)), (0, padh))[None, :]
    dsk = jnp.repeat(d_skip.astype(F32), SSD_HEADDIM)[None, :]
    nw = norm_w.astype(F32)[None, :]
    expand = (jnp.arange(LANES)[:, None] == (jnp.arange(di) // SSD_HEADDIM)[None, :]).astype(MXU_DTYPE)
    e2 = jnp.concatenate([expand, expand], axis=0)

    row = lambda b, c: b * ncb + c
    full = _resident
    return pl.pallas_call(
        _ssd_core_kernel,
        grid=(batch, ncb),
        in_specs=[pl.BlockSpec((q, di), lambda b, c: (row(b, c), 0)),
                  pl.BlockSpec((q, di), lambda b, c: (row(b, c), 1)),
                  pl.BlockSpec((q, SSD_BC), lambda b, c: (row(b, c), 2 * di // SSD_BC)),
                  pl.BlockSpec((q, D_MODEL), lambda b, c: (row(b, c), 0)),
                  full((D_MODEL, LANES)),
                  full((SSD_CONV, di)), full((SSD_CONV, SSD_BC)), full((1, di)), full((1, SSD_BC)),
                  full((1, LANES)), full((1, LANES)), full((1, di)), full((1, di)),
                  full((2 * LANES, di))],
        out_specs=pl.BlockSpec((q, di), lambda b, c: (row(b, c), 0)),
        out_shape=jax.ShapeDtypeStruct((m, di), MXU_DTYPE),
        scratch_shapes=[pltpu.VMEM((q + 8, di), F32), pltpu.VMEM((q + 8, SSD_BC), F32),
                        pltpu.VMEM((SSD_GROUPS, SSD_STATE, SSD_GROUP_DIM), F32),
                        pltpu.VMEM((q, di), F32)],
        compiler_params=_params("parallel", "arbitrary"),
        name="ssd_core",
    )(proj, proj, proj, x, wdt, cw[:, :di], cw[:, di:], cbias[:, :di], cbias[:, di:], dtb, a, dsk, nw, e2)


def _ssd_layer(x, batch, w_in_stack, layer, conv_w, conv_b, dt_bias, a_log, d_skip, norm_w, w_out, g, b):
    n_main = SSD_D_INNER + SSD_CONV_DIM
    proj = _proj(x, w_in_stack, layer, n_main, tm=min(1024, x.shape[0]), tn=1024)
    w_dt = w_in_stack[layer, :, n_main:]
    y = _ssd_core(proj, x, w_dt, batch, conv_w, conv_b, dt_bias, a_log, d_skip, norm_w)
    return _out_ln(y, w_out.astype(MXU_DTYPE), x, g, b, tm=512)


def _ret_core_kernel(cd_ref, q_ref, k_ref, v_ref, g_ref, cos_ref, sin_ref, dmat_ref, xi_ref, zeta_ref,
                     o_ref, st_ref):
    @pl.when(pl.program_id(1) == 0)
    def _():
        st_ref[...] = jnp.zeros_like(st_ref)

    cos = cos_ref[...]
    sin = sin_ref[...]
    half = RET_DK // 2

    def rotate(t):
        t1 = t[:, :half]
        t2 = t[:, half:]
        return jnp.concatenate([t1 * cos - t2 * sin, t1 * sin + t2 * cos], axis=1)

    for h in range(RET_HEADS):
        qh = rotate(q_ref[:, h * RET_DK:(h + 1) * RET_DK])
        kh = rotate(k_ref[:, h * RET_DK:(h + 1) * RET_DK] * (RET_DK ** -0.5))
        vh = v_ref[:, h * RET_DV:(h + 1) * RET_DV].astype(MXU_DTYPE)
        qb = qh.astype(MXU_DTYPE)
        scores = _mm_nt(qb, kh.astype(MXU_DTYPE)) * dmat_ref[h]
        inner = _mm(scores.astype(MXU_DTYPE), vh)
        st = st_ref[h]
        xi = jnp.concatenate([xi_ref[h]] * (RET_DV // LANES), axis=1)
        cross = _mm(qb, st.astype(MXU_DTYPE)) * xi
        kz = kh * jnp.concatenate([zeta_ref[h]] * (RET_DK // LANES), axis=1)
        st_ref[h] = cd_ref[h] * st + _mm_tn(kz.astype(MXU_DTYPE), vh)
        out = inner + cross
        mu = jnp.mean(out, axis=-1, keepdims=True)
        d = out - mu
        var = jnp.mean(d * d, axis=-1, keepdims=True)
        out = d * lax.rsqrt(var + LN_EPS)
        gate = _silu(g_ref[:, h * RET_DV:(h + 1) * RET_DV])
        o_ref[:, h * RET_DV:(h + 1) * RET_DV] = (gate * out).astype(o_ref.dtype)


def _ret_core(proj, batch):
    m = proj.shape[0]
    seq = m // batch
    q = RET_CHUNK
    ncb = seq // q
    theta = 1.0 / (RET_ROPE_BASE ** jnp.linspace(0.0, 1.0, RET_DK // 2, dtype=F32))
    ang = jnp.arange(seq, dtype=F32)[:, None] * theta[None, :]
    cos, sin = jnp.cos(ang), jnp.sin(ang)
    lg = jnp.log1p(-jnp.exp2(-5.0 - jnp.arange(RET_HEADS, dtype=F32)))
    pos = jnp.arange(q, dtype=F32)
    diff = pos[:, None] - pos[None, :]
    dmat = jnp.where(diff >= 0, jnp.exp(jnp.maximum(diff, 0.0)[None] * lg[:, None, None]), 0.0)
    xi = jnp.exp((pos[None, :] + 1.0) * lg[:, None])
    zeta = jnp.exp((q - 1.0 - pos[None, :]) * lg[:, None])
    xi_b = jnp.broadcast_to(xi[:, :, None], (RET_HEADS, q, LANES))
    zeta_b = jnp.broadcast_to(zeta[:, :, None], (RET_HEADS, q, LANES))
    chunk_decay = jnp.exp(q * lg)

    row = lambda b, c: b * ncb + c
    d = D_MODEL
    return pl.pallas_call(
        _ret_core_kernel,
        grid=(batch, ncb),
        in_specs=[pl.BlockSpec(memory_space=pltpu.SMEM),
                  pl.BlockSpec((q, d), lambda b, c: (row(b, c), 0)),
                  pl.BlockSpec((q, d), lambda b, c: (row(b, c), 1)),
                  pl.BlockSpec((q, 2 * d), lambda b, c: (row(b, c), 1)),
                  pl.BlockSpec((q, 2 * d), lambda b, c: (row(b, c), 2)),
                  pl.BlockSpec((q, RET_DK // 2), lambda b, c: (c, 0)),
                  pl.BlockSpec((q, RET_DK // 2), lambda b, c: (c, 0)),
                  pl.BlockSpec((RET_HEADS, q, q), lambda b, c: (0, 0, 0)),
                  pl.BlockSpec((RET_HEADS, q, LANES), lambda b, c: (0, 0, 0)),
                  pl.BlockSpec((RET_HEADS, q, LANES), lambda b, c: (0, 0, 0))],
        out_specs=pl.BlockSpec((q, 2 * d), lambda b, c: (row(b, c), 0)),
        out_shape=jax.ShapeDtypeStruct((m, 2 * d), MXU_DTYPE),
        scratch_shapes=[pltpu.VMEM((RET_HEADS, RET_DK, RET_DV), F32)],
        compiler_params=_params("parallel", "arbitrary"),
        name="ret_core",
    )(chunk_decay, proj, proj, proj, proj, cos, sin, dmat, xi_b, zeta_b)


def _ret_layer(x, batch, w_in_stack, layer, w_out, g, b):
    proj = _proj(x, w_in_stack, layer, RET_IN_DIM, tm=min(1024, x.shape[0]), tn=1024)
    y = _ret_core(proj, batch)
    return _out_ln(y, w_out.astype(MXU_DTYPE), x, g, b, tm=512)


def kernel(x, ln1_g, ln1_b, ln2_g, ln2_b, mlp_w1, mlp_w2, s5_w_in, s5_lam_re, s5_lam_im, s5_log_dt, s5_b_re, s5_b_im, s5_c_re, s5_c_im, s5_d, s5_w_out, s5_w_gate, ssd_w_in, ssd_conv_w, ssd_conv_b, ssd_dt_bias, ssd_a_log, ssd_d, ssd_norm_w, ssd_w_out, ret_w_in, ret_w_out):
    batch, seq, d = x.shape
    h = x.reshape(batch * seq, d)
    for i in range(DEPTH):
        kind = i % 3
        j = i // 3
        g1 = ln1_g[i][None, :]
        b1 = ln1_b[i][None, :]
        if kind == 0:
            h = _s5_layer(h, batch, s5_w_in[j], s5_lam_re[j], s5_lam_im[j], s5_log_dt[j], s5_b_re[j],
                          s5_b_im[j], s5_c_re[j], s5_c_im[j], s5_d[j], s5_w_out[j], s5_w_gate[j], g1, b1)
        elif kind == 1:
            h = _ssd_layer(h, batch, ssd_w_in, j, ssd_conv_w[j], ssd_conv_b[j], ssd_dt_bias[j], ssd_a_log[j],
                           ssd_d[j], ssd_norm_w[j], ssd_w_out[j], g1, b1)
        else:
            h = _ret_layer(h, batch, ret_w_in, j, ret_w_out[j], g1, b1)
        h = _mlp(h, mlp_w1, mlp_w2, i, ln2_g[i][None, :], ln2_b[i][None, :],
                 tm=min(1024, h.shape[0]), tf=512)
    return h.reshape(batch, seq, d)
```

```python
import functools
import math

import jax
import jax.numpy as jnp
from jax import lax
from jax.experimental import pallas as pl
from jax.experimental.pallas import tpu as pltpu

F32 = jnp.float32
MXU_DTYPE = jnp.bfloat16
HI = lax.Precision.HIGHEST

LANES = 128
VMEM_LIMIT_BYTES = 58 * 1024 * 1024

D_MODEL = 2048
DEPTH = 4
DN_ALPHA = (2 * DEPTH) ** 0.25
LN_EPS = 1e-5
D_FF = 4 * D_MODEL
MLP_OUT_SLABS = 4

S5_GROUP = 16
S5_GROUPS = D_MODEL // S5_GROUP
S5_STATE = 64
S5_CHUNK = 16
S5_SETS = D_MODEL // LANES
S5_SET_GROUPS = LANES // S5_GROUP
S5_SET_STATE = S5_SET_GROUPS * S5_STATE
S5_SCAN_UNROLL = 8

SSD_D_INNER = 2 * D_MODEL
SSD_HEADDIM = 64
SSD_HEADS = SSD_D_INNER // SSD_HEADDIM
SSD_GROUPS = 8
SSD_STATE = 128
SSD_CONV = 4
SSD_CHUNK = 128
SSD_BC = 2 * SSD_GROUPS * SSD_STATE
SSD_CONV_DIM = SSD_D_INNER + SSD_BC
SSD_IN_DIM = SSD_D_INNER + SSD_CONV_DIM + SSD_HEADS
SSD_IN_PAD = SSD_D_INNER + SSD_CONV_DIM + LANES
SSD_GROUP_DIM = SSD_D_INNER // SSD_GROUPS
SSD_HEADS_PER_GROUP = SSD_HEADS // SSD_GROUPS

RET_HEADS = 8
RET_DK = D_MODEL // RET_HEADS
RET_DV = 2 * D_MODEL // RET_HEADS
RET_CHUNK = 128
RET_IN_DIM = 6 * D_MODEL
RET_ROPE_BASE = 10000.0


def _params(*semantics):
    return pltpu.CompilerParams(dimension_semantics=semantics, vmem_limit_bytes=VMEM_LIMIT_BYTES)


def _layer_norm(v, g, b):
    mu = jnp.mean(v, axis=-1, keepdims=True)
    d = v - mu
    var = jnp.mean(d * d, axis=-1, keepdims=True)
    return d * lax.rsqrt(var + LN_EPS) * g + b


def _silu(v):
    return v * jax.nn.sigmoid(v)


def _mm(a, b):
    return jnp.dot(a, b, preferred_element_type=F32)


def _mm_nt(a, b, precision=None):
    return lax.dot_general(a, b, (((1,), (1,)), ((), ())), preferred_element_type=F32, precision=precision)


def _mm_tn(a, b):
    return lax.dot_general(a, b, (((0,), (0,)), ((), ())), preferred_element_type=F32)


def _proj_kernel(x_ref, w_ref, o_ref, xb_ref):
    @pl.when(pl.program_id(1) == 0)
    def _():
        xb_ref[...] = x_ref[...].astype(MXU_DTYPE)

    o_ref[...] = _mm(xb_ref[...], w_ref[...].astype(MXU_DTYPE)).astype(o_ref.dtype)


def _proj(x, w_stack, layer, n, tm, tn, out_dtype):
    m, k = x.shape
    return pl.pallas_call(
        _proj_kernel,
        grid=(m // tm, n // tn),
        in_specs=[pl.BlockSpec((tm, k), lambda i, j: (i, 0)),
                  pl.BlockSpec((None, k, tn), lambda i, j: (layer, 0, j))],
        out_specs=pl.BlockSpec((tm, tn), lambda i, j: (i, j)),
        out_shape=jax.ShapeDtypeStruct((m, n), out_dtype),
        scratch_shapes=[pltpu.VMEM((tm, k), MXU_DTYPE)],
        compiler_params=_params("parallel", "arbitrary"),
        name="proj",
    )(x, w_stack)


def _out_ln_kernel(y_ref, w_ref, x_ref, g_ref, b_ref, o_ref):
    f = _mm(y_ref[...], w_ref[...])
    o_ref[...] = _layer_norm(DN_ALPHA * x_ref[...] + f, g_ref[...], b_ref[...])


def _resident(shape):
    return pl.BlockSpec(shape, lambda *_: (0,) * len(shape), pipeline_mode=pl.Buffered(1))


def _out_ln(y, w, x, g, b, tm):
    m, k = y.shape
    d = w.shape[1]
    return pl.pallas_call(
        _out_ln_kernel,
        grid=(m // tm,),
        in_specs=[pl.BlockSpec((tm, k), lambda i: (i, 0)),
                  _resident((k, d)),
                  pl.BlockSpec((tm, d), lambda i: (i, 0)),
                  _resident((1, d)),
                  _resident((1, d))],
        out_specs=pl.BlockSpec((tm, d), lambda i: (i, 0)),
        out_shape=jax.ShapeDtypeStruct((m, d), F32),
        compiler_params=_params("parallel"),
        name="out_ln",
    )(y, w, x, g, b)


def _mlp_kernel(x_ref, w1_ref, w2_ref, g_ref, b_ref, o_ref, xb_ref, *, nf):
    j = pl.program_id(1)

    @pl.when(j == 0)
    def _():
        xb_ref[...] = x_ref[...].astype(MXU_DTYPE)
        o_ref[...] = jnp.zeros_like(o_ref)

    h = _mm(xb_ref[...], w1_ref[...])
    h = jnp.square(jnp.maximum(h, 0.0)).astype(MXU_DTYPE)
    slab = o_ref.shape[1] // MLP_OUT_SLABS
    for s in range(MLP_OUT_SLABS):
        cols = slice(s * slab, (s + 1) * slab)
        o_ref[:, cols] += _mm(h, w2_ref[:, cols])

    @pl.when(j == nf - 1)
    def _():
        o_ref[...] = _layer_norm(DN_ALPHA * x_ref[...] + o_ref[...], g_ref[...], b_ref[...])


def _mlp(x, w1_stack, w2_stack, layer, g, b, tm, tf):
    m, d = x.shape
    f = w1_stack.shape[2]
    nf = f // tf
    return pl.pallas_call(
        functools.partial(_mlp_kernel, nf=nf),
        grid=(m // tm, nf),
        in_specs=[pl.BlockSpec((tm, d), lambda i, j: (i, 0), pipeline_mode=pl.Buffered(1)),
                  pl.BlockSpec((None, d, tf), lambda i, j: (layer, 0, j)),
                  pl.BlockSpec((None, tf, d), lambda i, j: (layer, j, 0)),
                  _resident((1, d)),
                  _resident((1, d))],
        out_specs=pl.BlockSpec((tm, d), lambda i, j: (i, 0)),
        out_shape=jax.ShapeDtypeStruct((m, d), F32),
        scratch_shapes=[pltpu.VMEM((tm, d), MXU_DTYPE)],
        compiler_params=_params("parallel", "arbitrary"),
        name="mlp",
    )(x, w1_stack, w2_stack, g, b)


def _s5_in_kernel(x_ref, w_ref, o_ref):
    u = _mm(x_ref[...].astype(MXU_DTYPE), w_ref[...])
    for gs in range(S5_SETS):
        o_ref[gs] = u[:, gs * LANES:(gs + 1) * LANES]


def _s5_in(x, w, tm):
    m = x.shape[0]
    return pl.pallas_call(
        _s5_in_kernel,
        grid=(m // tm,),
        in_specs=[pl.BlockSpec((tm, D_MODEL), lambda i: (i, 0)),
                  _resident((D_MODEL, D_MODEL))],
        out_specs=pl.BlockSpec((S5_SETS, tm, LANES), lambda i: (0, i, 0)),
        out_shape=jax.ShapeDtypeStruct((S5_SETS, m, LANES), F32),
        compiler_params=_params("parallel"),
        name="s5_in",
    )(x, w)


def _s5_core_kernel(u_ref, lb_ref, rc_ref, cc_ref, a16_ref, d_ref, o_ref,
                    wt_ref, win_ref, wout_ref, v_ref, sp_ref, *, batch, chunks_per_batch):
    nst = S5_SET_STATE
    kdim = S5_CHUNK * LANES
    lb = lb_ref[0]

    drev = lax.dot_general(lb, rc_ref[0], (((1,), (0,)), ((), ())),
                           preferred_element_type=F32, precision=HI)
    row = lax.broadcasted_iota(jnp.int32, (kdim, LANES), 0)
    lane = lax.broadcasted_iota(jnp.int32, (kdim, LANES), 1)
    row_group = (row % LANES) // S5_GROUP
    drev = jnp.where(row_group == lane // S5_GROUP, drev, 0.0).astype(MXU_DTYPE)
    for t in range(S5_CHUNK):
        used = (t + 1) * LANES
        wt_ref[0:used, t * LANES:(t + 1) * LANES] = drev[kdim - used:, :]
        if used < kdim:
            wt_ref[used:, t * LANES:(t + 1) * LANES] = jnp.zeros((kdim - used, LANES), MXU_DTYPE)

    swapped = pltpu.roll(lb, S5_STATE, axis=1)
    low = lane < S5_STATE
    re_dup = jnp.where(low, lb, swapped)
    im_dup = jnp.where(low, swapped, lb)
    for v in range(nst // LANES):
        sel = row_group == 2 * v + (lane >= S5_STATE).astype(jnp.int32)
        win_ref[:, v * LANES:(v + 1) * LANES] = jnp.where(sel, re_dup, 0.0).astype(MXU_DTYPE)
        win_ref[:, nst + v * LANES:nst + (v + 1) * LANES] = jnp.where(sel, im_dup, 0.0).astype(MXU_DTYPE)

    cc = cc_ref[0]
    lane_group = (lax.broadcasted_iota(jnp.int32, (S5_STATE, kdim), 1) % LANES) // S5_GROUP
    for ri in range(2):
        part = cc[ri * S5_STATE:(ri + 1) * S5_STATE, :]
        for g in range(S5_SET_GROUPS):
            r0 = ri * nst + g * S5_STATE
            wout_ref[r0:r0 + S5_STATE, :] = jnp.where(lane_group == g, part, 0.0).astype(MXU_DTYPE)

    nchunks = batch * chunks_per_batch
    u = jnp.concatenate([u_ref[pl.ds(t, nchunks, stride=S5_CHUNK), :] for t in range(S5_CHUNK)], axis=1)
    xb = u.astype(MXU_DTYPE)
    vin = _mm(xb, win_ref[...])
    ntile = nst // LANES
    for k in range(2 * ntile):
        v_ref[k] = vin[:, k * LANES:(k + 1) * LANES]

    a_re = [a16_ref[0, :, k * LANES:(k + 1) * LANES] for k in range(ntile)]
    a_im = [a16_ref[0, :, nst + k * LANES:nst + (k + 1) * LANES] for k in range(ntile)]

    def step(c, carry):
        rows = pl.ds(c, batch, stride=chunks_per_batch)
        new = []
        for k in range(ntile):
            s_re, s_im = carry[k], carry[ntile + k]
            sp_ref[k, rows, :] = s_re
            sp_ref[ntile + k, rows, :] = s_im
            new.append((a_re[k] * s_re - a_im[k] * s_im + v_ref[k, rows, :],
                        a_re[k] * s_im + a_im[k] * s_re + v_ref[ntile + k, rows, :]))
        return tuple(n[0] for n in new) + tuple(n[1] for n in new)

    zero = jnp.zeros((batch, LANES), F32)
    lax.fori_loop(0, chunks_per_batch, step, (zero,) * (2 * ntile), unroll=S5_SCAN_UNROLL)

    sp = jnp.concatenate([sp_ref[k] for k in range(2 * ntile)], axis=1)
    y = _mm(xb, wt_ref[...]) + _mm(sp.astype(MXU_DTYPE), wout_ref[...])
    y = y + d_ref[0] * u
    h = jax.nn.gelu(y)
    for t in range(S5_CHUNK):
        o_ref[pl.ds(t, nchunks, stride=S5_CHUNK), :] = h[:, t * LANES:(t + 1) * LANES]


def _s5_core(ug, lb, rc, cc, a16, drow, batch):
    nsets, m, _ = ug.shape
    nchunks = m // S5_CHUNK
    kdim = S5_CHUNK * LANES
    kern = functools.partial(_s5_core_kernel, batch=batch, chunks_per_batch=nchunks // batch)
    return pl.pallas_call(
        kern,
        grid=(nsets,),
        in_specs=[pl.BlockSpec((None, m, LANES), lambda s: (s, 0, 0)),
                  pl.BlockSpec((1, kdim, LANES), lambda s: (s, 0, 0)),
                  pl.BlockSpec((1, LANES, LANES), lambda s: (s, 0, 0)),
                  pl.BlockSpec((1, LANES, kdim), lambda s: (s, 0, 0)),
                  pl.BlockSpec((1, 1, 2 * S5_SET_STATE), lambda s: (s, 0, 0)),
                  pl.BlockSpec((1, 1, kdim), lambda s: (s, 0, 0))],
        out_specs=pl.BlockSpec((None, m, LANES), lambda s: (s, 0, 0)),
        out_shape=jax.ShapeDtypeStruct((nsets, m, LANES), F32),
        scratch_shapes=[pltpu.VMEM((kdim, kdim), MXU_DTYPE),
                        pltpu.VMEM((kdim, 2 * S5_SET_STATE), MXU_DTYPE),
                        pltpu.VMEM((2 * S5_SET_STATE, kdim), MXU_DTYPE),
                        pltpu.VMEM((2 * S5_SET_STATE // LANES, nchunks, LANES), F32),
                        pltpu.VMEM((2 * S5_SET_STATE // LANES, nchunks, LANES), F32)],
        compiler_params=_params("parallel"),
        name="s5_core",
    )(ug, lb, rc, cc, a16, drow)


def _s5_out_kernel(h_ref, wo_ref, wg_ref, x_ref, g_ref, b_ref, o_ref):
    h = jnp.concatenate([h_ref[gs] for gs in range(S5_SETS)], axis=1).astype(MXU_DTYPE)
    f = _mm(h, wo_ref[...]) * jax.nn.sigmoid(_mm(h, wg_ref[...]))
    o_ref[...] = _layer_norm(DN_ALPHA * x_ref[...] + f, g_ref[...], b_ref[...])


def _s5_out(hg, wo, wg, x, g, b, tm):
    m = x.shape[0]
    return pl.pallas_call(
        _s5_out_kernel,
        grid=(m // tm,),
        in_specs=[pl.BlockSpec((S5_SETS, tm, LANES), lambda i: (0, i, 0)),
                  _resident((D_MODEL, D_MODEL)),
                  _resident((D_MODEL, D_MODEL)),
                  pl.BlockSpec((tm, D_MODEL), lambda i: (i, 0)),
                  _resident((1, D_MODEL)),
                  _resident((1, D_MODEL))],
        out_specs=pl.BlockSpec((tm, D_MODEL), lambda i: (i, 0)),
        out_shape=jax.ShapeDtypeStruct(x.shape, F32),
        compiler_params=_params("parallel"),
        name="s5_out",
    )(hg, wo, wg, x, g, b)


def _s5_tables(lam_re, lam_im, log_dt, b_re, b_im, c_re, c_im, d_skip):
    lr = lam_re.astype(F32)
    li = lam_im.astype(F32)
    dt = jnp.exp(log_dt.astype(F32))[:, None]
    mag = jnp.exp(lr * dt)
    ar = mag * jnp.cos(li * dt)
    ai = mag * jnp.sin(li * dt)
    den = lr * lr + li * li
    zr = ((ar - 1.0) * lr + ai * li) / den
    zi = (ai * lr - (ar - 1.0) * li) / den
    br_ = b_re.astype(F32)
    bi_ = b_im.astype(F32)
    bbr = zr[..., None] * br_ - zi[..., None] * bi_
    bbi = zr[..., None] * bi_ + zi[..., None] * br_
    k = jnp.arange(S5_CHUNK + 1, dtype=F32)[:, None, None]
    pmag = jnp.exp(k * (lr * dt))
    pr = pmag * jnp.cos(k * (li * dt))
    pi = pmag * jnp.sin(k * (li * dt))

    bt_re = jnp.swapaxes(bbr, 1, 2)
    bt_im = jnp.swapaxes(bbi, 1, 2)
    shape_b = (S5_SETS, 1, S5_SET_GROUPS, S5_GROUP, 2 * S5_STATE)
    b_with_re = jnp.concatenate([bt_re, bt_im], axis=-1).reshape(shape_b)
    b_with_im = jnp.concatenate([-bt_im, bt_re], axis=-1).reshape(shape_b)
    def power_rows(t):
        t = jnp.concatenate([t, t], axis=-1).reshape(S5_CHUNK, S5_SETS, S5_SET_GROUPS, 1, 2 * S5_STATE)
        return t.transpose(1, 0, 2, 3, 4)
    lb = power_rows(pr[S5_CHUNK - 1::-1]) * b_with_re + power_rows(pi[S5_CHUNK - 1::-1]) * b_with_im
    lb = lb.reshape(S5_SETS, S5_CHUNK * LANES, 2 * S5_STATE)

    def rows_p(t):
        return t.astype(F32).reshape(S5_SETS, S5_SET_GROUPS, S5_GROUP, S5_STATE).transpose(0, 3, 1, 2)
    ct_re = rows_p(c_re)
    ct_im = rows_p(c_im)
    rc = jnp.concatenate([ct_re, -ct_im], axis=1).reshape(S5_SETS, 2 * S5_STATE, LANES)

    def power_lanes(t):
        return t.reshape(S5_CHUNK, S5_SETS, S5_SET_GROUPS, S5_STATE).transpose(1, 3, 0, 2)[..., None]
    prs = power_lanes(pr[1:])
    pis = power_lanes(pi[1:])
    ca_re = ct_re[:, :, None] * prs - ct_im[:, :, None] * pis
    ca_im = ct_re[:, :, None] * pis + ct_im[:, :, None] * prs
    cc = jnp.concatenate([ca_re, -ca_im], axis=1).reshape(S5_SETS, 2 * S5_STATE, S5_CHUNK * LANES)

    a16 = jnp.concatenate([pr[S5_CHUNK].reshape(S5_SETS, 1, S5_SET_STATE),
                           pi[S5_CHUNK].reshape(S5_SETS, 1, S5_SET_STATE)], axis=2)
    drow = jnp.tile(d_skip.astype(F32).reshape(S5_SETS, 1, LANES), (1, 1, S5_CHUNK))
    return lb, rc, cc, a16, drow


def _s5_layer(x, batch, w_in, lam_re, lam_im, log_dt, b_re, b_im, c_re, c_im, d_skip, w_out, w_gate, g, b):
    lb, rc, cc, a16, drow = _s5_tables(lam_re, lam_im, log_dt, b_re, b_im, c_re, c_im, d_skip)
    ug = _s5_in(x, w_in.astype(MXU_DTYPE), tm=512)
    hg = _s5_core(ug, lb, rc, cc, a16, drow, batch)
    return _s5_out(hg, w_out.astype(MXU_DTYPE), w_gate.astype(MXU_DTYPE), x, g, b, tm=512)


def _split_hi_lo(v):
    hi = v.astype(MXU_DTYPE)
    lo = (v - hi.astype(F32)).astype(MXU_DTYPE)
    return jnp.concatenate([hi, lo], axis=1)


def _ssd_core_kernel(z_ref, xs_ref, bc_ref, xin_ref, wdt_ref, cwx_ref, cwb_ref, cbx_ref, cbb_ref,
                     dtb_ref, a_ref, dsk_ref, nw_ref, e2_ref, o_ref,
                     xbuf_ref, bbuf_ref, st_ref, yd_ref):
    q = SSD_CHUNK
    pad = 8

    @pl.when(pl.program_id(1) == 0)
    def _():
        xbuf_ref[0:pad, :] = jnp.zeros((pad, SSD_D_INNER), F32)
        bbuf_ref[0:pad, :] = jnp.zeros((pad, SSD_BC), F32)
        st_ref[...] = jnp.zeros_like(st_ref)

    xbuf_ref[pad:, :] = xs_ref[...]
    bbuf_ref[pad:, :] = bc_ref[...]

    def conv(buf_ref, w_ref, b_ref):
        acc = b_ref[...] + w_ref[0:1, :] * buf_ref[pl.ds(pad - 3, q), :]
        for k in range(1, SSD_CONV):
            acc = acc + w_ref[k:k + 1, :] * buf_ref[pl.ds(pad - 3 + k, q), :]
        return _silu(acc)

    xs = conv(xbuf_ref, cwx_ref, cbx_ref)
    bc = conv(bbuf_ref, cwb_ref, cbb_ref)
    xbuf_ref[0:pad, :] = xbuf_ref[q:q + pad, :]
    bbuf_ref[0:pad, :] = bbuf_ref[q:q + pad, :]

    dtr = _mm(xin_ref[...].astype(MXU_DTYPE), wdt_ref[...]) + dtb_ref[...]
    dt = jnp.maximum(dtr, 0.0) + jnp.log1p(jnp.exp(-jnp.abs(dtr)))
    adt = dt * a_ref[...]
    ri = lax.broadcasted_iota(jnp.int32, (q, q), 0)
    ci = lax.broadcasted_iota(jnp.int32, (q, q), 1)
    causal = ri >= ci
    tri = causal.astype(F32)
    cs = jnp.dot(tri, adt, preferred_element_type=F32, precision=HI)
    eye = (ri == ci).astype(F32)
    cs_rows = _mm_nt(eye, cs, precision=HI)
    dt_rows = _mm_nt(eye, dt, precision=HI)
    tot = cs[q - 1:q, :]
    w_col = dt * jnp.exp(tot - cs)
    din = jnp.exp(cs)
    w_x = _mm(_split_hi_lo(w_col), e2_ref[...])
    din_x = _mm(_split_hi_lo(din), e2_ref[...])

    lane = lax.broadcasted_iota(jnp.int32, (q, LANES), 1)
    low = lane < SSD_HEADDIM

    xw = (xs * w_x).astype(MXU_DTYPE)
    xsb = xs.astype(MXU_DTYPE)
    for g in range(SSD_GROUPS):
        bg = bc[:, g * SSD_STATE:(g + 1) * SSD_STATE].astype(MXU_DTYPE)
        cg = bc[:, SSD_GROUPS * SSD_STATE + g * SSD_STATE:
                SSD_GROUPS * SSD_STATE + (g + 1) * SSD_STATE].astype(MXU_DTYPE)
        cb = _mm_nt(cg, bg)
        for pair in range(SSD_HEADS_PER_GROUP // 2):
            h0 = g * SSD_HEADS_PER_GROUP + 2 * pair
            ms = []
            for h in (h0, h0 + 1):
                seg = cs[:, h:h + 1] - cs_rows[h:h + 1, :]
                lmat = jnp.exp(jnp.where(causal, seg, -jnp.inf))
                ms.append((cb * lmat * dt_rows[h:h + 1, :]).astype(MXU_DTYPE))
            lhs = jnp.concatenate(ms, axis=1)
            xp = xsb[:, h0 * SSD_HEADDIM:(h0 + 2) * SSD_HEADDIM]
            zero = jnp.zeros_like(xp)
            rhs = jnp.concatenate([jnp.where(low, xp, zero), jnp.where(low, zero, xp)], axis=0)
            yd_ref[:, h0 * SSD_HEADDIM:(h0 + 2) * SSD_HEADDIM] = _mm(lhs, rhs)
        gsl = slice(g * SSD_GROUP_DIM, (g + 1) * SSD_GROUP_DIM)
        st = st_ref[g]
        y_off = _mm(cg, st.astype(MXU_DTYPE)) * din_x[:, gsl]
        yd_ref[:, gsl] = yd_ref[:, gsl] + y_off
        st_ref[g] = din_x[q - 1:q, gsl] * st + _mm_tn(bg, xw[:, gsl])

    y = yd_ref[...] + dsk_ref[...] * xs
    y = y * _silu(z_ref[...])
    for g in range(SSD_GROUPS):
        gsl = slice(g * SSD_GROUP_DIM, (g + 1) * SSD_GROUP_DIM)
        yg = y[:, gsl]
        yg = yg * lax.rsqrt(jnp.mean(yg * yg, axis=-1, keepdims=True) + LN_EPS)
        o_ref[:, gsl] = (yg * nw_ref[:, gsl]).astype(o_ref.dtype)


def _ssd_core(proj, x, w_dt, batch, conv_w, conv_b, dt_bias, a_log, d_skip, norm_w):
    m = proj.shape[0]
    q = SSD_CHUNK
    ncb = m // batch // q
    di = SSD_D_INNER
    wdt = jnp.pad(w_dt.astype(MXU_DTYPE), ((0, 0), (0, LANES - SSD_HEADS)))
    cw = conv_w.astype(F32)
    cbias = conv_b.astype(F32)[None, :]
    padh = LANES - SSD_HEADS
    dtb = jnp.pad(dt_bias.astype(F32), (0, padh))[None, :]
    a = jnp.pad(-jnp.exp(a_log.astype(F32)), (0, padh))[None, :]
    dsk = jnp.repeat(d_skip.astype(F32), SSD_HEADDIM)[None, :]
    nw = norm_w.astype(F32)[None, :]
    expand = (jnp.arange(LANES)[:, None] == (jnp.arange(di) // SSD_HEADDIM)[None, :]).astype(MXU_DTYPE)
    e2 = jnp.concatenate([expand, expand], axis=0)

    row = lambda b, c: b * ncb + c
    full = _resident
    return pl.pallas_call(
        _ssd_core_kernel,
        grid=(batch, ncb),
        in_specs=[pl.BlockSpec((q, di), lambda b, c: (row(b, c), 0)),
                  pl.BlockSpec((q, di), lambda b, c: (row(b, c), 1)),
                  pl.BlockSpec((q, SSD_BC), lambda b, c: (row(b, c), 2 * di // SSD_BC)),
                  pl.BlockSpec((q, D_MODEL), lambda b, c: (row(b, c), 0)),
                  full((D_MODEL, LANES)),
                  full((SSD_CONV, di)), full((SSD_CONV, SSD_BC)), full((1, di)), full((1, SSD_BC)),
                  full((1, LANES)), full((1, LANES)), full((1, di)), full((1, di)),
                  full((2 * LANES, di))],
        out_specs=pl.BlockSpec((q, di), lambda b, c: (row(b, c), 0)),
        out_shape=jax.ShapeDtypeStruct((m, di), MXU_DTYPE),
        scratch_shapes=[pltpu.VMEM((q + 8, di), F32), pltpu.VMEM((q + 8, SSD_BC), F32),
                        pltpu.VMEM((SSD_GROUPS, SSD_STATE, SSD_GROUP_DIM), F32),
                        pltpu.VMEM((q, di), F32)],
        compiler_params=_params("parallel", "arbitrary"),
        name="ssd_core",
    )(proj, proj, proj, x, wdt, cw[:, :di], cw[:, di:], cbias[:, :di], cbias[:, di:], dtb, a, dsk, nw, e2)


def _ssd_layer(x, batch, w_in_stack, layer, conv_w, conv_b, dt_bias, a_log, d_skip, norm_w, w_out, g, b):
    n_main = SSD_D_INNER + SSD_CONV_DIM
    proj = _proj(x, w_in_stack.astype(MXU_DTYPE), layer, n_main, tm=min(1024, x.shape[0]), tn=1024,
                 out_dtype=F32)
    w_dt = w_in_stack[layer, :, n_main:]
    y = _ssd_core(proj, x, w_dt, batch, conv_w, conv_b, dt_bias, a_log, d_skip, norm_w)
    return _out_ln(y, w_out.astype(MXU_DTYPE), x, g, b, tm=512)


def _ret_core_kernel(cd_ref, q_ref, k_ref, v_ref, g_ref, cos_ref, sin_ref, dmat_ref, xi_ref, zeta_ref,
                     o_ref, st_ref):
    @pl.when(pl.program_id(1) == 0)
    def _():
        st_ref[...] = jnp.zeros_like(st_ref)

    cos = cos_ref[...]
    sin = sin_ref[...]
    half = RET_DK // 2

    def rotate(t):
        t1 = t[:, :half]
        t2 = t[:, half:]
        return jnp.concatenate([t1 * cos - t2 * sin, t1 * sin + t2 * cos], axis=1)

    for h in range(RET_HEADS):
        qh = rotate(q_ref[:, h * RET_DK:(h + 1) * RET_DK].astype(F32))
        kh = rotate(k_ref[:, h * RET_DK:(h + 1) * RET_DK].astype(F32) * (RET_DK ** -0.5))
        vh = v_ref[:, h * RET_DV:(h + 1) * RET_DV].astype(MXU_DTYPE)
        qb = qh.astype(MXU_DTYPE)
        scores = _mm_nt(qb, kh.astype(MXU_DTYPE)) * dmat_ref[h]
        inner = _mm(scores.astype(MXU_DTYPE), vh)
        st = st_ref[h]
        xi = jnp.concatenate([xi_ref[h]] * (RET_DV // LANES), axis=1)
        cross = _mm(qb, st.astype(MXU_DTYPE)) * xi
        kz = kh * jnp.concatenate([zeta_ref[h]] * (RET_DK // LANES), axis=1)
        st_ref[h] = cd_ref[h] * st + _mm_tn(kz.astype(MXU_DTYPE), vh)
        out = inner + cross
        mu = jnp.mean(out, axis=-1, keepdims=True)
        d = out - mu
        var = jnp.mean(d * d, axis=-1, keepdims=True)
        out = d * lax.rsqrt(var + LN_EPS)
        gate = _silu(g_ref[:, h * RET_DV:(h + 1) * RET_DV].astype(F32))
        o_ref[:, h * RET_DV:(h + 1) * RET_DV] = (gate * out).astype(o_ref.dtype)


def _ret_core(proj, batch):
    m = proj.shape[0]
    seq = m // batch
    q = RET_CHUNK
    ncb = seq // q
    theta = 1.0 / (RET_ROPE_BASE ** jnp.linspace(0.0, 1.0, RET_DK // 2, dtype=F32))
    ang = jnp.arange(seq, dtype=F32)[:, None] * theta[None, :]
    cos, sin = jnp.cos(ang), jnp.sin(ang)
    lg = jnp.log1p(-jnp.exp2(-5.0 - jnp.arange(RET_HEADS, dtype=F32)))
    pos = jnp.arange(q, dtype=F32)
    diff = pos[:, None] - pos[None, :]
    dmat = jnp.where(diff >= 0, jnp.exp(jnp.maximum(diff, 0.0)[None] * lg[:, None, None]), 0.0)
    xi = jnp.exp((pos[None, :] + 1.0) * lg[:, None])
    zeta = jnp.exp((q - 1.0 - pos[None, :]) * lg[:, None])
    xi_b = jnp.broadcast_to(xi[:, :, None], (RET_HEADS, q, LANES))
    zeta_b = jnp.broadcast_to(zeta[:, :, None], (RET_HEADS, q, LANES))
    chunk_decay = jnp.exp(q * lg)

    row = lambda b, c: b * ncb + c
    d = D_MODEL
    return pl.pallas_call(
        _ret_core_kernel,
        grid=(batch, ncb),
        in_specs=[pl.BlockSpec(memory_space=pltpu.SMEM),
                  pl.BlockSpec((q, d), lambda b, c: (row(b, c), 0)),
                  pl.BlockSpec((q, d), lambda b, c: (row(b, c), 1)),
                  pl.BlockSpec((q, 2 * d), lambda b, c: (row(b, c), 1)),
                  pl.BlockSpec((q, 2 * d), lambda b, c: (row(b, c), 2)),
                  pl.BlockSpec((q, RET_DK // 2), lambda b, c: (c, 0)),
                  pl.BlockSpec((q, RET_DK // 2), lambda b, c: (c, 0)),
                  _resident((RET_HEADS, q, q)),
                  _resident((RET_HEADS, q, LANES)),
                  _resident((RET_HEADS, q, LANES))],
        out_specs=pl.BlockSpec((q, 2 * d), lambda b, c: (row(b, c), 0)),
        out_shape=jax.ShapeDtypeStruct((m, 2 * d), MXU_DTYPE),
        scratch_shapes=[pltpu.VMEM((RET_HEADS, RET_DK, RET_DV), F32)],
        compiler_params=_params("parallel", "arbitrary"),
        name="ret_core",
    )(chunk_decay, proj, proj, proj, proj, cos, sin, dmat, xi_b, zeta_b)


def _ret_layer(x, batch, w_in_stack, layer, w_out, g, b):
    proj = _proj(x, w_in_stack, layer, RET_IN_DIM, tm=min(1024, x.shape[0]), tn=1024, out_dtype=MXU_DTYPE)
    y = _ret_core(proj, batch)
    return _out_ln(y, w_out.astype(MXU_DTYPE), x, g, b, tm=512)


def kernel(x, ln1_g, ln1_b, ln2_g, ln2_b, mlp_w1, mlp_w2, s5_w_in, s5_lam_re, s5_lam_im, s5_log_dt, s5_b_re, s5_b_im, s5_c_re, s5_c_im, s5_d, s5_w_out, s5_w_gate, ssd_w_in, ssd_conv_w, ssd_conv_b, ssd_dt_bias, ssd_a_log, ssd_d, ssd_norm_w, ssd_w_out, ret_w_in, ret_w_out):
    batch, seq, d = x.shape
    h = x.reshape(batch * seq, d)
    w1_stack = mlp_w1.astype(MXU_DTYPE)
    w2_stack = mlp_w2.astype(MXU_DTYPE)
    for i in range(DEPTH):
        kind = i % 3
        j = i // 3
        g1 = ln1_g[i][None, :]
        b1 = ln1_b[i][None, :]
        if kind == 0:
            h = _s5_layer(h, batch, s5_w_in[j], s5_lam_re[j], s5_lam_im[j], s5_log_dt[j], s5_b_re[j],
                          s5_b_im[j], s5_c_re[j], s5_c_im[j], s5_d[j], s5_w_out[j], s5_w_gate[j], g1, b1)
        elif kind == 1:
            h = _ssd_layer(h, batch, ssd_w_in, j, ssd_conv_w[j], ssd_conv_b[j], ssd_dt_bias[j], ssd_a_log[j],
                           ssd_d[j], ssd_norm_w[j], ssd_w_out[j], g1, b1)
        else:
            h = _ret_layer(h, batch, ret_w_in, j, ret_w_out[j], g1, b1)
        h = _mlp(h, w1_stack, w2_stack, i, ln2_g[i][None, :], ln2_b[i][None, :],
                 tm=min(1024, h.shape[0]), tf=1024)
    return h.reshape(batch, seq, d)
```

```python
import functools
import math

import jax
import jax.numpy as jnp
from jax import lax
from jax.experimental import pallas as pl
from jax.experimental.pallas import tpu as pltpu

F32 = jnp.float32
MXU_DTYPE = jnp.bfloat16
HI = lax.Precision.HIGHEST
LOG2_E = 1.0 / math.log(2.0)

LANES = 128
VMEM_LIMIT_BYTES = 58 * 1024 * 1024

D_MODEL = 2048
DEPTH = 4
DN_ALPHA = (2 * DEPTH) ** 0.25
LN_EPS = 1e-5
D_FF = 4 * D_MODEL
MLP_OUT_SLABS = 4
ROW_SPLIT = 2

S5_GROUP = 16
S5_GROUPS = D_MODEL // S5_GROUP
S5_STATE = 64
S5_CHUNK = 16
S5_SETS = D_MODEL // LANES
S5_SET_GROUPS = LANES // S5_GROUP
S5_SET_STATE = S5_SET_GROUPS * S5_STATE
S5_SCAN_UNROLL = 8

SSD_D_INNER = 2 * D_MODEL
SSD_HEADDIM = 64
SSD_HEADS = SSD_D_INNER // SSD_HEADDIM
SSD_GROUPS = 8
SSD_STATE = 128
SSD_CONV = 4
SSD_CHUNK = 128
SSD_BC = 2 * SSD_GROUPS * SSD_STATE
SSD_CONV_DIM = SSD_D_INNER + SSD_BC
SSD_IN_DIM = SSD_D_INNER + SSD_CONV_DIM + SSD_HEADS
SSD_IN_PAD = SSD_D_INNER + SSD_CONV_DIM + LANES
SSD_GROUP_DIM = SSD_D_INNER // SSD_GROUPS
SSD_HEADS_PER_GROUP = SSD_HEADS // SSD_GROUPS

RET_HEADS = 8
RET_DK = D_MODEL // RET_HEADS
RET_DV = 2 * D_MODEL // RET_HEADS
RET_CHUNK = 128
RET_IN_DIM = 6 * D_MODEL
RET_ROPE_BASE = 10000.0


def _params(*semantics):
    return pltpu.CompilerParams(dimension_semantics=semantics, vmem_limit_bytes=VMEM_LIMIT_BYTES)


def _layer_norm(v, g, b):
    mu = jnp.mean(v, axis=-1, keepdims=True)
    d = v - mu
    var = jnp.mean(d * d, axis=-1, keepdims=True)
    return d * lax.rsqrt(var + LN_EPS) * g + b


def _silu(v):
    return v * jax.nn.sigmoid(v)


def _mm(a, b):
    return jnp.dot(a, b, preferred_element_type=F32)


def _mm_nt(a, b, precision=None):
    return lax.dot_general(a, b, (((1,), (1,)), ((), ())), preferred_element_type=F32, precision=precision)


def _mm_tn(a, b):
    return lax.dot_general(a, b, (((0,), (0,)), ((), ())), preferred_element_type=F32)


def _proj_kernel(x_ref, w_ref, o_ref, xb_ref):
    @pl.when(pl.program_id(1) == 0)
    def _():
        xb_ref[...] = x_ref[...].astype(MXU_DTYPE)

    o_ref[...] = _mm(xb_ref[...], w_ref[...].astype(MXU_DTYPE)).astype(o_ref.dtype)


def _proj(x, w_stack, layer, n, tm, tn, out_dtype):
    m, k = x.shape
    return pl.pallas_call(
        _proj_kernel,
        grid=(m // tm, n // tn),
        in_specs=[pl.BlockSpec((tm, k), lambda i, j: (i, 0)),
                  pl.BlockSpec((None, k, tn), lambda i, j: (layer, 0, j))],
        out_specs=pl.BlockSpec((tm, tn), lambda i, j: (i, j)),
        out_shape=jax.ShapeDtypeStruct((m, n), out_dtype),
        scratch_shapes=[pltpu.VMEM((tm, k), MXU_DTYPE)],
        compiler_params=_params("parallel", "arbitrary"),
        name="proj",
    )(x, w_stack)


def _out_ln_kernel(y_ref, w_ref, x_ref, g_ref, b_ref, o_ref):
    half = o_ref.shape[0] // ROW_SPLIT
    for r in range(ROW_SPLIT):
        rows = slice(r * half, (r + 1) * half)
        f = _mm(y_ref[rows, :], w_ref[...])
        o_ref[rows, :] = _layer_norm(DN_ALPHA * x_ref[rows, :] + f, g_ref[...], b_ref[...])


def _resident(shape):
    return pl.BlockSpec(shape, lambda *_: (0,) * len(shape), pipeline_mode=pl.Buffered(1))


def _out_ln(y, w, x, g, b, tm):
    m, k = y.shape
    d = w.shape[1]
    return pl.pallas_call(
        _out_ln_kernel,
        grid=(m // tm,),
        in_specs=[pl.BlockSpec((tm, k), lambda i: (i, 0)),
                  _resident((k, d)),
                  pl.BlockSpec((tm, d), lambda i: (i, 0)),
                  _resident((1, d)),
                  _resident((1, d))],
        out_specs=pl.BlockSpec((tm, d), lambda i: (i, 0)),
        out_shape=jax.ShapeDtypeStruct((m, d), F32),
        compiler_params=_params("parallel"),
        name="out_ln",
    )(y, w, x, g, b)


def _mlp_kernel(x_ref, w1_ref, w2_ref, g_ref, b_ref, o_ref, xb_ref, *, nf):
    j = pl.program_id(1)

    @pl.when(j == 0)
    def _():
        xb_ref[...] = x_ref[...].astype(MXU_DTYPE)
        o_ref[...] = jnp.zeros_like(o_ref)

    h = _mm(xb_ref[...], w1_ref[...].astype(MXU_DTYPE))
    h = jnp.square(jnp.maximum(h, 0.0)).astype(MXU_DTYPE)
    slab = o_ref.shape[1] // MLP_OUT_SLABS
    for s in range(MLP_OUT_SLABS):
        cols = slice(s * slab, (s + 1) * slab)
        o_ref[:, cols] += _mm(h, w2_ref[:, cols].astype(MXU_DTYPE))

    @pl.when(j == nf - 1)
    def _():
        o_ref[...] = _layer_norm(DN_ALPHA * x_ref[...] + o_ref[...], g_ref[...], b_ref[...])


def _mlp(x, w1_stack, w2_stack, layer, g, b, tm, tf):
    m, d = x.shape
    f = w1_stack.shape[2]
    nf = f // tf
    return pl.pallas_call(
        functools.partial(_mlp_kernel, nf=nf),
        grid=(m // tm, nf),
        in_specs=[pl.BlockSpec((tm, d), lambda i, j: (i, 0), pipeline_mode=pl.Buffered(1)),
                  pl.BlockSpec((None, d, tf), lambda i, j: (layer, 0, j)),
                  pl.BlockSpec((None, tf, d), lambda i, j: (layer, j, 0)),
                  _resident((1, d)),
                  _resident((1, d))],
        out_specs=pl.BlockSpec((tm, d), lambda i, j: (i, 0)),
        out_shape=jax.ShapeDtypeStruct((m, d), F32),
        scratch_shapes=[pltpu.VMEM((tm, d), MXU_DTYPE)],
        compiler_params=_params("parallel", "arbitrary"),
        name="mlp",
    )(x, w1_stack, w2_stack, g, b)


def _s5_in_kernel(x_ref, w_ref, o_ref):
    u = _mm(x_ref[...].astype(MXU_DTYPE), w_ref[...])
    for gs in range(S5_SETS):
        o_ref[gs] = u[:, gs * LANES:(gs + 1) * LANES]


def _s5_in(x, w, tm):
    m = x.shape[0]
    return pl.pallas_call(
        _s5_in_kernel,
        grid=(m // tm,),
        in_specs=[pl.BlockSpec((tm, D_MODEL), lambda i: (i, 0)),
                  _resident((D_MODEL, D_MODEL))],
        out_specs=pl.BlockSpec((S5_SETS, tm, LANES), lambda i: (0, i, 0)),
        out_shape=jax.ShapeDtypeStruct((S5_SETS, m, LANES), F32),
        compiler_params=_params("parallel"),
        name="s5_in",
    )(x, w)


def _s5_core_kernel(u_ref, lb_ref, rc_ref, cc_ref, a16_ref, d_ref, o_ref,
                    wt_ref, win_ref, wout_ref, v_ref, sp_ref, yin_ref, *, batch, chunks_per_batch):
    nst = S5_SET_STATE
    kdim = S5_CHUNK * LANES
    lb = lb_ref[0]

    drev = lax.dot_general(lb, rc_ref[0], (((1,), (0,)), ((), ())),
                           preferred_element_type=F32, precision=HI)
    row = lax.broadcasted_iota(jnp.int32, (kdim, LANES), 0)
    lane = lax.broadcasted_iota(jnp.int32, (kdim, LANES), 1)
    row_group = (row % LANES) // S5_GROUP
    drev = jnp.where(row_group == lane // S5_GROUP, drev, 0.0).astype(MXU_DTYPE)
    for t in range(S5_CHUNK):
        used = (t + 1) * LANES
        wt_ref[0:used, t * LANES:(t + 1) * LANES] = drev[kdim - used:, :]
        if used < kdim:
            wt_ref[used:, t * LANES:(t + 1) * LANES] = jnp.zeros((kdim - used, LANES), MXU_DTYPE)

    swapped = pltpu.roll(lb, S5_STATE, axis=1)
    low = lane < S5_STATE
    re_dup = jnp.where(low, lb, swapped)
    im_dup = jnp.where(low, swapped, lb)
    for v in range(nst // LANES):
        sel = row_group == 2 * v + (lane >= S5_STATE).astype(jnp.int32)
        win_ref[:, v * LANES:(v + 1) * LANES] = jnp.where(sel, re_dup, 0.0).astype(MXU_DTYPE)
        win_ref[:, nst + v * LANES:nst + (v + 1) * LANES] = jnp.where(sel, im_dup, 0.0).astype(MXU_DTYPE)

    cc = cc_ref[0]
    lane_group = (lax.broadcasted_iota(jnp.int32, (S5_STATE, kdim), 1) % LANES) // S5_GROUP
    for ri in range(2):
        part = cc[ri * S5_STATE:(ri + 1) * S5_STATE, :]
        for g in range(S5_SET_GROUPS):
            r0 = ri * nst + g * S5_STATE
            wout_ref[r0:r0 + S5_STATE, :] = jnp.where(lane_group == g, part, 0.0).astype(MXU_DTYPE)

    nchunks = batch * chunks_per_batch
    u = jnp.concatenate([u_ref[pl.ds(t, nchunks, stride=S5_CHUNK), :] for t in range(S5_CHUNK)], axis=1)
    xb = u.astype(MXU_DTYPE)
    vin = _mm(xb, win_ref[...])
    ntile = nst // LANES
    for k in range(2 * ntile):
        v_ref[k] = vin[:, k * LANES:(k + 1) * LANES]
    yin_ref[...] = _mm(xb, wt_ref[...]) + d_ref[0] * u

    a_re = [a16_ref[0, :, k * LANES:(k + 1) * LANES] for k in range(ntile)]
    a_im = [a16_ref[0, :, nst + k * LANES:nst + (k + 1) * LANES] for k in range(ntile)]

    def step(c, carry):
        rows = pl.ds(c, batch, stride=chunks_per_batch)
        new = []
        for k in range(ntile):
            s_re, s_im = carry[k], carry[ntile + k]
            sp_ref[k, rows, :] = s_re
            sp_ref[ntile + k, rows, :] = s_im
            new.append((a_re[k] * s_re - a_im[k] * s_im + v_ref[k, rows, :],
                        a_re[k] * s_im + a_im[k] * s_re + v_ref[ntile + k, rows, :]))
        return tuple(n[0] for n in new) + tuple(n[1] for n in new)

    zero = jnp.zeros((batch, LANES), F32)
    lax.fori_loop(0, chunks_per_batch, step, (zero,) * (2 * ntile), unroll=S5_SCAN_UNROLL)

    sp = jnp.concatenate([sp_ref[k] for k in range(2 * ntile)], axis=1)
    h = jax.nn.gelu(yin_ref[...] + _mm(sp.astype(MXU_DTYPE), wout_ref[...]))
    for t in range(S5_CHUNK):
        o_ref[pl.ds(t, nchunks, stride=S5_CHUNK), :] = h[:, t * LANES:(t + 1) * LANES]


def _s5_core(ug, lb, rc, cc, a16, drow, batch):
    nsets, m, _ = ug.shape
    nchunks = m // S5_CHUNK
    kdim = S5_CHUNK * LANES
    kern = functools.partial(_s5_core_kernel, batch=batch, chunks_per_batch=nchunks // batch)
    return pl.pallas_call(
        kern,
        grid=(nsets,),
        in_specs=[pl.BlockSpec((None, m, LANES), lambda s: (s, 0, 0)),
                  pl.BlockSpec((1, kdim, LANES), lambda s: (s, 0, 0)),
                  pl.BlockSpec((1, LANES, LANES), lambda s: (s, 0, 0)),
                  pl.BlockSpec((1, LANES, kdim), lambda s: (s, 0, 0)),
                  pl.BlockSpec((1, 1, 2 * S5_SET_STATE), lambda s: (s, 0, 0)),
                  pl.BlockSpec((1, 1, kdim), lambda s: (s, 0, 0))],
        out_specs=pl.BlockSpec((None, m, LANES), lambda s: (s, 0, 0)),
        out_shape=jax.ShapeDtypeStruct((nsets, m, LANES), F32),
        scratch_shapes=[pltpu.VMEM((kdim, kdim), MXU_DTYPE),
                        pltpu.VMEM((kdim, 2 * S5_SET_STATE), MXU_DTYPE),
                        pltpu.VMEM((2 * S5_SET_STATE, kdim), MXU_DTYPE),
                        pltpu.VMEM((2 * S5_SET_STATE // LANES, nchunks, LANES), F32),
                        pltpu.VMEM((2 * S5_SET_STATE // LANES, nchunks, LANES), F32),
                        pltpu.VMEM((nchunks, kdim), F32)],
        compiler_params=_params("parallel"),
        name="s5_core",
    )(ug, lb, rc, cc, a16, drow)


def _s5_out_kernel(h_ref, wo_ref, wg_ref, x_ref, g_ref, b_ref, o_ref):
    half = o_ref.shape[0] // ROW_SPLIT
    for r in range(ROW_SPLIT):
        rows = slice(r * half, (r + 1) * half)
        h = jnp.concatenate([h_ref[gs, rows, :] for gs in range(S5_SETS)], axis=1).astype(MXU_DTYPE)
        f = _mm(h, wo_ref[...]) * jax.nn.sigmoid(_mm(h, wg_ref[...]))
        o_ref[rows, :] = _layer_norm(DN_ALPHA * x_ref[rows, :] + f, g_ref[...], b_ref[...])


def _s5_out(hg, wo, wg, x, g, b, tm):
    m = x.shape[0]
    return pl.pallas_call(
        _s5_out_kernel,
        grid=(m // tm,),
        in_specs=[pl.BlockSpec((S5_SETS, tm, LANES), lambda i: (0, i, 0)),
                  _resident((D_MODEL, D_MODEL)),
                  _resident((D_MODEL, D_MODEL)),
                  pl.BlockSpec((tm, D_MODEL), lambda i: (i, 0)),
                  _resident((1, D_MODEL)),
                  _resident((1, D_MODEL))],
        out_specs=pl.BlockSpec((tm, D_MODEL), lambda i: (i, 0)),
        out_shape=jax.ShapeDtypeStruct(x.shape, F32),
        compiler_params=_params("parallel"),
        name="s5_out",
    )(hg, wo, wg, x, g, b)


def _s5_tables(lam_re, lam_im, log_dt, b_re, b_im, c_re, c_im, d_skip):
    lr = lam_re.astype(F32)
    li = lam_im.astype(F32)
    dt = jnp.exp(log_dt.astype(F32))[:, None]
    mag = jnp.exp(lr * dt)
    ar = mag * jnp.cos(li * dt)
    ai = mag * jnp.sin(li * dt)
    den = lr * lr + li * li
    zr = ((ar - 1.0) * lr + ai * li) / den
    zi = (ai * lr - (ar - 1.0) * li) / den
    br_ = b_re.astype(F32)
    bi_ = b_im.astype(F32)
    bbr = zr[..., None] * br_ - zi[..., None] * bi_
    bbi = zr[..., None] * bi_ + zi[..., None] * br_
    k = jnp.arange(S5_CHUNK + 1, dtype=F32)[:, None, None]
    pmag = jnp.exp(k * (lr * dt))
    pr = pmag * jnp.cos(k * (li * dt))
    pi = pmag * jnp.sin(k * (li * dt))

    bt_re = jnp.swapaxes(bbr, 1, 2)
    bt_im = jnp.swapaxes(bbi, 1, 2)
    shape_b = (S5_SETS, 1, S5_SET_GROUPS, S5_GROUP, 2 * S5_STATE)
    b_with_re = jnp.concatenate([bt_re, bt_im], axis=-1).reshape(shape_b)
    b_with_im = jnp.concatenate([-bt_im, bt_re], axis=-1).reshape(shape_b)
    def power_rows(t):
        t = jnp.concatenate([t, t], axis=-1).reshape(S5_CHUNK, S5_SETS, S5_SET_GROUPS, 1, 2 * S5_STATE)
        return t.transpose(1, 0, 2, 3, 4)
    lb = power_rows(pr[S5_CHUNK - 1::-1]) * b_with_re + power_rows(pi[S5_CHUNK - 1::-1]) * b_with_im
    lb = lb.reshape(S5_SETS, S5_CHUNK * LANES, 2 * S5_STATE)

    def rows_p(t):
        return t.astype(F32).reshape(S5_SETS, S5_SET_GROUPS, S5_GROUP, S5_STATE).transpose(0, 3, 1, 2)
    ct_re = rows_p(c_re)
    ct_im = rows_p(c_im)
    rc = jnp.concatenate([ct_re, -ct_im], axis=1).reshape(S5_SETS, 2 * S5_STATE, LANES)

    def power_lanes(t):
        return t.reshape(S5_CHUNK, S5_SETS, S5_SET_GROUPS, S5_STATE).transpose(1, 3, 0, 2)[..., None]
    prs = power_lanes(pr[1:])
    pis = power_lanes(pi[1:])
    ca_re = ct_re[:, :, None] * prs - ct_im[:, :, None] * pis
    ca_im = ct_re[:, :, None] * pis + ct_im[:, :, None] * prs
    cc = jnp.concatenate([ca_re, -ca_im], axis=1).reshape(S5_SETS, 2 * S5_STATE, S5_CHUNK * LANES)

    a16 = jnp.concatenate([pr[S5_CHUNK].reshape(S5_SETS, 1, S5_SET_STATE),
                           pi[S5_CHUNK].reshape(S5_SETS, 1, S5_SET_STATE)], axis=2)
    drow = jnp.tile(d_skip.astype(F32).reshape(S5_SETS, 1, LANES), (1, 1, S5_CHUNK))
    return lb, rc, cc, a16, drow


def _s5_layer(x, batch, w_in, lam_re, lam_im, log_dt, b_re, b_im, c_re, c_im, d_skip, w_out, w_gate, g, b):
    lb, rc, cc, a16, drow = _s5_tables(lam_re, lam_im, log_dt, b_re, b_im, c_re, c_im, d_skip)
    ug = _s5_in(x, w_in.astype(MXU_DTYPE), tm=512)
    hg = _s5_core(ug, lb, rc, cc, a16, drow, batch)
    return _s5_out(hg, w_out.astype(MXU_DTYPE), w_gate.astype(MXU_DTYPE), x, g, b, tm=512)


def _split_hi_lo(v):
    hi = v.astype(MXU_DTYPE)
    lo = (v - hi.astype(F32)).astype(MXU_DTYPE)
    return jnp.concatenate([hi, lo], axis=1)


def _ssd_core_kernel(z_ref, xs_ref, bc_ref, xin_ref, wdt_ref, cwx_ref, cwb_ref, cbx_ref, cbb_ref,
                     dtb_ref, a_ref, dsk_ref, nw_ref, e2_ref, o_ref,
                     xbuf_ref, bbuf_ref, st_ref, yd_ref):
    q = SSD_CHUNK
    pad = 8

    @pl.when(pl.program_id(1) == 0)
    def _():
        xbuf_ref[...] = jnp.zeros_like(xbuf_ref)
        bbuf_ref[...] = jnp.zeros_like(bbuf_ref)
        st_ref[...] = jnp.zeros_like(st_ref)

    def conv(in_ref, tail_ref, w_ref, b_ref):
        cur = in_ref[...]
        tail = tail_ref[...]
        row8 = lax.broadcasted_iota(jnp.int32, tail.shape, 0)
        acc = b_ref[...] + w_ref[SSD_CONV - 1:SSD_CONV, :] * cur
        for k in range(1, SSD_CONV):
            back = pltpu.roll(cur, k, axis=0)
            head = jnp.where(row8 < k, pltpu.roll(tail, k, axis=0), back[0:pad])
            back = jnp.concatenate([head, back[pad:]], axis=0)
            acc = acc + w_ref[SSD_CONV - 1 - k:SSD_CONV - k, :] * back
        tail_ref[...] = cur[q - pad:, :]
        return _silu(acc)

    xs = conv(xs_ref, xbuf_ref, cwx_ref, cbx_ref)
    bc = conv(bc_ref, bbuf_ref, cwb_ref, cbb_ref)

    dtr = _mm(xin_ref[...].astype(MXU_DTYPE), wdt_ref[...]) + dtb_ref[...]
    dt = jnp.maximum(dtr, 0.0) + jnp.log1p(jnp.exp(-jnp.abs(dtr)))
    adt = dt * a_ref[...]
    ri = lax.broadcasted_iota(jnp.int32, (q, q), 0)
    ci = lax.broadcasted_iota(jnp.int32, (q, q), 1)
    causal = ri >= ci
    tri = causal.astype(F32)
    cs = jnp.dot(tri, adt, preferred_element_type=F32, precision=HI)
    eye = (ri == ci).astype(F32)
    cs2 = cs * LOG2_E
    r_rows = _mm_nt(eye, cs2, precision=HI) - jnp.log2(_mm_nt(eye, dt, precision=HI))
    tot = cs[q - 1:q, :]
    w_col = dt * jnp.exp(tot - cs)
    din = jnp.exp(cs)
    w_x = _mm(_split_hi_lo(w_col), e2_ref[...])
    din_x = _mm(_split_hi_lo(din), e2_ref[...])

    lane = lax.broadcasted_iota(jnp.int32, (q, LANES), 1)
    low = lane < SSD_HEADDIM

    xw = (xs * w_x).astype(MXU_DTYPE)
    xsb = xs.astype(MXU_DTYPE)
    for g in range(SSD_GROUPS):
        bg = bc[:, g * SSD_STATE:(g + 1) * SSD_STATE].astype(MXU_DTYPE)
        cg = bc[:, SSD_GROUPS * SSD_STATE + g * SSD_STATE:
                SSD_GROUPS * SSD_STATE + (g + 1) * SSD_STATE].astype(MXU_DTYPE)
        cb = _mm_nt(cg, bg)
        for pair in range(SSD_HEADS_PER_GROUP // 2):
            h0 = g * SSD_HEADS_PER_GROUP + 2 * pair
            ms = []
            for h in (h0, h0 + 1):
                seg = cs2[:, h:h + 1] - r_rows[h:h + 1, :]
                ms.append((cb * jnp.exp2(jnp.where(causal, seg, -jnp.inf))).astype(MXU_DTYPE))
            lhs = jnp.concatenate(ms, axis=1)
            xp = xsb[:, h0 * SSD_HEADDIM:(h0 + 2) * SSD_HEADDIM]
            zero = jnp.zeros_like(xp)
            rhs = jnp.concatenate([jnp.where(low, xp, zero), jnp.where(low, zero, xp)], axis=0)
            yd_ref[:, h0 * SSD_HEADDIM:(h0 + 2) * SSD_HEADDIM] = _mm(lhs, rhs)
        gsl = slice(g * SSD_GROUP_DIM, (g + 1) * SSD_GROUP_DIM)
        st = st_ref[g]
        y_off = _mm(cg, st.astype(MXU_DTYPE)) * din_x[:, gsl]
        yd_ref[:, gsl] = yd_ref[:, gsl] + y_off
        st_ref[g] = din_x[q - 1:q, gsl] * st + _mm_tn(bg, xw[:, gsl])

    y = yd_ref[...] + dsk_ref[...] * xs
    y = y * _silu(z_ref[...])
    for g in range(SSD_GROUPS):
        gsl = slice(g * SSD_GROUP_DIM, (g + 1) * SSD_GROUP_DIM)
        yg = y[:, gsl]
        yg = yg * lax.rsqrt(jnp.mean(yg * yg, axis=-1, keepdims=True) + LN_EPS)
        o_ref[:, gsl] = (yg * nw_ref[:, gsl]).astype(o_ref.dtype)


def _ssd_core(proj, x, w_dt, batch, conv_w, conv_b, dt_bias, a_log, d_skip, norm_w):
    m = proj.shape[0]
    q = SSD_CHUNK
    ncb = m // batch // q
    di = SSD_D_INNER
    wdt = jnp.pad(w_dt.astype(MXU_DTYPE), ((0, 0), (0, LANES - SSD_HEADS)))
    cw = conv_w.astype(F32)
    cbias = conv_b.astype(F32)[None, :]
    padh = LANES - SSD_HEADS
    dtb = jnp.pad(dt_bias.astype(F32), (0, padh))[None, :]
    a = jnp.pad(-jnp.exp(a_log.astype(F32)), (0, padh))[None, :]
    dsk = jnp.repeat(d_skip.astype(F32), SSD_HEADDIM)[None, :]
    nw = norm_w.astype(F32)[None, :]
    expand = (jnp.arange(LANES)[:, None] == (jnp.arange(di) // SSD_HEADDIM)[None, :]).astype(MXU_DTYPE)
    e2 = jnp.concatenate([expand, expand], axis=0)

    row = lambda b, c: b * ncb + c
    full = _resident
    return pl.pallas_call(
        _ssd_core_kernel,
        grid=(batch, ncb),
        in_specs=[pl.BlockSpec((q, di), lambda b, c: (row(b, c), 0)),
                  pl.BlockSpec((q, di), lambda b, c: (row(b, c), 1)),
                  pl.BlockSpec((q, SSD_BC), lambda b, c: (row(b, c), 2 * di // SSD_BC)),
                  pl.BlockSpec((q, D_MODEL), lambda b, c: (row(b, c), 0)),
                  full((D_MODEL, LANES)),
                  full((SSD_CONV, di)), full((SSD_CONV, SSD_BC)), full((1, di)), full((1, SSD_BC)),
                  full((1, LANES)), full((1, LANES)), full((1, di)), full((1, di)),
                  full((2 * LANES, di))],
        out_specs=pl.BlockSpec((q, di), lambda b, c: (row(b, c), 0)),
        out_shape=jax.ShapeDtypeStruct((m, di), MXU_DTYPE),
        scratch_shapes=[pltpu.VMEM((8, di), F32), pltpu.VMEM((8, SSD_BC), F32),
                        pltpu.VMEM((SSD_GROUPS, SSD_STATE, SSD_GROUP_DIM), F32),
                        pltpu.VMEM((q, di), F32)],
        compiler_params=_params("parallel", "arbitrary"),
        name="ssd_core",
    )(proj, proj, proj, x, wdt, cw[:, :di], cw[:, di:], cbias[:, :di], cbias[:, di:], dtb, a, dsk, nw, e2)


def _ssd_layer(x, batch, w_in_stack, layer, conv_w, conv_b, dt_bias, a_log, d_skip, norm_w, w_out, g, b):
    n_main = SSD_D_INNER + SSD_CONV_DIM
    proj = _proj(x, w_in_stack.astype(MXU_DTYPE), layer, n_main, tm=min(1024, x.shape[0]), tn=1024,
                 out_dtype=F32)
    w_dt = w_in_stack[layer, :, n_main:]
    y = _ssd_core(proj, x, w_dt, batch, conv_w, conv_b, dt_bias, a_log, d_skip, norm_w)
    return _out_ln(y, w_out.astype(MXU_DTYPE), x, g, b, tm=512)


def _ret_core_kernel(cd_ref, q_ref, k_ref, v_ref, g_ref, cos_ref, sin_ref, dmat_ref, xi_ref, zeta_ref,
                     o_ref, st_ref):
    @pl.when(pl.program_id(1) == 0)
    def _():
        st_ref[...] = jnp.zeros_like(st_ref)

    cos = cos_ref[...]
    sin = sin_ref[...]
    half = RET_DK // 2

    def rotate(t):
        t1 = t[:, :half]
        t2 = t[:, half:]
        return jnp.concatenate([t1 * cos - t2 * sin, t1 * sin + t2 * cos], axis=1)

    for h in range(RET_HEADS):
        qh = rotate(q_ref[:, h * RET_DK:(h + 1) * RET_DK].astype(F32))
        kh = rotate(k_ref[:, h * RET_DK:(h + 1) * RET_DK].astype(F32) * (RET_DK ** -0.5))
        vh = v_ref[:, h * RET_DV:(h + 1) * RET_DV].astype(MXU_DTYPE)
        qb = qh.astype(MXU_DTYPE)
        scores = _mm_nt(qb, kh.astype(MXU_DTYPE)) * dmat_ref[h]
        inner = _mm(scores.astype(MXU_DTYPE), vh)
        st = st_ref[h]
        xi = jnp.concatenate([xi_ref[h]] * (RET_DV // LANES), axis=1)
        cross = _mm(qb, st.astype(MXU_DTYPE)) * xi
        kz = kh * jnp.concatenate([zeta_ref[h]] * (RET_DK // LANES), axis=1)
        st_ref[h] = cd_ref[h] * st + _mm_tn(kz.astype(MXU_DTYPE), vh)
        out = inner + cross
        mu = jnp.mean(out, axis=-1, keepdims=True)
        d = out - mu
        var = jnp.mean(d * d, axis=-1, keepdims=True)
        out = d * lax.rsqrt(var + LN_EPS)
        gate = _silu(g_ref[:, h * RET_DV:(h + 1) * RET_DV].astype(F32))
        o_ref[:, h * RET_DV:(h + 1) * RET_DV] = (gate * out).astype(o_ref.dtype)


def _ret_core(proj, batch):
    m = proj.shape[0]
    seq = m // batch
    q = RET_CHUNK
    ncb = seq // q
    theta = 1.0 / (RET_ROPE_BASE ** jnp.linspace(0.0, 1.0, RET_DK // 2, dtype=F32))
    ang = jnp.arange(seq, dtype=F32)[:, None] * theta[None, :]
    cos, sin = jnp.cos(ang), jnp.sin(ang)
    lg = jnp.log1p(-jnp.exp2(-5.0 - jnp.arange(RET_HEADS, dtype=F32)))
    pos = jnp.arange(q, dtype=F32)
    diff = pos[:, None] - pos[None, :]
    dmat = jnp.where(diff >= 0, jnp.exp(jnp.maximum(diff, 0.0)[None] * lg[:, None, None]), 0.0)
    xi = jnp.exp((pos[None, :] + 1.0) * lg[:, None])
    zeta = jnp.exp((q - 1.0 - pos[None, :]) * lg[:, None])
    xi_b = jnp.broadcast_to(xi[:, :, None], (RET_HEADS, q, LANES))
    zeta_b = jnp.broadcast_to(zeta[:, :, None], (RET_HEADS, q, LANES))
    chunk_decay = jnp.exp(q * lg)

    row = lambda b, c: b * ncb + c
    d = D_MODEL
    return pl.pallas_call(
        _ret_core_kernel,
        grid=(batch, ncb),
        in_specs=[pl.BlockSpec(memory_space=pltpu.SMEM),
                  pl.BlockSpec((q, d), lambda b, c: (row(b, c), 0)),
                  pl.BlockSpec((q, d), lambda b, c: (row(b, c), 1)),
                  pl.BlockSpec((q, 2 * d), lambda b, c: (row(b, c), 1)),
                  pl.BlockSpec((q, 2 * d), lambda b, c: (row(b, c), 2)),
                  pl.BlockSpec((q, RET_DK // 2), lambda b, c: (c, 0)),
                  pl.BlockSpec((q, RET_DK // 2), lambda b, c: (c, 0)),
                  _resident((RET_HEADS, q, q)),
                  _resident((RET_HEADS, q, LANES)),
                  _resident((RET_HEADS, q, LANES))],
        out_specs=pl.BlockSpec((q, 2 * d), lambda b, c: (row(b, c), 0)),
        out_shape=jax.ShapeDtypeStruct((m, 2 * d), MXU_DTYPE),
        scratch_shapes=[pltpu.VMEM((RET_HEADS, RET_DK, RET_DV), F32)],
        compiler_params=_params("parallel", "arbitrary"),
        name="ret_core",
    )(chunk_decay, proj, proj, proj, proj, cos, sin, dmat, xi_b, zeta_b)


def _ret_layer(x, batch, w_in_stack, layer, w_out, g, b):
    proj = _proj(x, w_in_stack, layer, RET_IN_DIM, tm=min(1024, x.shape[0]), tn=1024, out_dtype=MXU_DTYPE)
    y = _ret_core(proj, batch)
    return _out_ln(y, w_out.astype(MXU_DTYPE), x, g, b, tm=512)


def kernel(x, ln1_g, ln1_b, ln2_g, ln2_b, mlp_w1, mlp_w2, s5_w_in, s5_lam_re, s5_lam_im, s5_log_dt, s5_b_re, s5_b_im, s5_c_re, s5_c_im, s5_d, s5_w_out, s5_w_gate, ssd_w_in, ssd_conv_w, ssd_conv_b, ssd_dt_bias, ssd_a_log, ssd_d, ssd_norm_w, ssd_w_out, ret_w_in, ret_w_out):
    batch, seq, d = x.shape
    h = x.reshape(batch * seq, d)
    for i in range(DEPTH):
        kind = i % 3
        j = i // 3
        g1 = ln1_g[i][None, :]
        b1 = ln1_b[i][None, :]
        if kind == 0:
            h = _s5_layer(h, batch, s5_w_in[j], s5_lam_re[j], s5_lam_im[j], s5_log_dt[j], s5_b_re[j],
                          s5_b_im[j], s5_c_re[j], s5_c_im[j], s5_d[j], s5_w_out[j], s5_w_gate[j], g1, b1)
        elif kind == 1:
            h = _ssd_layer(h, batch, ssd_w_in, j, ssd_conv_w[j], ssd_conv_b[j], ssd_dt_bias[j], ssd_a_log[j],
                           ssd_d[j], ssd_norm_w[j], ssd_w_out[j], g1, b1)
        else:
            h = _ret_layer(h, batch, ret_w_in, j, ret_w_out[j], g1, b1)
        h = _mlp(h, mlp_w1, mlp_w2, i, ln2_g[i][None, :], ln2_b[i][None, :],
                 tm=min(1024, h.shape[0]), tf=512)
    return h.reshape(batch, seq, d)
```

```python
import functools
import math

import jax
import jax.numpy as jnp
import numpy as np
from jax import lax
from jax.experimental import pallas as pl
from jax.experimental.pallas import tpu as pltpu

F32 = jnp.float32
MXU_DTYPE = jnp.bfloat16
HI = lax.Precision.HIGHEST
LOG2_E = 1.0 / math.log(2.0)

LANES = 128
VMEM_LIMIT_BYTES = 58 * 1024 * 1024

D_MODEL = 2048
DEPTH = 4
DN_ALPHA = (2 * DEPTH) ** 0.25
LN_EPS = 1e-5
D_FF = 4 * D_MODEL
MLP_OUT_SLABS = 4
ROW_SPLIT = 2

S5_GROUP = 16
S5_GROUPS = D_MODEL // S5_GROUP
S5_STATE = 64
S5_CHUNK = 16
S5_SETS = D_MODEL // LANES
S5_SET_GROUPS = LANES // S5_GROUP
S5_SET_STATE = S5_SET_GROUPS * S5_STATE
S5_SCAN_UNROLL = 8
S5_TOEPLITZ_GROUP = 4

SSD_D_INNER = 2 * D_MODEL
SSD_HEADDIM = 64
SSD_HEADS = SSD_D_INNER // SSD_HEADDIM
SSD_GROUPS = 8
SSD_STATE = 128
SSD_CONV = 4
SSD_CHUNK = 128
SSD_BC = 2 * SSD_GROUPS * SSD_STATE
SSD_CONV_DIM = SSD_D_INNER + SSD_BC
SSD_IN_DIM = SSD_D_INNER + SSD_CONV_DIM + SSD_HEADS
SSD_IN_PAD = SSD_D_INNER + SSD_CONV_DIM + LANES
SSD_GROUP_DIM = SSD_D_INNER // SSD_GROUPS
SSD_HEADS_PER_GROUP = SSD_HEADS // SSD_GROUPS

RET_HEADS = 8
RET_DK = D_MODEL // RET_HEADS
RET_DV = 2 * D_MODEL // RET_HEADS
RET_CHUNK = 128
RET_IN_DIM = 6 * D_MODEL
RET_ROPE_BASE = 10000.0


def _params(*semantics):
    return pltpu.CompilerParams(dimension_semantics=semantics, vmem_limit_bytes=VMEM_LIMIT_BYTES)


def _layer_norm(v, g, b):
    mu = jnp.mean(v, axis=-1, keepdims=True)
    d = v - mu
    var = jnp.mean(d * d, axis=-1, keepdims=True)
    return d * lax.rsqrt(var + LN_EPS) * g + b


def _silu(v):
    h = 0.5 * v
    return h + h * jnp.tanh(h)


def _mm(a, b):
    return jnp.dot(a, b, preferred_element_type=F32)


def _mm_nt(a, b, precision=None):
    return lax.dot_general(a, b, (((1,), (1,)), ((), ())), preferred_element_type=F32, precision=precision)


def _mm_tn(a, b):
    return lax.dot_general(a, b, (((0,), (0,)), ((), ())), preferred_element_type=F32)


def _proj_kernel(x_ref, w_ref, o_ref, xb_ref):
    @pl.when(pl.program_id(1) == 0)
    def _():
        xb_ref[...] = x_ref[...].astype(MXU_DTYPE)

    o_ref[...] = _mm(xb_ref[...], w_ref[...].astype(MXU_DTYPE)).astype(o_ref.dtype)


def _proj(x, w_stack, layer, n, tm, tn, out_dtype):
    m, k = x.shape
    return pl.pallas_call(
        _proj_kernel,
        grid=(m // tm, n // tn),
        in_specs=[pl.BlockSpec((tm, k), lambda i, j: (i, 0)),
                  pl.BlockSpec((None, k, tn), lambda i, j: (layer, 0, j))],
        out_specs=pl.BlockSpec((tm, tn), lambda i, j: (i, j)),
        out_shape=jax.ShapeDtypeStruct((m, n), out_dtype),
        scratch_shapes=[pltpu.VMEM((tm, k), MXU_DTYPE)],
        compiler_params=_params("parallel", "arbitrary"),
        name="proj",
    )(x, w_stack)


def _out_ln_kernel(y_ref, w_ref, x_ref, g_ref, b_ref, o_ref):
    half = o_ref.shape[0] // ROW_SPLIT
    for r in range(ROW_SPLIT):
        rows = slice(r * half, (r + 1) * half)
        f = _mm(y_ref[rows, :], w_ref[...])
        o_ref[rows, :] = _layer_norm(DN_ALPHA * x_ref[rows, :] + f, g_ref[...], b_ref[...])


def _resident(shape):
    return pl.BlockSpec(shape, lambda *_: (0,) * len(shape), pipeline_mode=pl.Buffered(1))


def _layer_row(shape, layer):
    return pl.BlockSpec((None,) + shape, lambda *_: (layer, 0, 0), pipeline_mode=pl.Buffered(1))


def _out_ln(y, w, x, ln, tm):
    m, k = y.shape
    d = w.shape[1]
    g, b, layer = ln
    return pl.pallas_call(
        _out_ln_kernel,
        grid=(m // tm,),
        in_specs=[pl.BlockSpec((tm, k), lambda i: (i, 0)),
                  _resident((k, d)),
                  pl.BlockSpec((tm, d), lambda i: (i, 0)),
                  _layer_row((1, d), layer),
                  _layer_row((1, d), layer)],
        out_specs=pl.BlockSpec((tm, d), lambda i: (i, 0)),
        out_shape=jax.ShapeDtypeStruct((m, d), F32),
        compiler_params=_params("parallel"),
        name="out_ln",
    )(y, w, x, g, b)


def _mlp_kernel(x_ref, w1_ref, w2_ref, g_ref, b_ref, o_ref, xb_ref, *, nf):
    j = pl.program_id(1)

    @pl.when(j == 0)
    def _():
        xb_ref[...] = x_ref[...].astype(MXU_DTYPE)
        o_ref[...] = jnp.zeros_like(o_ref)

    h = _mm(xb_ref[...], w1_ref[...].astype(MXU_DTYPE))
    h = jnp.square(jnp.maximum(h, 0.0)).astype(MXU_DTYPE)
    slab = o_ref.shape[1] // MLP_OUT_SLABS
    for s in range(MLP_OUT_SLABS):
        cols = slice(s * slab, (s + 1) * slab)
        o_ref[:, cols] += _mm(h, w2_ref[:, cols].astype(MXU_DTYPE))

    @pl.when(j == nf - 1)
    def _():
        o_ref[...] = _layer_norm(DN_ALPHA * x_ref[...] + o_ref[...], g_ref[...], b_ref[...])


def _mlp(x, w1_stack, w2_stack, layer, ln, tm, tf):
    m, d = x.shape
    f = w1_stack.shape[2]
    nf = f // tf
    g, b, ln_layer = ln
    return pl.pallas_call(
        functools.partial(_mlp_kernel, nf=nf),
        grid=(m // tm, nf),
        in_specs=[pl.BlockSpec((tm, d), lambda i, j: (i, 0), pipeline_mode=pl.Buffered(1)),
                  pl.BlockSpec((None, d, tf), lambda i, j: (layer, 0, j)),
                  pl.BlockSpec((None, tf, d), lambda i, j: (layer, j, 0)),
                  _layer_row((1, d), ln_layer),
                  _layer_row((1, d), ln_layer)],
        out_specs=pl.BlockSpec((tm, d), lambda i, j: (i, 0)),
        out_shape=jax.ShapeDtypeStruct((m, d), F32),
        scratch_shapes=[pltpu.VMEM((tm, d), MXU_DTYPE)],
        compiler_params=_params("parallel", "arbitrary"),
        name="mlp",
    )(x, w1_stack, w2_stack, g, b)


def _s5_in_kernel(x_ref, w_ref, o_ref):
    u = _mm(x_ref[...].astype(MXU_DTYPE), w_ref[...])
    for gs in range(S5_SETS):
        o_ref[gs] = u[:, gs * LANES:(gs + 1) * LANES]


def _s5_in(x, w, layer, tm):
    m = x.shape[0]
    return pl.pallas_call(
        _s5_in_kernel,
        grid=(m // tm,),
        in_specs=[pl.BlockSpec((tm, D_MODEL), lambda i: (i, 0)),
                  _layer_row((D_MODEL, D_MODEL), layer)],
        out_specs=pl.BlockSpec((S5_SETS, tm, LANES), lambda i: (0, i, 0)),
        out_shape=jax.ShapeDtypeStruct((S5_SETS, m, LANES), F32),
        compiler_params=_params("parallel"),
        name="s5_in",
    )(x, w)


def _s5_core_kernel(u_ref, lb_ref, rc_ref, cc_ref, a16_ref, d_ref, o_ref,
                    wt_ref, win_ref, wout_ref, v_ref, sp_ref, yin_ref, *, batch, chunks_per_batch):
    nst = S5_SET_STATE
    kdim = S5_CHUNK * LANES
    lb = lb_ref[0]

    rc = rc_ref[0]
    lb_hi = lb.astype(MXU_DTYPE)
    lb_lo = (lb - lb_hi.astype(F32)).astype(MXU_DTYPE)
    rc_hi = rc.astype(MXU_DTYPE)
    rc_lo = (rc - rc_hi.astype(F32)).astype(MXU_DTYPE)
    drev = _mm(jnp.concatenate([lb_hi, lb_lo], axis=1), jnp.concatenate([rc_hi, rc_hi], axis=0))
    drev = drev + _mm(lb_hi, rc_lo)
    row = lax.broadcasted_iota(jnp.int32, (kdim, LANES), 0)
    lane = lax.broadcasted_iota(jnp.int32, (kdim, LANES), 1)
    row_group = (row % LANES) // S5_GROUP
    drev = jnp.where(row_group == lane // S5_GROUP, drev, 0.0).astype(MXU_DTYPE)
    for t in range(S5_CHUNK):
        used = (t + 1) * LANES
        group_rows = (t // S5_TOEPLITZ_GROUP + 1) * S5_TOEPLITZ_GROUP * LANES
        wt_ref[0:used, t * LANES:(t + 1) * LANES] = drev[kdim - used:, :]
        if used < group_rows:
            wt_ref[used:group_rows, t * LANES:(t + 1) * LANES] = jnp.zeros((group_rows - used, LANES), MXU_DTYPE)

    swapped = pltpu.roll(lb, S5_STATE, axis=1)
    low = lane < S5_STATE
    re_dup = jnp.where(low, lb, swapped)
    im_dup = jnp.where(low, swapped, lb)
    for v in range(nst // LANES):
        sel = row_group == 2 * v + (lane >= S5_STATE).astype(jnp.int32)
        win_ref[:, v * LANES:(v + 1) * LANES] = jnp.where(sel, re_dup, 0.0).astype(MXU_DTYPE)
        win_ref[:, nst + v * LANES:nst + (v + 1) * LANES] = jnp.where(sel, im_dup, 0.0).astype(MXU_DTYPE)

    cc = cc_ref[0]
    lane_group = (lax.broadcasted_iota(jnp.int32, (S5_STATE, kdim), 1) % LANES) // S5_GROUP
    for ri in range(2):
        part = cc[ri * S5_STATE:(ri + 1) * S5_STATE, :]
        for g in range(S5_SET_GROUPS):
            r0 = ri * nst + g * S5_STATE
            wout_ref[r0:r0 + S5_STATE, :] = jnp.where(lane_group == g, part, 0.0).astype(MXU_DTYPE)

    nchunks = batch * chunks_per_batch
    u = jnp.concatenate([u_ref[pl.ds(t, nchunks, stride=S5_CHUNK), :] for t in range(S5_CHUNK)], axis=1)
    xb = u.astype(MXU_DTYPE)
    vin = _mm(xb, win_ref[...])
    ntile = nst // LANES
    for k in range(2 * ntile):
        v_ref[k] = vin[:, k * LANES:(k + 1) * LANES]
    gcols = S5_TOEPLITZ_GROUP * LANES
    for j in range(S5_CHUNK // S5_TOEPLITZ_GROUP):
        cols = slice(j * gcols, (j + 1) * gcols)
        krows = (j + 1) * gcols
        yin_ref[:, cols] = _mm(xb[:, :krows], wt_ref[0:krows, cols]) + d_ref[0][:, cols] * u[:, cols]

    a_re = [a16_ref[0, :, k * LANES:(k + 1) * LANES] for k in range(ntile)]
    a_im = [a16_ref[0, :, nst + k * LANES:nst + (k + 1) * LANES] for k in range(ntile)]

    def step(c, carry):
        rows = pl.ds(c, batch, stride=chunks_per_batch)
        new = []
        for k in range(ntile):
            s_re, s_im = carry[k], carry[ntile + k]
            sp_ref[k, rows, :] = s_re
            sp_ref[ntile + k, rows, :] = s_im
            new.append((a_re[k] * s_re - a_im[k] * s_im + v_ref[k, rows, :],
                        a_re[k] * s_im + a_im[k] * s_re + v_ref[ntile + k, rows, :]))
        return tuple(n[0] for n in new) + tuple(n[1] for n in new)

    zero = jnp.zeros((batch, LANES), F32)
    lax.fori_loop(0, chunks_per_batch, step, (zero,) * (2 * ntile), unroll=S5_SCAN_UNROLL)

    sp = jnp.concatenate([sp_ref[k] for k in range(2 * ntile)], axis=1)
    h = jax.nn.gelu(yin_ref[...] + _mm(sp.astype(MXU_DTYPE), wout_ref[...]))
    for t in range(S5_CHUNK):
        o_ref[pl.ds(t, nchunks, stride=S5_CHUNK), :] = h[:, t * LANES:(t + 1) * LANES]


def _s5_core(ug, tables, layer, batch):
    nsets, m, _ = ug.shape
    nchunks = m // S5_CHUNK
    kdim = S5_CHUNK * LANES
    kern = functools.partial(_s5_core_kernel, batch=batch, chunks_per_batch=nchunks // batch)
    table = lambda rows, cols: pl.BlockSpec((None, 1, rows, cols), lambda s: (layer, s, 0, 0))
    return pl.pallas_call(
        kern,
        grid=(nsets,),
        in_specs=[pl.BlockSpec((None, m, LANES), lambda s: (s, 0, 0)),
                  table(kdim, LANES),
                  table(LANES, LANES),
                  table(LANES, kdim),
                  table(1, 2 * S5_SET_STATE),
                  table(1, kdim)],
        out_specs=pl.BlockSpec((None, m, LANES), lambda s: (s, 0, 0)),
        out_shape=jax.ShapeDtypeStruct((nsets, m, LANES), F32),
        scratch_shapes=[pltpu.VMEM((kdim, kdim), MXU_DTYPE),
                        pltpu.VMEM((kdim, 2 * S5_SET_STATE), MXU_DTYPE),
                        pltpu.VMEM((2 * S5_SET_STATE, kdim), MXU_DTYPE),
                        pltpu.VMEM((2 * S5_SET_STATE // LANES, nchunks, LANES), F32),
                        pltpu.VMEM((2 * S5_SET_STATE // LANES, nchunks, LANES), F32),
                        pltpu.VMEM((nchunks, kdim), F32)],
        compiler_params=_params("parallel"),
        name="s5_core",
    )(ug, *tables)


def _s5_out_kernel(h_ref, wo_ref, wg_ref, x_ref, g_ref, b_ref, o_ref):
    half = o_ref.shape[0] // ROW_SPLIT
    for r in range(ROW_SPLIT):
        rows = slice(r * half, (r + 1) * half)
        h = jnp.concatenate([h_ref[gs, rows, :] for gs in range(S5_SETS)], axis=1).astype(MXU_DTYPE)
        f = _mm(h, wo_ref[...]) * jax.nn.sigmoid(_mm(h, wg_ref[...]))
        o_ref[rows, :] = _layer_norm(DN_ALPHA * x_ref[rows, :] + f, g_ref[...], b_ref[...])


def _s5_out(hg, wo, wg, w_layer, x, ln, tm):
    m = x.shape[0]
    g, b, layer = ln
    return pl.pallas_call(
        _s5_out_kernel,
        grid=(m // tm,),
        in_specs=[pl.BlockSpec((S5_SETS, tm, LANES), lambda i: (0, i, 0)),
                  _layer_row((D_MODEL, D_MODEL), w_layer),
                  _layer_row((D_MODEL, D_MODEL), w_layer),
                  pl.BlockSpec((tm, D_MODEL), lambda i: (i, 0)),
                  _layer_row((1, D_MODEL), layer),
                  _layer_row((1, D_MODEL), layer)],
        out_specs=pl.BlockSpec((tm, D_MODEL), lambda i: (i, 0)),
        out_shape=jax.ShapeDtypeStruct(x.shape, F32),
        compiler_params=_params("parallel"),
        name="s5_out",
    )(hg, wo, wg, x, g, b)


def _s5_tables(lam_re, lam_im, log_dt, b_re, b_im, c_re, c_im, d_skip):
    lr = lam_re.astype(F32)
    li = lam_im.astype(F32)
    dt = jnp.exp(log_dt.astype(F32))[:, None]
    mag = jnp.exp(lr * dt)
    ar = mag * jnp.cos(li * dt)
    ai = mag * jnp.sin(li * dt)
    den = lr * lr + li * li
    zr = ((ar - 1.0) * lr + ai * li) / den
    zi = (ai * lr - (ar - 1.0) * li) / den
    br_ = b_re.astype(F32)
    bi_ = b_im.astype(F32)
    bbr = zr[..., None] * br_ - zi[..., None] * bi_
    bbi = zr[..., None] * bi_ + zi[..., None] * br_
    k = jnp.arange(S5_CHUNK + 1, dtype=F32)[:, None, None]
    pmag = jnp.exp(k * (lr * dt))
    pr = pmag * jnp.cos(k * (li * dt))
    pi = pmag * jnp.sin(k * (li * dt))

    bt_re = jnp.swapaxes(bbr, 1, 2)
    bt_im = jnp.swapaxes(bbi, 1, 2)
    shape_b = (S5_SETS, 1, S5_SET_GROUPS, S5_GROUP, 2 * S5_STATE)
    b_with_re = jnp.concatenate([bt_re, bt_im], axis=-1).reshape(shape_b)
    b_with_im = jnp.concatenate([-bt_im, bt_re], axis=-1).reshape(shape_b)
    def power_rows(t):
        t = jnp.concatenate([t, t], axis=-1).reshape(S5_CHUNK, S5_SETS, S5_SET_GROUPS, 1, 2 * S5_STATE)
        return t.transpose(1, 0, 2, 3, 4)
    lb = power_rows(pr[S5_CHUNK - 1::-1]) * b_with_re + power_rows(pi[S5_CHUNK - 1::-1]) * b_with_im
    lb = lb.reshape(S5_SETS, S5_CHUNK * LANES, 2 * S5_STATE)

    def rows_p(t):
        return t.astype(F32).reshape(S5_SETS, S5_SET_GROUPS, S5_GROUP, S5_STATE).transpose(0, 3, 1, 2)
    ct_re = rows_p(c_re)
    ct_im = rows_p(c_im)
    rc = jnp.concatenate([ct_re, -ct_im], axis=1).reshape(S5_SETS, 2 * S5_STATE, LANES)

    def power_lanes(t):
        return t.reshape(S5_CHUNK, S5_SETS, S5_SET_GROUPS, S5_STATE).transpose(1, 3, 0, 2)[..., None]
    prs = power_lanes(pr[1:])
    pis = power_lanes(pi[1:])
    ca_re = ct_re[:, :, None] * prs - ct_im[:, :, None] * pis
    ca_im = ct_re[:, :, None] * pis + ct_im[:, :, None] * prs
    cc = jnp.concatenate([ca_re, -ca_im], axis=1).reshape(S5_SETS, 2 * S5_STATE, S5_CHUNK * LANES)

    a16 = jnp.concatenate([pr[S5_CHUNK].reshape(S5_SETS, 1, S5_SET_STATE),
                           pi[S5_CHUNK].reshape(S5_SETS, 1, S5_SET_STATE)], axis=2)
    drow = jnp.tile(d_skip.astype(F32).reshape(S5_SETS, 1, LANES), (1, 1, S5_CHUNK))
    return lb, rc, cc, a16, drow


def _s5_layer(x, batch, w_in, tables, layer, w_out, w_gate, ln):
    ug = _s5_in(x, w_in, layer, tm=512)
    hg = _s5_core(ug, tables, layer, batch)
    return _s5_out(hg, w_out, w_gate, layer, x, ln, tm=512)


def _split_hi_lo(v):
    hi = v.astype(MXU_DTYPE)
    lo = (v - hi.astype(F32)).astype(MXU_DTYPE)
    return jnp.concatenate([hi, lo], axis=1)


def _ssd_core_kernel(z_ref, xs_ref, bc_ref, xin_ref, wdt_ref, cwx_ref, cwb_ref, cbx_ref, cbb_ref,
                     dtb_ref, a_ref, dsk_ref, nw_ref, e2_ref, o_ref,
                     xbuf_ref, bbuf_ref, st_ref, yd_ref):
    q = SSD_CHUNK
    pad = 8

    @pl.when(pl.program_id(1) == 0)
    def _():
        xbuf_ref[...] = jnp.zeros_like(xbuf_ref)
        bbuf_ref[...] = jnp.zeros_like(bbuf_ref)
        st_ref[...] = jnp.zeros_like(st_ref)

    def conv(in_ref, tail_ref, w_ref, b_ref):
        cur = in_ref[...]
        tail = tail_ref[...]
        row8 = lax.broadcasted_iota(jnp.int32, tail.shape, 0)
        acc = b_ref[...] + w_ref[SSD_CONV - 1:SSD_CONV, :] * cur
        for k in range(1, SSD_CONV):
            back = pltpu.roll(cur, k, axis=0)
            head = jnp.where(row8 < k, pltpu.roll(tail, k, axis=0), back[0:pad])
            back = jnp.concatenate([head, back[pad:]], axis=0)
            acc = acc + w_ref[SSD_CONV - 1 - k:SSD_CONV - k, :] * back
        tail_ref[...] = cur[q - pad:, :]
        return _silu(acc)

    xs = conv(xs_ref, xbuf_ref, cwx_ref, cbx_ref)
    bc = conv(bc_ref, bbuf_ref, cwb_ref, cbb_ref)

    dtr = _mm(xin_ref[...].astype(MXU_DTYPE), wdt_ref[...]) + dtb_ref[...]
    dt = jnp.maximum(dtr, 0.0) + jnp.log1p(jnp.exp(-jnp.abs(dtr)))
    adt = dt * a_ref[...]
    ri = lax.broadcasted_iota(jnp.int32, (q, q), 0)
    ci = lax.broadcasted_iota(jnp.int32, (q, q), 1)
    causal = ri >= ci
    tri = causal.astype(F32)
    cs = jnp.dot(tri, adt, preferred_element_type=F32, precision=HI)
    eye = (ri == ci).astype(F32)
    cs2 = cs * LOG2_E
    r_rows = _mm_nt(eye, cs2, precision=HI) - jnp.log2(_mm_nt(eye, dt, precision=HI))
    tot = cs[q - 1:q, :]
    w_col = dt * jnp.exp(tot - cs)
    din = jnp.exp(cs)
    w_x = _mm(_split_hi_lo(w_col), e2_ref[...])
    din_x = _mm(_split_hi_lo(din), e2_ref[...])

    lane = lax.broadcasted_iota(jnp.int32, (q, LANES), 1)
    low = lane < SSD_HEADDIM

    xw = (xs * w_x).astype(MXU_DTYPE)
    xsb = xs.astype(MXU_DTYPE)
    for g in range(SSD_GROUPS):
        bg = bc[:, g * SSD_STATE:(g + 1) * SSD_STATE].astype(MXU_DTYPE)
        cg = bc[:, SSD_GROUPS * SSD_STATE + g * SSD_STATE:
                SSD_GROUPS * SSD_STATE + (g + 1) * SSD_STATE].astype(MXU_DTYPE)
        cb = _mm_nt(cg, bg)
        for pair in range(SSD_HEADS_PER_GROUP // 2):
            h0 = g * SSD_HEADS_PER_GROUP + 2 * pair
            ms = []
            for h in (h0, h0 + 1):
                seg = cs2[:, h:h + 1] - r_rows[h:h + 1, :]
                ms.append((cb * jnp.exp2(jnp.where(causal, seg, -jnp.inf))).astype(MXU_DTYPE))
            lhs = jnp.concatenate(ms, axis=1)
            xp = xsb[:, h0 * SSD_HEADDIM:(h0 + 2) * SSD_HEADDIM]
            zero = jnp.zeros_like(xp)
            rhs = jnp.concatenate([jnp.where(low, xp, zero), jnp.where(low, zero, xp)], axis=0)
            yd_ref[:, h0 * SSD_HEADDIM:(h0 + 2) * SSD_HEADDIM] = _mm(lhs, rhs)
        gsl = slice(g * SSD_GROUP_DIM, (g + 1) * SSD_GROUP_DIM)
        st = st_ref[g]
        y_off = _mm(cg, st.astype(MXU_DTYPE)) * din_x[:, gsl]
        yd_ref[:, gsl] = yd_ref[:, gsl] + y_off
        st_ref[g] = din_x[q - 1:q, gsl] * st + _mm_tn(bg, xw[:, gsl])

    y = yd_ref[...] + dsk_ref[...] * xs
    y = y * _silu(z_ref[...])
    for g in range(SSD_GROUPS):
        gsl = slice(g * SSD_GROUP_DIM, (g + 1) * SSD_GROUP_DIM)
        yg = y[:, gsl]
        yg = yg * lax.rsqrt(jnp.mean(yg * yg, axis=-1, keepdims=True) + LN_EPS)
        o_ref[:, gsl] = (yg * nw_ref[:, gsl]).astype(o_ref.dtype)


def _ssd_core(proj, x, w_dt, batch, conv_w, conv_b, dt_bias, a_log, d_skip, norm_w):
    m = proj.shape[0]
    q = SSD_CHUNK
    ncb = m // batch // q
    di = SSD_D_INNER
    wdt = jnp.pad(w_dt.astype(MXU_DTYPE), ((0, 0), (0, LANES - SSD_HEADS)))
    cw = conv_w.astype(F32)
    cbias = conv_b.astype(F32)[None, :]
    padh = LANES - SSD_HEADS
    dtb = jnp.pad(dt_bias.astype(F32), (0, padh))[None, :]
    a = jnp.pad(-jnp.exp(a_log.astype(F32)), (0, padh))[None, :]
    dsk = jnp.repeat(d_skip.astype(F32), SSD_HEADDIM)[None, :]
    nw = norm_w.astype(F32)[None, :]
    expand = (np.arange(LANES)[:, None] == (np.arange(di) // SSD_HEADDIM)[None, :]).astype(np.float32)
    e2 = jnp.asarray(np.concatenate([expand, expand], axis=0), dtype=MXU_DTYPE)

    row = lambda b, c: b * ncb + c
    full = _resident
    return pl.pallas_call(
        _ssd_core_kernel,
        grid=(batch, ncb),
        in_specs=[pl.BlockSpec((q, di), lambda b, c: (row(b, c), 0)),
                  pl.BlockSpec((q, di), lambda b, c: (row(b, c), 1)),
                  pl.BlockSpec((q, SSD_BC), lambda b, c: (row(b, c), 2 * di // SSD_BC)),
                  pl.BlockSpec((q, D_MODEL), lambda b, c: (row(b, c), 0)),
                  full((D_MODEL, LANES)),
                  full((SSD_CONV, di)), full((SSD_CONV, SSD_BC)), full((1, di)), full((1, SSD_BC)),
                  full((1, LANES)), full((1, LANES)), full((1, di)), full((1, di)),
                  full((2 * LANES, di))],
        out_specs=pl.BlockSpec((q, di), lambda b, c: (row(b, c), 0)),
        out_shape=jax.ShapeDtypeStruct((m, di), MXU_DTYPE),
        scratch_shapes=[pltpu.VMEM((8, di), F32), pltpu.VMEM((8, SSD_BC), F32),
                        pltpu.VMEM((SSD_GROUPS, SSD_STATE, SSD_GROUP_DIM), F32),
                        pltpu.VMEM((q, di), F32)],
        compiler_params=_params("parallel", "arbitrary"),
        name="ssd_core",
    )(proj, proj, proj, x, wdt, cw[:, :di], cw[:, di:], cbias[:, :di], cbias[:, di:], dtb, a, dsk, nw, e2)


def _ssd_layer(x, batch, w_in_stack, layer, conv_w, conv_b, dt_bias, a_log, d_skip, norm_w, w_out, ln):
    n_main = SSD_D_INNER + SSD_CONV_DIM
    proj = _proj(x, w_in_stack.astype(MXU_DTYPE), layer, n_main, tm=min(1024, x.shape[0]), tn=1024,
                 out_dtype=F32)
    w_dt = w_in_stack[layer, :, n_main:]
    y = _ssd_core(proj, x, w_dt, batch, conv_w, conv_b, dt_bias, a_log, d_skip, norm_w)
    return _out_ln(y, w_out.astype(MXU_DTYPE), x, ln, tm=512)


def _ret_core_kernel(cd_ref, q_ref, k_ref, v_ref, g_ref, cos_ref, sin_ref, dmat_ref, xi_ref, zeta_ref,
                     o_ref, st_ref):
    @pl.when(pl.program_id(1) == 0)
    def _():
        st_ref[...] = jnp.zeros_like(st_ref)

    cos = cos_ref[...]
    sin = sin_ref[...]
    half = RET_DK // 2

    def rotate(t):
        t1 = t[:, :half]
        t2 = t[:, half:]
        return jnp.concatenate([t1 * cos - t2 * sin, t1 * sin + t2 * cos], axis=1)

    for h in range(RET_HEADS):
        qh = rotate(q_ref[:, h * RET_DK:(h + 1) * RET_DK].astype(F32))
        kh = rotate(k_ref[:, h * RET_DK:(h + 1) * RET_DK].astype(F32) * (RET_DK ** -0.5))
        vh = v_ref[:, h * RET_DV:(h + 1) * RET_DV].astype(MXU_DTYPE)
        qb = qh.astype(MXU_DTYPE)
        scores = _mm_nt(qb, kh.astype(MXU_DTYPE)) * dmat_ref[h]
        inner = _mm(scores.astype(MXU_DTYPE), vh)
        st = st_ref[h]
        xi = jnp.concatenate([xi_ref[h]] * (RET_DV // LANES), axis=1)
        cross = _mm(qb, st.astype(MXU_DTYPE)) * xi
        kz = kh * jnp.concatenate([zeta_ref[h]] * (RET_DK // LANES), axis=1)
        st_ref[h] = cd_ref[h] * st + _mm_tn(kz.astype(MXU_DTYPE), vh)
        out = inner + cross
        mu = jnp.mean(out, axis=-1, keepdims=True)
        d = out - mu
        var = jnp.mean(d * d, axis=-1, keepdims=True)
        out = d * lax.rsqrt(var + LN_EPS)
        gate = _silu(g_ref[:, h * RET_DV:(h + 1) * RET_DV].astype(F32))
        o_ref[:, h * RET_DV:(h + 1) * RET_DV] = (gate * out).astype(o_ref.dtype)


def _ret_core(proj, batch):
    m = proj.shape[0]
    seq = m // batch
    q = RET_CHUNK
    ncb = seq // q
    theta = (1.0 / (RET_ROPE_BASE ** np.linspace(0.0, 1.0, RET_DK // 2))).astype(np.float32)
    ang = (np.arange(seq, dtype=np.float32)[:, None] * theta[None, :]).astype(np.float64)
    cos, sin = np.cos(ang).astype(np.float32), np.sin(ang).astype(np.float32)
    lg = np.log1p(-np.exp2(-5.0 - np.arange(RET_HEADS)))
    pos = np.arange(q, dtype=np.float64)
    diff = pos[:, None] - pos[None, :]
    dmat = np.where(diff >= 0, np.exp(np.maximum(diff, 0.0)[None] * lg[:, None, None]), 0.0).astype(np.float32)
    xi = np.exp((pos[None, :] + 1.0) * lg[:, None])
    zeta = np.exp((q - 1.0 - pos[None, :]) * lg[:, None])
    xi_b = np.broadcast_to(xi[:, :, None], (RET_HEADS, q, LANES)).astype(np.float32)
    zeta_b = np.broadcast_to(zeta[:, :, None], (RET_HEADS, q, LANES)).astype(np.float32)
    chunk_decay = np.exp(q * lg).astype(np.float32)

    row = lambda b, c: b * ncb + c
    d = D_MODEL
    return pl.pallas_call(
        _ret_core_kernel,
        grid=(batch, ncb),
        in_specs=[pl.BlockSpec(memory_space=pltpu.SMEM),
                  pl.BlockSpec((q, d), lambda b, c: (row(b, c), 0)),
                  pl.BlockSpec((q, d), lambda b, c: (row(b, c), 1)),
                  pl.BlockSpec((q, 2 * d), lambda b, c: (row(b, c), 1)),
                  pl.BlockSpec((q, 2 * d), lambda b, c: (row(b, c), 2)),
                  pl.BlockSpec((q, RET_DK // 2), lambda b, c: (c, 0)),
                  pl.BlockSpec((q, RET_DK // 2), lambda b, c: (c, 0)),
                  _resident((RET_HEADS, q, q)),
                  _resident((RET_HEADS, q, LANES)),
                  _resident((RET_HEADS, q, LANES))],
        out_specs=pl.BlockSpec((q, 2 * d), lambda b, c: (row(b, c), 0)),
        out_shape=jax.ShapeDtypeStruct((m, 2 * d), MXU_DTYPE),
        scratch_shapes=[pltpu.VMEM((RET_HEADS, RET_DK, RET_DV), F32)],
        compiler_params=_params("parallel", "arbitrary"),
        name="ret_core",
    )(chunk_decay, proj, proj, proj, proj, cos, sin, dmat, xi_b, zeta_b)


def _ret_layer(x, batch, w_in_stack, layer, w_out, ln):
    proj = _proj(x, w_in_stack, layer, RET_IN_DIM, tm=min(1024, x.shape[0]), tn=1024, out_dtype=MXU_DTYPE)
    y = _ret_core(proj, batch)
    return _out_ln(y, w_out.astype(MXU_DTYPE), x, ln, tm=512)


def kernel(x, ln1_g, ln1_b, ln2_g, ln2_b, mlp_w1, mlp_w2, s5_w_in, s5_lam_re, s5_lam_im, s5_log_dt, s5_b_re, s5_b_im, s5_c_re, s5_c_im, s5_d, s5_w_out, s5_w_gate, ssd_w_in, ssd_conv_w, ssd_conv_b, ssd_dt_bias, ssd_a_log, ssd_d, ssd_norm_w, ssd_w_out, ret_w_in, ret_w_out):
    batch, seq, d = x.shape
    h = x.reshape(batch * seq, d)
    ln1 = (ln1_g[:, None, :], ln1_b[:, None, :])
    ln2 = (ln2_g[:, None, :], ln2_b[:, None, :])
    s5_tables = jax.vmap(_s5_tables)(s5_lam_re, s5_lam_im, s5_log_dt, s5_b_re, s5_b_im, s5_c_re, s5_c_im, s5_d)
    s5_w_in_b = s5_w_in.astype(MXU_DTYPE)
    s5_w_out_b = s5_w_out.astype(MXU_DTYPE)
    s5_w_gate_b = s5_w_gate.astype(MXU_DTYPE)
    for i in range(DEPTH):
        kind = i % 3
        j = i // 3
        if kind == 0:
            h = _s5_layer(h, batch, s5_w_in_b, s5_tables, j, s5_w_out_b, s5_w_gate_b, ln1 + (i,))
        elif kind == 1:
            h = _ssd_layer(h, batch, ssd_w_in, j, ssd_conv_w[j], ssd_conv_b[j], ssd_dt_bias[j], ssd_a_log[j],
                           ssd_d[j], ssd_norm_w[j], ssd_w_out[j], ln1 + (i,))
        else:
            h = _ret_layer(h, batch, ret_w_in, j, ret_w_out[j], ln1 + (i,))
        h = _mlp(h, mlp_w1, mlp_w2, i, ln2 + (i,), tm=min(1024, h.shape[0]), tf=512)
    return h.reshape(batch, seq, d)
```

```python
import functools
import math

import jax
import jax.numpy as jnp
import numpy as np
from jax import lax
from jax.experimental import pallas as pl
from jax.experimental.pallas import tpu as pltpu

F32 = jnp.float32
MXU_DTYPE = jnp.bfloat16
HI = lax.Precision.HIGHEST
LOG2_E = 1.0 / math.log(2.0)

LANES = 128
VMEM_LIMIT_BYTES = 58 * 1024 * 1024

D_MODEL = 2048
DEPTH = 4
DN_ALPHA = (2 * DEPTH) ** 0.25
LN_EPS = 1e-5
D_FF = 4 * D_MODEL
MLP_OUT_SLABS = 4
ROW_SPLIT = 2

S5_GROUP = 16
S5_GROUPS = D_MODEL // S5_GROUP
S5_STATE = 64
S5_CHUNK = 16
S5_SETS = D_MODEL // LANES
S5_SET_GROUPS = LANES // S5_GROUP
S5_SET_STATE = S5_SET_GROUPS * S5_STATE
S5_SCAN_UNROLL = 8
S5_TOEPLITZ_GROUP = 4

SSD_D_INNER = 2 * D_MODEL
SSD_HEADDIM = 64
SSD_HEADS = SSD_D_INNER // SSD_HEADDIM
SSD_GROUPS = 8
SSD_STATE = 128
SSD_CONV = 4
SSD_CHUNK = 128
SSD_BC = 2 * SSD_GROUPS * SSD_STATE
SSD_CONV_DIM = SSD_D_INNER + SSD_BC
SSD_GROUP_DIM = SSD_D_INNER // SSD_GROUPS
SSD_HEADS_PER_GROUP = SSD_HEADS // SSD_GROUPS

RET_HEADS = 8
RET_DK = D_MODEL // RET_HEADS
RET_DV = 2 * D_MODEL // RET_HEADS
RET_CHUNK = 128
RET_IN_DIM = 6 * D_MODEL
RET_ROPE_BASE = 10000.0


def _params(*semantics):
    return pltpu.CompilerParams(dimension_semantics=semantics, vmem_limit_bytes=VMEM_LIMIT_BYTES)


def _layer_norm(v, g, b):
    mu = jnp.mean(v, axis=-1, keepdims=True)
    d = v - mu
    var = jnp.mean(d * d, axis=-1, keepdims=True)
    return d * lax.rsqrt(var + LN_EPS) * g + b


def _silu(v):
    h = 0.5 * v
    return h + h * jnp.tanh(h)


def _mm(a, b):
    return jnp.dot(a, b, preferred_element_type=F32)


def _mm_nt(a, b, precision=None):
    return lax.dot_general(a, b, (((1,), (1,)), ((), ())), preferred_element_type=F32, precision=precision)


def _mm_tn(a, b):
    return lax.dot_general(a, b, (((0,), (0,)), ((), ())), preferred_element_type=F32)


def _out_ln_kernel(y_ref, w_ref, x_ref, g_ref, b_ref, o_ref):
    half = o_ref.shape[0] // ROW_SPLIT
    for r in range(ROW_SPLIT):
        rows = slice(r * half, (r + 1) * half)
        f = _mm(y_ref[rows, :], w_ref[...])
        o_ref[rows, :] = _layer_norm(DN_ALPHA * x_ref[rows, :] + f, g_ref[...], b_ref[...])


def _resident(shape):
    return pl.BlockSpec(shape, lambda *_: (0,) * len(shape), pipeline_mode=pl.Buffered(1))


def _layer_row(shape, layer):
    return pl.BlockSpec((None,) + shape, lambda *_: (layer, 0, 0), pipeline_mode=pl.Buffered(1))


def _out_ln(y, w, x, ln, tm):
    m, k = y.shape
    d = w.shape[1]
    g, b, layer = ln
    return pl.pallas_call(
        _out_ln_kernel,
        grid=(m // tm,),
        in_specs=[pl.BlockSpec((tm, k), lambda i: (i, 0)),
                  _resident((k, d)),
                  pl.BlockSpec((tm, d), lambda i: (i, 0)),
                  _layer_row((1, d), layer),
                  _layer_row((1, d), layer)],
        out_specs=pl.BlockSpec((tm, d), lambda i: (i, 0)),
        out_shape=jax.ShapeDtypeStruct((m, d), F32),
        compiler_params=_params("parallel"),
        name="out_ln",
    )(y, w, x, g, b)


def _mlp_kernel(x_ref, w1_ref, w2_ref, g_ref, b_ref, o_ref, xb_ref, *, nf):
    j = pl.program_id(1)

    @pl.when(j == 0)
    def _():
        xb_ref[...] = x_ref[...].astype(MXU_DTYPE)
        o_ref[...] = jnp.zeros_like(o_ref)

    h = _mm(xb_ref[...], w1_ref[...].astype(MXU_DTYPE))
    h = jnp.square(jnp.maximum(h, 0.0)).astype(MXU_DTYPE)
    slab = o_ref.shape[1] // MLP_OUT_SLABS
    for s in range(MLP_OUT_SLABS):
        cols = slice(s * slab, (s + 1) * slab)
        o_ref[:, cols] += _mm(h, w2_ref[:, cols].astype(MXU_DTYPE))

    @pl.when(j == nf - 1)
    def _():
        o_ref[...] = _layer_norm(DN_ALPHA * x_ref[...] + o_ref[...], g_ref[...], b_ref[...])


def _mlp(x, w1_stack, w2_stack, layer, ln, tm, tf):
    m, d = x.shape
    f = w1_stack.shape[2]
    nf = f // tf
    g, b, ln_layer = ln
    return pl.pallas_call(
        functools.partial(_mlp_kernel, nf=nf),
        grid=(m // tm, nf),
        in_specs=[pl.BlockSpec((tm, d), lambda i, j: (i, 0), pipeline_mode=pl.Buffered(1)),
                  pl.BlockSpec((None, d, tf), lambda i, j: (layer, 0, j)),
                  pl.BlockSpec((None, tf, d), lambda i, j: (layer, j, 0)),
                  _layer_row((1, d), ln_layer),
                  _layer_row((1, d), ln_layer)],
        out_specs=pl.BlockSpec((tm, d), lambda i, j: (i, 0)),
        out_shape=jax.ShapeDtypeStruct((m, d), F32),
        scratch_shapes=[pltpu.VMEM((tm, d), MXU_DTYPE)],
        compiler_params=_params("parallel", "arbitrary"),
        name="mlp",
    )(x, w1_stack, w2_stack, g, b)


def _s5_in_kernel(x_ref, w_ref, o_ref):
    u = _mm(x_ref[...].astype(MXU_DTYPE), w_ref[...])
    for gs in range(S5_SETS):
        o_ref[gs] = u[:, gs * LANES:(gs + 1) * LANES]


def _s5_in(x, w, layer, tm):
    m = x.shape[0]
    return pl.pallas_call(
        _s5_in_kernel,
        grid=(m // tm,),
        in_specs=[pl.BlockSpec((tm, D_MODEL), lambda i: (i, 0)),
                  _layer_row((D_MODEL, D_MODEL), layer)],
        out_specs=pl.BlockSpec((S5_SETS, tm, LANES), lambda i: (0, i, 0)),
        out_shape=jax.ShapeDtypeStruct((S5_SETS, m, LANES), F32),
        compiler_params=_params("parallel"),
        name="s5_in",
    )(x, w)


def _s5_core_kernel(u_ref, lb_ref, rc_ref, cc_ref, a16_ref, d_ref, o_ref,
                    wt_ref, win_ref, wout_ref, v_ref, sp_ref, yin_ref, *, batch, chunks_per_batch):
    nst = S5_SET_STATE
    kdim = S5_CHUNK * LANES
    lb = lb_ref[0]

    rc = rc_ref[0]
    lb_hi = lb.astype(MXU_DTYPE)
    lb_lo = (lb - lb_hi.astype(F32)).astype(MXU_DTYPE)
    rc_hi = rc.astype(MXU_DTYPE)
    rc_lo = (rc - rc_hi.astype(F32)).astype(MXU_DTYPE)
    drev = _mm(jnp.concatenate([lb_hi, lb_lo], axis=1), jnp.concatenate([rc_hi, rc_hi], axis=0))
    drev = drev + _mm(lb_hi, rc_lo)
    row = lax.broadcasted_iota(jnp.int32, (kdim, LANES), 0)
    lane = lax.broadcasted_iota(jnp.int32, (kdim, LANES), 1)
    row_group = (row % LANES) // S5_GROUP
    drev = jnp.where(row_group == lane // S5_GROUP, drev, 0.0).astype(MXU_DTYPE)
    for t in range(S5_CHUNK):
        used = (t + 1) * LANES
        group_rows = (t // S5_TOEPLITZ_GROUP + 1) * S5_TOEPLITZ_GROUP * LANES
        wt_ref[0:used, t * LANES:(t + 1) * LANES] = drev[kdim - used:, :]
        if used < group_rows:
            wt_ref[used:group_rows, t * LANES:(t + 1) * LANES] = jnp.zeros((group_rows - used, LANES), MXU_DTYPE)

    swapped = pltpu.roll(lb, S5_STATE, axis=1)
    low = lane < S5_STATE
    re_dup = jnp.where(low, lb, swapped)
    im_dup = jnp.where(low, swapped, lb)
    for v in range(nst // LANES):
        sel = row_group == 2 * v + (lane >= S5_STATE).astype(jnp.int32)
        win_ref[:, v * LANES:(v + 1) * LANES] = jnp.where(sel, re_dup, 0.0).astype(MXU_DTYPE)
        win_ref[:, nst + v * LANES:nst + (v + 1) * LANES] = jnp.where(sel, im_dup, 0.0).astype(MXU_DTYPE)

    cc = cc_ref[0]
    lane_group = (lax.broadcasted_iota(jnp.int32, (S5_STATE, kdim), 1) % LANES) // S5_GROUP
    for ri in range(2):
        part = cc[ri * S5_STATE:(ri + 1) * S5_STATE, :]
        for g in range(S5_SET_GROUPS):
            r0 = ri * nst + g * S5_STATE
            wout_ref[r0:r0 + S5_STATE, :] = jnp.where(lane_group == g, part, 0.0).astype(MXU_DTYPE)

    nchunks = batch * chunks_per_batch
    u = jnp.concatenate([u_ref[pl.ds(t, nchunks, stride=S5_CHUNK), :] for t in range(S5_CHUNK)], axis=1)
    xb = u.astype(MXU_DTYPE)
    vin = _mm(xb, win_ref[...])
    ntile = nst // LANES
    for k in range(2 * ntile):
        v_ref[k] = vin[:, k * LANES:(k + 1) * LANES]
    gcols = S5_TOEPLITZ_GROUP * LANES
    for j in range(S5_CHUNK // S5_TOEPLITZ_GROUP):
        cols = slice(j * gcols, (j + 1) * gcols)
        krows = (j + 1) * gcols
        yin_ref[:, cols] = _mm(xb[:, :krows], wt_ref[0:krows, cols]) + d_ref[0][:, cols] * u[:, cols]

    a_re = [a16_ref[0, :, k * LANES:(k + 1) * LANES] for k in range(ntile)]
    a_im = [a16_ref[0, :, nst + k * LANES:nst + (k + 1) * LANES] for k in range(ntile)]

    def step(c, carry):
        rows = pl.ds(c, batch, stride=chunks_per_batch)
        new = []
        for k in range(ntile):
            s_re, s_im = carry[k], carry[ntile + k]
            sp_ref[k, rows, :] = s_re
            sp_ref[ntile + k, rows, :] = s_im
            new.append((a_re[k] * s_re - a_im[k] * s_im + v_ref[k, rows, :],
                        a_re[k] * s_im + a_im[k] * s_re + v_ref[ntile + k, rows, :]))
        return tuple(n[0] for n in new) + tuple(n[1] for n in new)

    zero = jnp.zeros((batch, LANES), F32)
    lax.fori_loop(0, chunks_per_batch, step, (zero,) * (2 * ntile), unroll=S5_SCAN_UNROLL)

    sp = jnp.concatenate([sp_ref[k] for k in range(2 * ntile)], axis=1)
    h = jax.nn.gelu(yin_ref[...] + _mm(sp.astype(MXU_DTYPE), wout_ref[...]))
    for t in range(S5_CHUNK):
        o_ref[pl.ds(t, nchunks, stride=S5_CHUNK), :] = h[:, t * LANES:(t + 1) * LANES]


def _s5_core(ug, tables, layer, batch):
    nsets, m, _ = ug.shape
    nchunks = m // S5_CHUNK
    kdim = S5_CHUNK * LANES
    kern = functools.partial(_s5_core_kernel, batch=batch, chunks_per_batch=nchunks // batch)
    table = lambda rows, cols: pl.BlockSpec((None, 1, rows, cols), lambda s: (layer, s, 0, 0))
    return pl.pallas_call(
        kern,
        grid=(nsets,),
        in_specs=[pl.BlockSpec((None, m, LANES), lambda s: (s, 0, 0)),
                  table(kdim, LANES),
                  table(LANES, LANES),
                  table(LANES, kdim),
                  table(1, 2 * S5_SET_STATE),
                  table(1, kdim)],
        out_specs=pl.BlockSpec((None, m, LANES), lambda s: (s, 0, 0)),
        out_shape=jax.ShapeDtypeStruct((nsets, m, LANES), F32),
        scratch_shapes=[pltpu.VMEM((kdim, kdim), MXU_DTYPE),
                        pltpu.VMEM((kdim, 2 * S5_SET_STATE), MXU_DTYPE),
                        pltpu.VMEM((2 * S5_SET_STATE, kdim), MXU_DTYPE),
                        pltpu.VMEM((2 * S5_SET_STATE // LANES, nchunks, LANES), F32),
                        pltpu.VMEM((2 * S5_SET_STATE // LANES, nchunks, LANES), F32),
                        pltpu.VMEM((nchunks, kdim), F32)],
        compiler_params=_params("parallel"),
        name="s5_core",
    )(ug, *tables)


def _s5_out_kernel(h_ref, wo_ref, wg_ref, x_ref, g_ref, b_ref, o_ref):
    half = o_ref.shape[0] // ROW_SPLIT
    for r in range(ROW_SPLIT):
        rows = slice(r * half, (r + 1) * half)
        h = jnp.concatenate([h_ref[gs, rows, :] for gs in range(S5_SETS)], axis=1).astype(MXU_DTYPE)
        f = _mm(h, wo_ref[...]) * jax.nn.sigmoid(_mm(h, wg_ref[...]))
        o_ref[rows, :] = _layer_norm(DN_ALPHA * x_ref[rows, :] + f, g_ref[...], b_ref[...])


def _s5_out(hg, wo, wg, w_layer, x, ln, tm):
    m = x.shape[0]
    g, b, layer = ln
    return pl.pallas_call(
        _s5_out_kernel,
        grid=(m // tm,),
        in_specs=[pl.BlockSpec((S5_SETS, tm, LANES), lambda i: (0, i, 0)),
                  _layer_row((D_MODEL, D_MODEL), w_layer),
                  _layer_row((D_MODEL, D_MODEL), w_layer),
                  pl.BlockSpec((tm, D_MODEL), lambda i: (i, 0)),
                  _layer_row((1, D_MODEL), layer),
                  _layer_row((1, D_MODEL), layer)],
        out_specs=pl.BlockSpec((tm, D_MODEL), lambda i: (i, 0)),
        out_shape=jax.ShapeDtypeStruct(x.shape, F32),
        compiler_params=_params("parallel"),
        name="s5_out",
    )(hg, wo, wg, x, g, b)


def _s5_tables(lam_re, lam_im, log_dt, b_re, b_im, c_re, c_im, d_skip):
    lr = lam_re.astype(F32)
    li = lam_im.astype(F32)
    dt = jnp.exp(log_dt.astype(F32))[:, None]
    mag = jnp.exp(lr * dt)
    ar = mag * jnp.cos(li * dt)
    ai = mag * jnp.sin(li * dt)
    den = lr * lr + li * li
    zr = ((ar - 1.0) * lr + ai * li) / den
    zi = (ai * lr - (ar - 1.0) * li) / den
    br_ = b_re.astype(F32)
    bi_ = b_im.astype(F32)
    bbr = zr[..., None] * br_ - zi[..., None] * bi_
    bbi = zr[..., None] * bi_ + zi[..., None] * br_
    k = jnp.arange(S5_CHUNK + 1, dtype=F32)[:, None, None]
    pmag = jnp.exp(k * (lr * dt))
    pr = pmag * jnp.cos(k * (li * dt))
    pi = pmag * jnp.sin(k * (li * dt))

    bt_re = jnp.swapaxes(bbr, 1, 2)
    bt_im = jnp.swapaxes(bbi, 1, 2)
    shape_b = (S5_SETS, 1, S5_SET_GROUPS, S5_GROUP, 2 * S5_STATE)
    b_with_re = jnp.concatenate([bt_re, bt_im], axis=-1).reshape(shape_b)
    b_with_im = jnp.concatenate([-bt_im, bt_re], axis=-1).reshape(shape_b)
    def power_rows(t):
        t = jnp.concatenate([t, t], axis=-1).reshape(S5_CHUNK, S5_SETS, S5_SET_GROUPS, 1, 2 * S5_STATE)
        return t.transpose(1, 0, 2, 3, 4)
    lb = power_rows(pr[S5_CHUNK - 1::-1]) * b_with_re + power_rows(pi[S5_CHUNK - 1::-1]) * b_with_im
    lb = lb.reshape(S5_SETS, S5_CHUNK * LANES, 2 * S5_STATE)

    def rows_p(t):
        return t.astype(F32).reshape(S5_SETS, S5_SET_GROUPS, S5_GROUP, S5_STATE).transpose(0, 3, 1, 2)
    ct_re = rows_p(c_re)
    ct_im = rows_p(c_im)
    rc = jnp.concatenate([ct_re, -ct_im], axis=1).reshape(S5_SETS, 2 * S5_STATE, LANES)

    def power_lanes(t):
        return t.reshape(S5_CHUNK, S5_SETS, S5_SET_GROUPS, S5_STATE).transpose(1, 3, 0, 2)[..., None]
    prs = power_lanes(pr[1:])
    pis = power_lanes(pi[1:])
    ca_re = ct_re[:, :, None] * prs - ct_im[:, :, None] * pis
    ca_im = ct_re[:, :, None] * pis + ct_im[:, :, None] * prs
    cc = jnp.concatenate([ca_re, -ca_im], axis=1).reshape(S5_SETS, 2 * S5_STATE, S5_CHUNK * LANES)

    a16 = jnp.concatenate([pr[S5_CHUNK].reshape(S5_SETS, 1, S5_SET_STATE),
                           pi[S5_CHUNK].reshape(S5_SETS, 1, S5_SET_STATE)], axis=2)
    drow = jnp.tile(d_skip.astype(F32).reshape(S5_SETS, 1, LANES), (1, 1, S5_CHUNK))
    return lb, rc, cc, a16, drow


def _s5_layer(x, batch, w_in, tables, layer, w_out, w_gate, ln):
    ug = _s5_in(x, w_in, layer, tm=512)
    hg = _s5_core(ug, tables, layer, batch)
    return _s5_out(hg, w_out, w_gate, layer, x, ln, tm=512)


def _split_hi_lo(v):
    hi = v.astype(MXU_DTYPE)
    lo = (v - hi.astype(F32)).astype(MXU_DTYPE)
    return jnp.concatenate([hi, lo], axis=1)


HALO = 8
SSD_PROJ_ROW_BLOCKS = 8


def _ssd_proj_kernel(x_ref, halo_ref, w_ref, cw_ref, cb_ref, o_ref, xb_ref, *, gate_tiles, seq):
    i = pl.program_id(0)
    j = pl.program_id(1)
    tm = o_ref.shape[0]

    @pl.when(j == 0)
    def _():
        xb_ref[...] = x_ref[...].astype(MXU_DTYPE)

    @pl.when(j < gate_tiles)
    def _():
        o_ref[...] = _silu(_mm(xb_ref[...], w_ref[...]))

    @pl.when(j >= gate_tiles)
    def _():
        w = w_ref[...]
        tail = _mm(halo_ref[...].astype(MXU_DTYPE), w)
        tail = jnp.where((i * tm) % seq == 0, 0.0, tail)
        row8 = lax.broadcasted_iota(jnp.int32, tail.shape, 0)
        sub = tm // SSD_PROJ_ROW_BLOCKS
        for r in range(SSD_PROJ_ROW_BLOCKS):
            cur = _mm(xb_ref[r * sub:(r + 1) * sub, :], w)
            acc = cb_ref[...] + cw_ref[SSD_CONV - 1:SSD_CONV, :] * cur
            for k in range(1, SSD_CONV):
                back = pltpu.roll(cur, k, axis=0)
                head = jnp.where(row8 < k, pltpu.roll(tail, k, axis=0), back[0:HALO])
                back = jnp.concatenate([head, back[HALO:]], axis=0)
                acc = acc + cw_ref[SSD_CONV - 1 - k:SSD_CONV - k, :] * back
            o_ref[r * sub:(r + 1) * sub, :] = _silu(acc)
            tail = cur[sub - HALO:, :]


def _ssd_proj(x, w_stack, layer, conv_w, conv_b, seq, tm, tn):
    m, k = x.shape
    assert seq % tm == 0, "a row tile must not straddle two sequences"
    n = SSD_D_INNER + SSD_CONV_DIM
    gate_tiles = SSD_D_INNER // tn
    conv_tile = lambda i, j: (0, jnp.maximum(j - gate_tiles, 0))
    return pl.pallas_call(
        functools.partial(_ssd_proj_kernel, gate_tiles=gate_tiles, seq=seq),
        grid=(m // tm, n // tn),
        in_specs=[pl.BlockSpec((tm, k), lambda i, j: (i, 0)),
                  pl.BlockSpec((HALO, k), lambda i, j: (jnp.maximum(i * (tm // HALO) - 1, 0), 0)),
                  pl.BlockSpec((None, k, tn), lambda i, j: (layer, 0, j)),
                  pl.BlockSpec((SSD_CONV, tn), conv_tile),
                  pl.BlockSpec((1, tn), conv_tile)],
        out_specs=pl.BlockSpec((tm, tn), lambda i, j: (i, j)),
        out_shape=jax.ShapeDtypeStruct((m, n), F32),
        scratch_shapes=[pltpu.VMEM((tm, k), MXU_DTYPE)],
        compiler_params=_params("parallel", "arbitrary"),
        name="ssd_proj",
    )(x, x, w_stack, conv_w, conv_b)


def _ssd_core_kernel(z_ref, xs_ref, bc_ref, xin_ref, wdt_ref,
                     dtb_ref, a_ref, dsk_ref, nw_ref, e2_ref, o_ref,
                     st_ref, yd_ref):
    q = SSD_CHUNK

    @pl.when(pl.program_id(1) == 0)
    def _():
        st_ref[...] = jnp.zeros_like(st_ref)

    xs = xs_ref[...]
    bc = bc_ref[...]

    dtr = _mm(xin_ref[...].astype(MXU_DTYPE), wdt_ref[...]) + dtb_ref[...]
    dt = jnp.maximum(dtr, 0.0) + jnp.log1p(jnp.exp(-jnp.abs(dtr)))
    adt = dt * a_ref[...]
    ri = lax.broadcasted_iota(jnp.int32, (q, q), 0)
    ci = lax.broadcasted_iota(jnp.int32, (q, q), 1)
    causal = ri >= ci
    tri = causal.astype(F32)
    cs = jnp.dot(tri, adt, preferred_element_type=F32, precision=HI)
    eye = (ri == ci).astype(F32)
    cs2 = cs * LOG2_E
    r_rows = _mm_nt(eye, cs2, precision=HI) - jnp.log2(_mm_nt(eye, dt, precision=HI))
    tot = cs[q - 1:q, :]
    w_col = dt * jnp.exp(tot - cs)
    din = jnp.exp(cs)
    w_x = _mm(_split_hi_lo(w_col), e2_ref[...])
    din_x = _mm(_split_hi_lo(din), e2_ref[...])

    lane = lax.broadcasted_iota(jnp.int32, (q, LANES), 1)
    low = lane < SSD_HEADDIM

    xw = (xs * w_x).astype(MXU_DTYPE)
    xsb = xs.astype(MXU_DTYPE)
    for g in range(SSD_GROUPS):
        bg = bc[:, g * SSD_STATE:(g + 1) * SSD_STATE].astype(MXU_DTYPE)
        cg = bc[:, SSD_GROUPS * SSD_STATE + g * SSD_STATE:
                SSD_GROUPS * SSD_STATE + (g + 1) * SSD_STATE].astype(MXU_DTYPE)
        cb = _mm_nt(cg, bg)
        for pair in range(SSD_HEADS_PER_GROUP // 2):
            h0 = g * SSD_HEADS_PER_GROUP + 2 * pair
            ms = []
            for h in (h0, h0 + 1):
                seg = cs2[:, h:h + 1] - r_rows[h:h + 1, :]
                ms.append((cb * jnp.exp2(jnp.where(causal, seg, -jnp.inf))).astype(MXU_DTYPE))
            lhs = jnp.concatenate(ms, axis=1)
            xp = xsb[:, h0 * SSD_HEADDIM:(h0 + 2) * SSD_HEADDIM]
            zero = jnp.zeros_like(xp)
            rhs = jnp.concatenate([jnp.where(low, xp, zero), jnp.where(low, zero, xp)], axis=0)
            yd_ref[:, h0 * SSD_HEADDIM:(h0 + 2) * SSD_HEADDIM] = _mm(lhs, rhs)
        gsl = slice(g * SSD_GROUP_DIM, (g + 1) * SSD_GROUP_DIM)
        st = st_ref[g]
        y_off = _mm(cg, st.astype(MXU_DTYPE)) * din_x[:, gsl]
        yd_ref[:, gsl] = yd_ref[:, gsl] + y_off
        st_ref[g] = din_x[q - 1:q, gsl] * st + _mm_tn(bg, xw[:, gsl])

    y = yd_ref[...] + dsk_ref[...] * xs
    y = y * z_ref[...]
    for g in range(SSD_GROUPS):
        gsl = slice(g * SSD_GROUP_DIM, (g + 1) * SSD_GROUP_DIM)
        yg = y[:, gsl]
        yg = yg * lax.rsqrt(jnp.mean(yg * yg, axis=-1, keepdims=True) + LN_EPS)
        o_ref[:, gsl] = (yg * nw_ref[:, gsl]).astype(o_ref.dtype)


def _ssd_core(proj, x, w_dt, batch, dt_bias, a_log, d_skip, norm_w):
    m = proj.shape[0]
    q = SSD_CHUNK
    ncb = m // batch // q
    di = SSD_D_INNER
    wdt = jnp.pad(w_dt.astype(MXU_DTYPE), ((0, 0), (0, LANES - SSD_HEADS)))
    padh = LANES - SSD_HEADS
    dtb = jnp.pad(dt_bias.astype(F32), (0, padh))[None, :]
    a = jnp.pad(-jnp.exp(a_log.astype(F32)), (0, padh))[None, :]
    dsk = jnp.repeat(d_skip.astype(F32), SSD_HEADDIM)[None, :]
    nw = norm_w.astype(F32)[None, :]
    expand = (np.arange(LANES)[:, None] == (np.arange(di) // SSD_HEADDIM)[None, :]).astype(np.float32)
    e2 = jnp.asarray(np.concatenate([expand, expand], axis=0), dtype=MXU_DTYPE)

    row = lambda b, c: b * ncb + c
    full = _resident
    return pl.pallas_call(
        _ssd_core_kernel,
        grid=(batch, ncb),
        in_specs=[pl.BlockSpec((q, di), lambda b, c: (row(b, c), 0)),
                  pl.BlockSpec((q, di), lambda b, c: (row(b, c), 1)),
                  pl.BlockSpec((q, SSD_BC), lambda b, c: (row(b, c), 2 * di // SSD_BC)),
                  pl.BlockSpec((q, D_MODEL), lambda b, c: (row(b, c), 0)),
                  full((D_MODEL, LANES)),
                  full((1, LANES)), full((1, LANES)), full((1, di)), full((1, di)),
                  full((2 * LANES, di))],
        out_specs=pl.BlockSpec((q, di), lambda b, c: (row(b, c), 0)),
        out_shape=jax.ShapeDtypeStruct((m, di), MXU_DTYPE),
        scratch_shapes=[pltpu.VMEM((SSD_GROUPS, SSD_STATE, SSD_GROUP_DIM), F32),
                        pltpu.VMEM((q, di), F32)],
        compiler_params=_params("parallel", "arbitrary"),
        name="ssd_core",
    )(proj, proj, proj, x, wdt, dtb, a, dsk, nw, e2)


def _ssd_layer(x, batch, w_in_stack, layer, conv_w, conv_b, dt_bias, a_log, d_skip, norm_w, w_out, ln):
    n_main = SSD_D_INNER + SSD_CONV_DIM
    proj = _ssd_proj(x, w_in_stack.astype(MXU_DTYPE), layer, conv_w.astype(F32), conv_b.astype(F32)[None, :],
                     seq=x.shape[0] // batch, tm=min(1024, x.shape[0]), tn=1024)
    w_dt = w_in_stack[layer, :, n_main:]
    y = _ssd_core(proj, x, w_dt, batch, dt_bias, a_log, d_skip, norm_w)
    return _out_ln(y, w_out.astype(MXU_DTYPE), x, ln, tm=512)


def _ret_proj_kernel(x_ref, w_ref, cos_ref, sin_ref, o_ref, xb_ref, *, tiles_qk, tiles_v):
    j = pl.program_id(1)

    @pl.when(j == 0)
    def _():
        xb_ref[...] = x_ref[...].astype(MXU_DTYPE)

    def product():
        return _mm(xb_ref[...], w_ref[...].astype(MXU_DTYPE))

    def rotary(t):
        cos = cos_ref[...]
        sin = sin_ref[...]
        half = RET_DK // 2
        for hh in range(t.shape[1] // RET_DK):
            t1 = t[:, hh * RET_DK:hh * RET_DK + half]
            t2 = t[:, hh * RET_DK + half:(hh + 1) * RET_DK]
            o_ref[:, hh * RET_DK:hh * RET_DK + half] = (t1 * cos - t2 * sin).astype(o_ref.dtype)
            o_ref[:, hh * RET_DK + half:(hh + 1) * RET_DK] = (t1 * sin + t2 * cos).astype(o_ref.dtype)

    @pl.when(j < tiles_qk)
    def _():
        rotary(product())

    @pl.when((j >= tiles_qk) & (j < 2 * tiles_qk))
    def _():
        rotary(product() * (RET_DK ** -0.5))

    @pl.when((j >= 2 * tiles_qk) & (j < 2 * tiles_qk + tiles_v))
    def _():
        o_ref[...] = product().astype(o_ref.dtype)

    @pl.when(j >= 2 * tiles_qk + tiles_v)
    def _():
        o_ref[...] = _silu(product()).astype(o_ref.dtype)


def _rotary_tables(seq):
    theta = (1.0 / (RET_ROPE_BASE ** np.linspace(0.0, 1.0, RET_DK // 2))).astype(np.float32)
    ang = (np.arange(seq, dtype=np.float32)[:, None] * theta[None, :]).astype(np.float64)
    return np.cos(ang).astype(np.float32), np.sin(ang).astype(np.float32)


def _ret_proj(x, w_stack, layer, seq, tm, tn):
    m, k = x.shape
    assert seq % tm == 0 and tn % RET_DK == 0
    cos, sin = _rotary_tables(seq)
    pos_tile = lambda i, j: (i % (seq // tm), 0)
    kern = functools.partial(_ret_proj_kernel, tiles_qk=D_MODEL // tn, tiles_v=2 * D_MODEL // tn)
    return pl.pallas_call(
        kern,
        grid=(m // tm, RET_IN_DIM // tn),
        in_specs=[pl.BlockSpec((tm, k), lambda i, j: (i, 0)),
                  pl.BlockSpec((None, k, tn), lambda i, j: (layer, 0, j)),
                  pl.BlockSpec((tm, RET_DK // 2), pos_tile),
                  pl.BlockSpec((tm, RET_DK // 2), pos_tile)],
        out_specs=pl.BlockSpec((tm, tn), lambda i, j: (i, j)),
        out_shape=jax.ShapeDtypeStruct((m, RET_IN_DIM), MXU_DTYPE),
        scratch_shapes=[pltpu.VMEM((tm, k), MXU_DTYPE)],
        compiler_params=_params("parallel", "arbitrary"),
        name="ret_proj",
    )(x, w_stack, cos, sin)


def _ret_core_kernel(cd_ref, q_ref, k_ref, v_ref, g_ref, dmat_ref, xi_ref, zeta_ref, o_ref, st_ref):
    @pl.when(pl.program_id(1) == 0)
    def _():
        st_ref[...] = jnp.zeros_like(st_ref)

    for h in range(RET_HEADS):
        qb = q_ref[:, h * RET_DK:(h + 1) * RET_DK]
        kb = k_ref[:, h * RET_DK:(h + 1) * RET_DK]
        vh = v_ref[:, h * RET_DV:(h + 1) * RET_DV]
        scores = _mm_nt(qb, kb) * dmat_ref[h]
        inner = _mm(scores.astype(MXU_DTYPE), vh)
        st = st_ref[h]
        xi = jnp.concatenate([xi_ref[h]] * (RET_DV // LANES), axis=1)
        cross = _mm(qb, st.astype(MXU_DTYPE)) * xi
        kz = kb.astype(F32) * jnp.concatenate([zeta_ref[h]] * (RET_DK // LANES), axis=1)
        st_ref[h] = cd_ref[h] * st + _mm_tn(kz.astype(MXU_DTYPE), vh)
        out = inner + cross
        mu = jnp.mean(out, axis=-1, keepdims=True)
        d = out - mu
        var = jnp.mean(d * d, axis=-1, keepdims=True)
        out = d * lax.rsqrt(var + LN_EPS)
        gate = g_ref[:, h * RET_DV:(h + 1) * RET_DV].astype(F32)
        o_ref[:, h * RET_DV:(h + 1) * RET_DV] = (gate * out).astype(o_ref.dtype)


def _ret_core(proj, batch):
    m = proj.shape[0]
    seq = m // batch
    q = RET_CHUNK
    ncb = seq // q
    lg = np.log1p(-np.exp2(-5.0 - np.arange(RET_HEADS)))
    pos = np.arange(q, dtype=np.float64)
    diff = pos[:, None] - pos[None, :]
    dmat = np.where(diff >= 0, np.exp(np.maximum(diff, 0.0)[None] * lg[:, None, None]), 0.0).astype(np.float32)
    xi = np.exp((pos[None, :] + 1.0) * lg[:, None])
    zeta = np.exp((q - 1.0 - pos[None, :]) * lg[:, None])
    xi_b = np.broadcast_to(xi[:, :, None], (RET_HEADS, q, LANES)).astype(np.float32)
    zeta_b = np.broadcast_to(zeta[:, :, None], (RET_HEADS, q, LANES)).astype(np.float32)
    chunk_decay = np.exp(q * lg).astype(np.float32)

    row = lambda b, c: b * ncb + c
    d = D_MODEL
    return pl.pallas_call(
        _ret_core_kernel,
        grid=(batch, ncb),
        in_specs=[pl.BlockSpec(memory_space=pltpu.SMEM),
                  pl.BlockSpec((q, d), lambda b, c: (row(b, c), 0)),
                  pl.BlockSpec((q, d), lambda b, c: (row(b, c), 1)),
                  pl.BlockSpec((q, 2 * d), lambda b, c: (row(b, c), 1)),
                  pl.BlockSpec((q, 2 * d), lambda b, c: (row(b, c), 2)),
                  _resident((RET_HEADS, q, q)),
                  _resident((RET_HEADS, q, LANES)),
                  _resident((RET_HEADS, q, LANES))],
        out_specs=pl.BlockSpec((q, 2 * d), lambda b, c: (row(b, c), 0)),
        out_shape=jax.ShapeDtypeStruct((m, 2 * d), MXU_DTYPE),
        scratch_shapes=[pltpu.VMEM((RET_HEADS, RET_DK, RET_DV), F32)],
        compiler_params=_params("parallel", "arbitrary"),
        name="ret_core",
    )(chunk_decay, proj, proj, proj, proj, dmat, xi_b, zeta_b)


def _ret_layer(x, batch, w_in_stack, layer, w_out, ln):
    proj = _ret_proj(x, w_in_stack, layer, seq=x.shape[0] // batch, tm=min(1024, x.shape[0]), tn=1024)
    y = _ret_core(proj, batch)
    return _out_ln(y, w_out.astype(MXU_DTYPE), x, ln, tm=512)


def kernel(x, ln1_g, ln1_b, ln2_g, ln2_b, mlp_w1, mlp_w2, s5_w_in, s5_lam_re, s5_lam_im, s5_log_dt, s5_b_re, s5_b_im, s5_c_re, s5_c_im, s5_d, s5_w_out, s5_w_gate, ssd_w_in, ssd_conv_w, ssd_conv_b, ssd_dt_bias, ssd_a_log, ssd_d, ssd_norm_w, ssd_w_out, ret_w_in, ret_w_out):
    batch, seq, d = x.shape
    h = x.reshape(batch * seq, d)
    ln1 = (ln1_g[:, None, :], ln1_b[:, None, :])
    ln2 = (ln2_g[:, None, :], ln2_b[:, None, :])
    s5_tables = jax.vmap(_s5_tables)(s5_lam_re, s5_lam_im, s5_log_dt, s5_b_re, s5_b_im, s5_c_re, s5_c_im, s5_d)
    s5_w_in_b = s5_w_in.astype(MXU_DTYPE)
    s5_w_out_b = s5_w_out.astype(MXU_DTYPE)
    s5_w_gate_b = s5_w_gate.astype(MXU_DTYPE)
    for i in range(DEPTH):
        kind = i % 3
        j = i // 3
        if kind == 0:
            h = _s5_layer(h, batch, s5_w_in_b, s5_tables, j, s5_w_out_b, s5_w_gate_b, ln1 + (i,))
        elif kind == 1:
            h = _ssd_layer(h, batch, ssd_w_in, j, ssd_conv_w[j], ssd_conv_b[j], ssd_dt_bias[j], ssd_a_log[j],
                           ssd_d[j], ssd_norm_w[j], ssd_w_out[j], ln1 + (i,))
        else:
            h = _ret_layer(h, batch, ret_w_in, j, ret_w_out[j], ln1 + (i,))
        h = _mlp(h, mlp_w1, mlp_w2, i, ln2 + (i,), tm=min(1024, h.shape[0]), tf=512)
    return h.reshape(batch, seq, d)
```

```python
import functools
import math

import jax
import jax.numpy as jnp
import numpy as np
from jax import lax
from jax.experimental import pallas as pl
from jax.experimental.pallas import tpu as pltpu

F32 = jnp.float32
MXU_DTYPE = jnp.bfloat16
HI = lax.Precision.HIGHEST
LOG2_E = 1.0 / math.log(2.0)

LANES = 128
VMEM_PHYSICAL_BYTES = 64 * 1024 * 1024
VMEM_LIMIT_BYTES = VMEM_PHYSICAL_BYTES - 6 * 1024 * 1024

D_MODEL = 2048
DEPTH = 4
DN_ALPHA = (2 * DEPTH) ** 0.25
LN_EPS = 1e-5
D_FF = 4 * D_MODEL
MLP_OUT_SLABS = 4
ROW_SPLIT = 2

S5_GROUP = 16
S5_GROUPS = D_MODEL // S5_GROUP
S5_STATE = 64
S5_CHUNK = 16
S5_SETS = D_MODEL // LANES
S5_SET_GROUPS = LANES // S5_GROUP
S5_SET_STATE = S5_SET_GROUPS * S5_STATE
S5_SCAN_UNROLL = 8
S5_TOEPLITZ_GROUP = 4

SSD_D_INNER = 2 * D_MODEL
SSD_HEADDIM = 64
SSD_HEADS = SSD_D_INNER // SSD_HEADDIM
SSD_GROUPS = 8
SSD_STATE = 128
SSD_CONV = 4
SSD_CHUNK = 128
SSD_BC = 2 * SSD_GROUPS * SSD_STATE
SSD_CONV_DIM = SSD_D_INNER + SSD_BC
SSD_GROUP_DIM = SSD_D_INNER // SSD_GROUPS
SSD_HEADS_PER_GROUP = SSD_HEADS // SSD_GROUPS

RET_HEADS = 8
RET_DK = D_MODEL // RET_HEADS
RET_DV = 2 * D_MODEL // RET_HEADS
RET_CHUNK = 128
RET_IN_DIM = 6 * D_MODEL
RET_ROPE_BASE = 10000.0


def _params(*semantics):
    return pltpu.CompilerParams(dimension_semantics=semantics, vmem_limit_bytes=VMEM_LIMIT_BYTES)


def _layer_norm(v, g, b):
    mu = jnp.mean(v, axis=-1, keepdims=True)
    d = v - mu
    var = jnp.mean(d * d, axis=-1, keepdims=True)
    return d * lax.rsqrt(var + LN_EPS) * g + b


def _silu(v):
    h = 0.5 * v
    return h + h * jnp.tanh(h)


def _mm(a, b):
    return jnp.dot(a, b, preferred_element_type=F32)


def _mm_nt(a, b, precision=None):
    return lax.dot_general(a, b, (((1,), (1,)), ((), ())), preferred_element_type=F32, precision=precision)


def _mm_tn(a, b):
    return lax.dot_general(a, b, (((0,), (0,)), ((), ())), preferred_element_type=F32)


def _resident(shape):
    return pl.BlockSpec(shape, lambda *_: (0,) * len(shape), pipeline_mode=pl.Buffered(1))


def _layer_row(shape, layer):
    return pl.BlockSpec((None,) + shape, lambda *_: (layer, 0, 0), pipeline_mode=pl.Buffered(1))


def _proj_kernel(x_ref, w_ref, o_ref, *, act_tiles):
    j = pl.program_id(1)

    def product():
        return _mm(x_ref[...].astype(MXU_DTYPE), w_ref[...].astype(MXU_DTYPE))

    if act_tiles == 0:
        o_ref[...] = product().astype(o_ref.dtype)
    else:
        @pl.when(j < act_tiles)
        def _():
            o_ref[...] = _silu(product()).astype(o_ref.dtype)

        @pl.when(j >= act_tiles)
        def _():
            o_ref[...] = product().astype(o_ref.dtype)


def _proj(x, w_stack, layer, n, tm, tn, out_dtype, silu_cols=0):
    m, k = x.shape
    assert silu_cols % tn == 0
    return pl.pallas_call(
        functools.partial(_proj_kernel, act_tiles=silu_cols // tn),
        grid=(m // tm, n // tn),
        in_specs=[pl.BlockSpec((tm, k), lambda i, j: (i, 0)),
                  pl.BlockSpec((None, k, tn), lambda i, j: (layer, 0, j))],
        out_specs=pl.BlockSpec((tm, tn), lambda i, j: (i, j)),
        out_shape=jax.ShapeDtypeStruct((m, n), out_dtype),
        compiler_params=_params("parallel", "arbitrary"),
        name="proj",
    )(x, w_stack)


def _out_ln_kernel(y_ref, w_ref, x_ref, g_ref, b_ref, o_ref):
    half = o_ref.shape[0] // ROW_SPLIT
    for r in range(ROW_SPLIT):
        rows = slice(r * half, (r + 1) * half)
        f = _mm(y_ref[rows, :], w_ref[...])
        o_ref[rows, :] = _layer_norm(DN_ALPHA * x_ref[rows, :] + f, g_ref[...], b_ref[...])


def _out_ln(y, w, x, ln, tm):
    m, k = y.shape
    d = w.shape[1]
    g, b, layer = ln
    return pl.pallas_call(
        _out_ln_kernel,
        grid=(m // tm,),
        in_specs=[pl.BlockSpec((tm, k), lambda i: (i, 0)),
                  _resident((k, d)),
                  pl.BlockSpec((tm, d), lambda i: (i, 0)),
                  _layer_row((1, d), layer),
                  _layer_row((1, d), layer)],
        out_specs=pl.BlockSpec((tm, d), lambda i: (i, 0)),
        out_shape=jax.ShapeDtypeStruct((m, d), F32),
        compiler_params=_params("parallel"),
        name="out_ln",
    )(y, w, x, g, b)


def _mlp_kernel(x_ref, w1_ref, w2_ref, g_ref, b_ref, o_ref, *, nf):
    j = pl.program_id(1)

    @pl.when(j == 0)
    def _():
        o_ref[...] = DN_ALPHA * x_ref[...]

    h = _mm(x_ref[...].astype(MXU_DTYPE), w1_ref[...].astype(MXU_DTYPE))
    h = jnp.square(jnp.maximum(h, 0.0)).astype(MXU_DTYPE)
    slab = o_ref.shape[1] // MLP_OUT_SLABS
    for s in range(MLP_OUT_SLABS):
        cols = slice(s * slab, (s + 1) * slab)
        o_ref[:, cols] += _mm(h, w2_ref[:, cols].astype(MXU_DTYPE))

    @pl.when(j == nf - 1)
    def _():
        o_ref[...] = _layer_norm(o_ref[...], g_ref[...], b_ref[...])


def _mlp(x, w1_stack, w2_stack, layer, ln, tm, tf):
    m, d = x.shape
    f = w1_stack.shape[2]
    nf = f // tf
    g, b, ln_layer = ln
    return pl.pallas_call(
        functools.partial(_mlp_kernel, nf=nf),
        grid=(m // tm, nf),
        in_specs=[pl.BlockSpec((tm, d), lambda i, j: (i, 0)),
                  pl.BlockSpec((None, d, tf), lambda i, j: (layer, 0, j)),
                  pl.BlockSpec((None, tf, d), lambda i, j: (layer, j, 0)),
                  _layer_row((1, d), ln_layer),
                  _layer_row((1, d), ln_layer)],
        out_specs=pl.BlockSpec((tm, d), lambda i, j: (i, 0)),
        out_shape=jax.ShapeDtypeStruct((m, d), F32),
        compiler_params=_params("parallel", "arbitrary"),
        name="mlp",
    )(x, w1_stack, w2_stack, g, b)


def _s5_in_kernel(x_ref, w_ref, o_ref):
    u = _mm(x_ref[...].astype(MXU_DTYPE), w_ref[...])
    for gs in range(S5_SETS):
        o_ref[gs] = u[:, gs * LANES:(gs + 1) * LANES]


def _s5_in(x, w, layer, tm):
    m = x.shape[0]
    return pl.pallas_call(
        _s5_in_kernel,
        grid=(m // tm,),
        in_specs=[pl.BlockSpec((tm, D_MODEL), lambda i: (i, 0)),
                  _layer_row((D_MODEL, D_MODEL), layer)],
        out_specs=pl.BlockSpec((S5_SETS, tm, LANES), lambda i: (0, i, 0)),
        out_shape=jax.ShapeDtypeStruct((S5_SETS, m, LANES), F32),
        compiler_params=_params("parallel"),
        name="s5_in",
    )(x, w)


def _s5_core_kernel(u_ref, lb_ref, rc_ref, cc_ref, a16_ref, d_ref, o_ref,
                    wt_ref, win_ref, wout_ref, v_ref, sp_ref, yin_ref, *, batch, chunks_per_batch):
    nst = S5_SET_STATE
    kdim = S5_CHUNK * LANES
    lb = lb_ref[0]

    rc = rc_ref[0]
    lb_hi = lb.astype(MXU_DTYPE)
    lb_lo = (lb - lb_hi.astype(F32)).astype(MXU_DTYPE)
    rc_hi = rc.astype(MXU_DTYPE)
    rc_lo = (rc - rc_hi.astype(F32)).astype(MXU_DTYPE)
    drev = _mm(jnp.concatenate([lb_hi, lb_lo], axis=1), jnp.concatenate([rc_hi, rc_hi], axis=0))
    drev = drev + _mm(lb_hi, rc_lo)
    row = lax.broadcasted_iota(jnp.int32, (kdim, LANES), 0)
    lane = lax.broadcasted_iota(jnp.int32, (kdim, LANES), 1)
    row_group = (row % LANES) // S5_GROUP
    drev = jnp.where(row_group == lane // S5_GROUP, drev, 0.0).astype(MXU_DTYPE)
    for t in range(S5_CHUNK):
        used = (t + 1) * LANES
        group_rows = (t // S5_TOEPLITZ_GROUP + 1) * S5_TOEPLITZ_GROUP * LANES
        wt_ref[0:used, t * LANES:(t + 1) * LANES] = drev[kdim - used:, :]
        if used < group_rows:
            wt_ref[used:group_rows, t * LANES:(t + 1) * LANES] = jnp.zeros((group_rows - used, LANES), MXU_DTYPE)

    swapped = pltpu.roll(lb, S5_STATE, axis=1)
    low = lane < S5_STATE
    re_dup = jnp.where(low, lb, swapped)
    im_dup = jnp.where(low, swapped, lb)
    for v in range(nst // LANES):
        sel = row_group == 2 * v + (lane >= S5_STATE).astype(jnp.int32)
        win_ref[:, v * LANES:(v + 1) * LANES] = jnp.where(sel, re_dup, 0.0).astype(MXU_DTYPE)
        win_ref[:, nst + v * LANES:nst + (v + 1) * LANES] = jnp.where(sel, im_dup, 0.0).astype(MXU_DTYPE)

    cc = cc_ref[0]
    lane_group = (lax.broadcasted_iota(jnp.int32, (S5_STATE, kdim), 1) % LANES) // S5_GROUP
    for ri in range(2):
        part = cc[ri * S5_STATE:(ri + 1) * S5_STATE, :]
        for g in range(S5_SET_GROUPS):
            r0 = ri * nst + g * S5_STATE
            wout_ref[r0:r0 + S5_STATE, :] = jnp.where(lane_group == g, part, 0.0).astype(MXU_DTYPE)

    nchunks = batch * chunks_per_batch
    u = jnp.concatenate([u_ref[pl.ds(t, nchunks, stride=S5_CHUNK), :] for t in range(S5_CHUNK)], axis=1)
    xb = u.astype(MXU_DTYPE)
    vin = _mm(xb, win_ref[...])
    ntile = nst // LANES
    for k in range(2 * ntile):
        v_ref[k] = vin[:, k * LANES:(k + 1) * LANES]
    gcols = S5_TOEPLITZ_GROUP * LANES
    for j in range(S5_CHUNK // S5_TOEPLITZ_GROUP):
        cols = slice(j * gcols, (j + 1) * gcols)
        krows = (j + 1) * gcols
        yin_ref[:, cols] = _mm(xb[:, :krows], wt_ref[0:krows, cols]) + d_ref[0][:, cols] * u[:, cols]

    a_re = [a16_ref[0, :, k * LANES:(k + 1) * LANES] for k in range(ntile)]
    a_im = [a16_ref[0, :, nst + k * LANES:nst + (k + 1) * LANES] for k in range(ntile)]

    def step(c, carry):
        rows = pl.ds(c, batch, stride=chunks_per_batch)
        new = []
        for k in range(ntile):
            s_re, s_im = carry[k], carry[ntile + k]
            sp_ref[k, rows, :] = s_re
            sp_ref[ntile + k, rows, :] = s_im
            new.append((a_re[k] * s_re - a_im[k] * s_im + v_ref[k, rows, :],
                        a_re[k] * s_im + a_im[k] * s_re + v_ref[ntile + k, rows, :]))
        return tuple(n[0] for n in new) + tuple(n[1] for n in new)

    zero = jnp.zeros((batch, LANES), F32)
    lax.fori_loop(0, chunks_per_batch, step, (zero,) * (2 * ntile), unroll=S5_SCAN_UNROLL)

    sp = jnp.concatenate([sp_ref[k] for k in range(2 * ntile)], axis=1)
    h = jax.nn.gelu(yin_ref[...] + _mm(sp.astype(MXU_DTYPE), wout_ref[...]))
    for t in range(S5_CHUNK):
        o_ref[pl.ds(t, nchunks, stride=S5_CHUNK), :] = h[:, t * LANES:(t + 1) * LANES]


def _s5_core(ug, tables, layer, batch):
    nsets, m, _ = ug.shape
    nchunks = m // S5_CHUNK
    kdim = S5_CHUNK * LANES
    kern = functools.partial(_s5_core_kernel, batch=batch, chunks_per_batch=nchunks // batch)
    table = lambda rows, cols: pl.BlockSpec((None, 1, rows, cols), lambda s: (layer, s, 0, 0))
    return pl.pallas_call(
        kern,
        grid=(nsets,),
        in_specs=[pl.BlockSpec((None, m, LANES), lambda s: (s, 0, 0)),
                  table(kdim, LANES),
                  table(LANES, LANES),
                  table(LANES, kdim),
                  table(1, 2 * S5_SET_STATE),
                  table(1, kdim)],
        out_specs=pl.BlockSpec((None, m, LANES), lambda s: (s, 0, 0)),
        out_shape=jax.ShapeDtypeStruct((nsets, m, LANES), F32),
        scratch_shapes=[pltpu.VMEM((kdim, kdim), MXU_DTYPE),
                        pltpu.VMEM((kdim, 2 * S5_SET_STATE), MXU_DTYPE),
                        pltpu.VMEM((2 * S5_SET_STATE, kdim), MXU_DTYPE),
                        pltpu.VMEM((2 * S5_SET_STATE // LANES, nchunks, LANES), F32),
                        pltpu.VMEM((2 * S5_SET_STATE // LANES, nchunks, LANES), F32),
                        pltpu.VMEM((nchunks, kdim), F32)],
        compiler_params=_params("parallel"),
        name="s5_core",
    )(ug, *tables)


def _s5_out_kernel(h_ref, wo_ref, wg_ref, x_ref, g_ref, b_ref, o_ref):
    half = o_ref.shape[0] // ROW_SPLIT
    for r in range(ROW_SPLIT):
        rows = slice(r * half, (r + 1) * half)
        h = jnp.concatenate([h_ref[gs, rows, :] for gs in range(S5_SETS)], axis=1).astype(MXU_DTYPE)
        f = _mm(h, wo_ref[...]) * jax.nn.sigmoid(_mm(h, wg_ref[...]))
        o_ref[rows, :] = _layer_norm(DN_ALPHA * x_ref[rows, :] + f, g_ref[...], b_ref[...])


def _s5_out(hg, wo, wg, w_layer, x, ln, tm):
    m = x.shape[0]
    g, b, layer = ln
    return pl.pallas_call(
        _s5_out_kernel,
        grid=(m // tm,),
        in_specs=[pl.BlockSpec((S5_SETS, tm, LANES), lambda i: (0, i, 0)),
                  _layer_row((D_MODEL, D_MODEL), w_layer),
                  _layer_row((D_MODEL, D_MODEL), w_layer),
                  pl.BlockSpec((tm, D_MODEL), lambda i: (i, 0)),
                  _layer_row((1, D_MODEL), layer),
                  _layer_row((1, D_MODEL), layer)],
        out_specs=pl.BlockSpec((tm, D_MODEL), lambda i: (i, 0)),
        out_shape=jax.ShapeDtypeStruct(x.shape, F32),
        compiler_params=_params("parallel"),
        name="s5_out",
    )(hg, wo, wg, x, g, b)


def _s5_tables(lam_re, lam_im, log_dt, b_re, b_im, c_re, c_im, d_skip):
    lr = lam_re.astype(F32)
    li = lam_im.astype(F32)
    dt = jnp.exp(log_dt.astype(F32))[:, None]
    mag = jnp.exp(lr * dt)
    ar = mag * jnp.cos(li * dt)
    ai = mag * jnp.sin(li * dt)
    den = lr * lr + li * li
    zr = ((ar - 1.0) * lr + ai * li) / den
    zi = (ai * lr - (ar - 1.0) * li) / den
    br_ = b_re.astype(F32)
    bi_ = b_im.astype(F32)
    bbr = zr[..., None] * br_ - zi[..., None] * bi_
    bbi = zr[..., None] * bi_ + zi[..., None] * br_
    k = jnp.arange(S5_CHUNK + 1, dtype=F32)[:, None, None]
    pmag = jnp.exp(k * (lr * dt))
    pr = pmag * jnp.cos(k * (li * dt))
    pi = pmag * jnp.sin(k * (li * dt))

    bt_re = jnp.swapaxes(bbr, 1, 2)
    bt_im = jnp.swapaxes(bbi, 1, 2)
    shape_b = (S5_SETS, 1, S5_SET_GROUPS, S5_GROUP, 2 * S5_STATE)
    b_with_re = jnp.concatenate([bt_re, bt_im], axis=-1).reshape(shape_b)
    b_with_im = jnp.concatenate([-bt_im, bt_re], axis=-1).reshape(shape_b)
    def power_rows(t):
        t = jnp.concatenate([t, t], axis=-1).reshape(S5_CHUNK, S5_SETS, S5_SET_GROUPS, 1, 2 * S5_STATE)
        return t.transpose(1, 0, 2, 3, 4)
    lb = power_rows(pr[S5_CHUNK - 1::-1]) * b_with_re + power_rows(pi[S5_CHUNK - 1::-1]) * b_with_im
    lb = lb.reshape(S5_SETS, S5_CHUNK * LANES, 2 * S5_STATE)

    def rows_p(t):
        return t.astype(F32).reshape(S5_SETS, S5_SET_GROUPS, S5_GROUP, S5_STATE).transpose(0, 3, 1, 2)
    ct_re = rows_p(c_re)
    ct_im = rows_p(c_im)
    rc = jnp.concatenate([ct_re, -ct_im], axis=1).reshape(S5_SETS, 2 * S5_STATE, LANES)

    def power_lanes(t):
        return t.reshape(S5_CHUNK, S5_SETS, S5_SET_GROUPS, S5_STATE).transpose(1, 3, 0, 2)[..., None]
    prs = power_lanes(pr[1:])
    pis = power_lanes(pi[1:])
    ca_re = ct_re[:, :, None] * prs - ct_im[:, :, None] * pis
    ca_im = ct_re[:, :, None] * pis + ct_im[:, :, None] * prs
    cc = jnp.concatenate([ca_re, -ca_im], axis=1).reshape(S5_SETS, 2 * S5_STATE, S5_CHUNK * LANES)

    a16 = jnp.concatenate([pr[S5_CHUNK].reshape(S5_SETS, 1, S5_SET_STATE),
                           pi[S5_CHUNK].reshape(S5_SETS, 1, S5_SET_STATE)], axis=2)
    drow = jnp.tile(d_skip.astype(F32).reshape(S5_SETS, 1, LANES), (1, 1, S5_CHUNK))
    return lb, rc, cc, a16, drow


def _s5_layer(x, batch, w_in, tables, layer, w_out, w_gate, ln):
    ug = _s5_in(x, w_in, layer, tm=512)
    hg = _s5_core(ug, tables, layer, batch)
    return _s5_out(hg, w_out, w_gate, layer, x, ln, tm=512)


def _split_hi_lo(v):
    hi = v.astype(MXU_DTYPE)
    lo = (v - hi.astype(F32)).astype(MXU_DTYPE)
    return jnp.concatenate([hi, lo], axis=1)


def _ssd_core_kernel(z_ref, xs_ref, bc_ref, xin_ref, wdt_ref, cwx_ref, cwb_ref, cbx_ref, cbb_ref,
                     dtb_ref, a_ref, dsk_ref, nw_ref, e2_ref, o_ref,
                     xbuf_ref, bbuf_ref, st_ref, yd_ref):
    q = SSD_CHUNK
    pad = 8

    @pl.when(pl.program_id(1) == 0)
    def _():
        xbuf_ref[...] = jnp.zeros_like(xbuf_ref)
        bbuf_ref[...] = jnp.zeros_like(bbuf_ref)
        st_ref[...] = jnp.zeros_like(st_ref)

    def conv(in_ref, tail_ref, w_ref, b_ref):
        cur = in_ref[...]
        tail = tail_ref[...]
        row8 = lax.broadcasted_iota(jnp.int32, tail.shape, 0)
        acc = b_ref[...] + w_ref[SSD_CONV - 1:SSD_CONV, :] * cur
        for k in range(1, SSD_CONV):
            back = pltpu.roll(cur, k, axis=0)
            head = jnp.where(row8 < k, pltpu.roll(tail, k, axis=0), back[0:pad])
            back = jnp.concatenate([head, back[pad:]], axis=0)
            acc = acc + w_ref[SSD_CONV - 1 - k:SSD_CONV - k, :] * back
        tail_ref[...] = cur[q - pad:, :]
        return _silu(acc)

    xs = conv(xs_ref, xbuf_ref, cwx_ref, cbx_ref)
    bc = conv(bc_ref, bbuf_ref, cwb_ref, cbb_ref)

    dtr = _mm(xin_ref[...].astype(MXU_DTYPE), wdt_ref[...]) + dtb_ref[...]
    dt = jnp.maximum(dtr, 0.0) + jnp.log1p(jnp.exp(-jnp.abs(dtr)))
    adt = dt * a_ref[...]
    ri = lax.broadcasted_iota(jnp.int32, (q, q), 0)
    ci = lax.broadcasted_iota(jnp.int32, (q, q), 1)
    causal = ri >= ci
    tri = causal.astype(F32)
    cs = jnp.dot(tri, adt, preferred_element_type=F32, precision=HI)
    eye = (ri == ci).astype(F32)
    cs2 = cs * LOG2_E
    r_rows = _mm_nt(eye, cs2, precision=HI) - jnp.log2(_mm_nt(eye, dt, precision=HI))
    tot = cs[q - 1:q, :]
    w_col = dt * jnp.exp(tot - cs)
    din = jnp.exp(cs)
    w_x = _mm(_split_hi_lo(w_col), e2_ref[...])
    din_x = _mm(_split_hi_lo(din), e2_ref[...])

    lane = lax.broadcasted_iota(jnp.int32, (q, LANES), 1)
    low = lane < SSD_HEADDIM

    xw = (xs * w_x).astype(MXU_DTYPE)
    xsb = xs.astype(MXU_DTYPE)
    for g in range(SSD_GROUPS):
        bg = bc[:, g * SSD_STATE:(g + 1) * SSD_STATE].astype(MXU_DTYPE)
        cg = bc[:, SSD_GROUPS * SSD_STATE + g * SSD_STATE:
                SSD_GROUPS * SSD_STATE + (g + 1) * SSD_STATE].astype(MXU_DTYPE)
        cb = _mm_nt(cg, bg)
        for pair in range(SSD_HEADS_PER_GROUP // 2):
            h0 = g * SSD_HEADS_PER_GROUP + 2 * pair
            ms = []
            for h in (h0, h0 + 1):
                seg = cs2[:, h:h + 1] - r_rows[h:h + 1, :]
                ms.append((cb * jnp.exp2(jnp.where(causal, seg, -jnp.inf))).astype(MXU_DTYPE))
            lhs = jnp.concatenate(ms, axis=1)
            xp = xsb[:, h0 * SSD_HEADDIM:(h0 + 2) * SSD_HEADDIM]
            zero = jnp.zeros_like(xp)
            rhs = jnp.concatenate([jnp.where(low, xp, zero), jnp.where(low, zero, xp)], axis=0)
            yd_ref[:, h0 * SSD_HEADDIM:(h0 + 2) * SSD_HEADDIM] = _mm(lhs, rhs)
        gsl = slice(g * SSD_GROUP_DIM, (g + 1) * SSD_GROUP_DIM)
        st = st_ref[g]
        y_off = _mm(cg, st.astype(MXU_DTYPE)) * din_x[:, gsl]
        yd_ref[:, gsl] = yd_ref[:, gsl] + y_off
        st_ref[g] = din_x[q - 1:q, gsl] * st + _mm_tn(bg, xw[:, gsl])

    y = yd_ref[...] + dsk_ref[...] * xs
    y = y * z_ref[...]
    for g in range(SSD_GROUPS):
        gsl = slice(g * SSD_GROUP_DIM, (g + 1) * SSD_GROUP_DIM)
        yg = y[:, gsl]
        yg = yg * lax.rsqrt(jnp.mean(yg * yg, axis=-1, keepdims=True) + LN_EPS)
        o_ref[:, gsl] = (yg * nw_ref[:, gsl]).astype(o_ref.dtype)


def _ssd_core(proj, x, w_dt, batch, conv_w, conv_b, dt_bias, a_log, d_skip, norm_w):
    m = proj.shape[0]
    q = SSD_CHUNK
    ncb = m // batch // q
    di = SSD_D_INNER
    wdt = jnp.pad(w_dt.astype(MXU_DTYPE), ((0, 0), (0, LANES - SSD_HEADS)))
    cw = conv_w.astype(F32)
    cbias = conv_b.astype(F32)[None, :]
    padh = LANES - SSD_HEADS
    dtb = jnp.pad(dt_bias.astype(F32), (0, padh))[None, :]
    a = jnp.pad(-jnp.exp(a_log.astype(F32)), (0, padh))[None, :]
    dsk = jnp.repeat(d_skip.astype(F32), SSD_HEADDIM)[None, :]
    nw = norm_w.astype(F32)[None, :]
    expand = (np.arange(LANES)[:, None] == (np.arange(di) // SSD_HEADDIM)[None, :]).astype(np.float32)
    e2 = jnp.asarray(np.concatenate([expand, expand], axis=0), dtype=MXU_DTYPE)

    row = lambda b, c: b * ncb + c
    full = _resident
    return pl.pallas_call(
        _ssd_core_kernel,
        grid=(batch, ncb),
        in_specs=[pl.BlockSpec((q, di), lambda b, c: (row(b, c), 0)),
                  pl.BlockSpec((q, di), lambda b, c: (row(b, c), 1)),
                  pl.BlockSpec((q, SSD_BC), lambda b, c: (row(b, c), 2 * di // SSD_BC)),
                  pl.BlockSpec((q, D_MODEL), lambda b, c: (row(b, c), 0)),
                  full((D_MODEL, LANES)),
                  full((SSD_CONV, di)), full((SSD_CONV, SSD_BC)), full((1, di)), full((1, SSD_BC)),
                  full((1, LANES)), full((1, LANES)), full((1, di)), full((1, di)),
                  full((2 * LANES, di))],
        out_specs=pl.BlockSpec((q, di), lambda b, c: (row(b, c), 0)),
        out_shape=jax.ShapeDtypeStruct((m, di), MXU_DTYPE),
        scratch_shapes=[pltpu.VMEM((8, di), F32), pltpu.VMEM((8, SSD_BC), F32),
                        pltpu.VMEM((SSD_GROUPS, SSD_STATE, SSD_GROUP_DIM), F32),
                        pltpu.VMEM((q, di), F32)],
        compiler_params=_params("parallel", "arbitrary"),
        name="ssd_core",
    )(proj, proj, proj, x, wdt, cw[:, :di], cw[:, di:], cbias[:, :di], cbias[:, di:], dtb, a, dsk, nw, e2)


def _ssd_layer(x, batch, w_in_stack, layer, conv_w, conv_b, dt_bias, a_log, d_skip, norm_w, w_out, ln):
    n_main = SSD_D_INNER + SSD_CONV_DIM
    proj = _proj(x, w_in_stack.astype(MXU_DTYPE), layer, n_main, tm=min(1024, x.shape[0]), tn=1024,
                 out_dtype=F32, silu_cols=SSD_D_INNER)
    w_dt = w_in_stack[layer, :, n_main:]
    y = _ssd_core(proj, x, w_dt, batch, conv_w, conv_b, dt_bias, a_log, d_skip, norm_w)
    return _out_ln(y, w_out.astype(MXU_DTYPE), x, ln, tm=512)


def _ret_core_kernel(cd_ref, q_ref, k_ref, v_ref, g_ref, cos_ref, sin_ref, dmat_ref, xi_ref, zeta_ref,
                     o_ref, st_ref):
    @pl.when(pl.program_id(1) == 0)
    def _():
        st_ref[...] = jnp.zeros_like(st_ref)

    cos = cos_ref[...]
    sin = sin_ref[...]
    half = RET_DK // 2

    def rotate(t):
        t1 = t[:, :half]
        t2 = t[:, half:]
        return jnp.concatenate([t1 * cos - t2 * sin, t1 * sin + t2 * cos], axis=1)

    for h in range(RET_HEADS):
        qh = rotate(q_ref[:, h * RET_DK:(h + 1) * RET_DK].astype(F32))
        kh = rotate(k_ref[:, h * RET_DK:(h + 1) * RET_DK].astype(F32) * (RET_DK ** -0.5))
        vh = v_ref[:, h * RET_DV:(h + 1) * RET_DV].astype(MXU_DTYPE)
        qb = qh.astype(MXU_DTYPE)
        scores = _mm_nt(qb, kh.astype(MXU_DTYPE)) * dmat_ref[h]
        inner = _mm(scores.astype(MXU_DTYPE), vh)
        st = st_ref[h]
        xi = jnp.concatenate([xi_ref[h]] * (RET_DV // LANES), axis=1)
        cross = _mm(qb, st.astype(MXU_DTYPE)) * xi
        kz = kh * jnp.concatenate([zeta_ref[h]] * (RET_DK // LANES), axis=1)
        st_ref[h] = cd_ref[h] * st + _mm_tn(kz.astype(MXU_DTYPE), vh)
        out = inner + cross
        mu = jnp.mean(out, axis=-1, keepdims=True)
        d = out - mu
        var = jnp.mean(d * d, axis=-1, keepdims=True)
        out = d * lax.rsqrt(var + LN_EPS)
        gate = _silu(g_ref[:, h * RET_DV:(h + 1) * RET_DV].astype(F32))
        o_ref[:, h * RET_DV:(h + 1) * RET_DV] = (gate * out).astype(o_ref.dtype)


def _ret_core(proj, batch):
    m = proj.shape[0]
    seq = m // batch
    q = RET_CHUNK
    ncb = seq // q
    theta = (1.0 / (RET_ROPE_BASE ** np.linspace(0.0, 1.0, RET_DK // 2))).astype(np.float32)
    ang = (np.arange(seq, dtype=np.float32)[:, None] * theta[None, :]).astype(np.float64)
    cos, sin = np.cos(ang).astype(np.float32), np.sin(ang).astype(np.float32)
    lg = np.log1p(-np.exp2(-5.0 - np.arange(RET_HEADS)))
    pos = np.arange(q, dtype=np.float64)
    diff = pos[:, None] - pos[None, :]
    dmat = np.where(diff >= 0, np.exp(np.maximum(diff, 0.0)[None] * lg[:, None, None]), 0.0).astype(np.float32)
    xi = np.exp((pos[None, :] + 1.0) * lg[:, None])
    zeta = np.exp((q - 1.0 - pos[None, :]) * lg[:, None])
    xi_b = np.broadcast_to(xi[:, :, None], (RET_HEADS, q, LANES)).astype(np.float32)
    zeta_b = np.broadcast_to(zeta[:, :, None], (RET_HEADS, q, LANES)).astype(np.float32)
    chunk_decay = np.exp(q * lg).astype(np.float32)

    row = lambda b, c: b * ncb + c
    d = D_MODEL
    return pl.pallas_call(
        _ret_core_kernel,
        grid=(batch, ncb),
        in_specs=[pl.BlockSpec(memory_space=pltpu.SMEM),
                  pl.BlockSpec((q, d), lambda b, c: (row(b, c), 0)),
                  pl.BlockSpec((q, d), lambda b, c: (row(b, c), 1)),
                  pl.BlockSpec((q, 2 * d), lambda b, c: (row(b, c), 1)),
                  pl.BlockSpec((q, 2 * d), lambda b, c: (row(b, c), 2)),
                  pl.BlockSpec((q, RET_DK // 2), lambda b, c: (c, 0)),
                  pl.BlockSpec((q, RET_DK // 2), lambda b, c: (c, 0)),
                  _resident((RET_HEADS, q, q)),
                  _resident((RET_HEADS, q, LANES)),
                  _resident((RET_HEADS, q, LANES))],
        out_specs=pl.BlockSpec((q, 2 * d), lambda b, c: (row(b, c), 0)),
        out_shape=jax.ShapeDtypeStruct((m, 2 * d), MXU_DTYPE),
        scratch_shapes=[pltpu.VMEM((RET_HEADS, RET_DK, RET_DV), F32)],
        compiler_params=_params("parallel", "arbitrary"),
        name="ret_core",
    )(chunk_decay, proj, proj, proj, proj, cos, sin, dmat, xi_b, zeta_b)


def _ret_layer(x, batch, w_in_stack, layer, w_out, ln):
    proj = _proj(x, w_in_stack, layer, RET_IN_DIM, tm=min(1024, x.shape[0]), tn=1024, out_dtype=MXU_DTYPE)
    y = _ret_core(proj, batch)
    return _out_ln(y, w_out.astype(MXU_DTYPE), x, ln, tm=512)


def kernel(x, ln1_g, ln1_b, ln2_g, ln2_b, mlp_w1, mlp_w2, s5_w_in, s5_lam_re, s5_lam_im, s5_log_dt, s5_b_re, s5_b_im, s5_c_re, s5_c_im, s5_d, s5_w_out, s5_w_gate, ssd_w_in, ssd_conv_w, ssd_conv_b, ssd_dt_bias, ssd_a_log, ssd_d, ssd_norm_w, ssd_w_out, ret_w_in, ret_w_out):
    batch, seq, d = x.shape
    h = x.reshape(batch * seq, d)
    ln1 = (ln1_g[:, None, :], ln1_b[:, None, :])
    ln2 = (ln2_g[:, None, :], ln2_b[:, None, :])
    s5_tables = jax.vmap(_s5_tables)(s5_lam_re, s5_lam_im, s5_log_dt, s5_b_re, s5_b_im, s5_c_re, s5_c_im, s5_d)
    s5_w_in_b = s5_w_in.astype(MXU_DTYPE)
    s5_w_out_b = s5_w_out.astype(MXU_DTYPE)
    s5_w_gate_b = s5_w_gate.astype(MXU_DTYPE)
    for i in range(DEPTH):
        kind = i % 3
        j = i // 3
        if kind == 0:
            h = _s5_layer(h, batch, s5_w_in_b, s5_tables, j, s5_w_out_b, s5_w_gate_b, ln1 + (i,))
        elif kind == 1:
            h = _ssd_layer(h, batch, ssd_w_in, j, ssd_conv_w[j], ssd_conv_b[j], ssd_dt_bias[j], ssd_a_log[j],
                           ssd_d[j], ssd_norm_w[j], ssd_w_out[j], ln1 + (i,))
        else:
            h = _ret_layer(h, batch, ret_w_in, j, ret_w_out[j], ln1 + (i,))
        h = _mlp(h, mlp_w1, mlp_w2, i, ln2 + (i,), tm=min(1024, h.shape[0]), tf=512)
    return h.reshape(batch, seq, d)
```

```python
import functools
import math

import jax
import jax.numpy as jnp
import numpy as np
from jax import lax
from jax.experimental import pallas as pl
from jax.experimental.pallas import tpu as pltpu

F32 = jnp.float32
MXU_DTYPE = jnp.bfloat16
HI = lax.Precision.HIGHEST
LOG2_E = 1.0 / math.log(2.0)

LANES = 128
VMEM_PHYSICAL_BYTES = 64 * 1024 * 1024
VMEM_LIMIT_BYTES = VMEM_PHYSICAL_BYTES - 6 * 1024 * 1024

D_MODEL = 2048
DEPTH = 4
DN_ALPHA = (2 * DEPTH) ** 0.25
LN_EPS = 1e-5
D_FF = 4 * D_MODEL
MLP_OUT_SLABS = 4
ROW_SPLIT = 2

S5_GROUP = 16
S5_GROUPS = D_MODEL // S5_GROUP
S5_STATE = 64
S5_CHUNK = 16
S5_SETS = D_MODEL // LANES
S5_SET_GROUPS = LANES // S5_GROUP
S5_SET_STATE = S5_SET_GROUPS * S5_STATE
S5_SCAN_UNROLL = 8
S5_TOEPLITZ_GROUP = 4

SSD_D_INNER = 2 * D_MODEL
SSD_HEADDIM = 64
SSD_HEADS = SSD_D_INNER // SSD_HEADDIM
SSD_GROUPS = 8
SSD_STATE = 128
SSD_CONV = 4
SSD_CHUNK = 128
SSD_BC = 2 * SSD_GROUPS * SSD_STATE
SSD_CONV_DIM = SSD_D_INNER + SSD_BC
SSD_GROUP_DIM = SSD_D_INNER // SSD_GROUPS
SSD_HEADS_PER_GROUP = SSD_HEADS // SSD_GROUPS

RET_HEADS = 8
RET_DK = D_MODEL // RET_HEADS
RET_DV = 2 * D_MODEL // RET_HEADS
RET_CHUNK = 128
RET_IN_DIM = 6 * D_MODEL
RET_ROPE_BASE = 10000.0


def _params(*semantics):
    return pltpu.CompilerParams(dimension_semantics=semantics, vmem_limit_bytes=VMEM_LIMIT_BYTES)


def _layer_norm(v, g, b):
    mu = jnp.mean(v, axis=-1, keepdims=True)
    d = v - mu
    var = jnp.mean(d * d, axis=-1, keepdims=True)
    return d * lax.rsqrt(var + LN_EPS) * g + b


def _silu(v):
    h = 0.5 * v
    return h + h * jnp.tanh(h)


def _mm(a, b):
    return jnp.dot(a, b, preferred_element_type=F32)


def _mm_nt(a, b, precision=None):
    return lax.dot_general(a, b, (((1,), (1,)), ((), ())), preferred_element_type=F32, precision=precision)


def _mm_tn(a, b):
    return lax.dot_general(a, b, (((0,), (0,)), ((), ())), preferred_element_type=F32)


def _resident(shape):
    return pl.BlockSpec(shape, lambda *_: (0,) * len(shape), pipeline_mode=pl.Buffered(1))


def _layer_row(shape, layer):
    return pl.BlockSpec((None,) + shape, lambda *_: (layer, 0, 0), pipeline_mode=pl.Buffered(1))


def _proj_kernel(x_ref, w_ref, o_ref, *, act_tiles):
    j = pl.program_id(1)

    def product():
        return _mm(x_ref[...].astype(MXU_DTYPE), w_ref[...].astype(MXU_DTYPE))

    if act_tiles == 0:
        o_ref[...] = product().astype(o_ref.dtype)
    else:
        @pl.when(j < act_tiles)
        def _():
            o_ref[...] = _silu(product()).astype(o_ref.dtype)

        @pl.when(j >= act_tiles)
        def _():
            o_ref[...] = product().astype(o_ref.dtype)


def _proj(x, w_stack, layer, n, tm, tn, out_dtype, silu_cols=0):
    m, k = x.shape
    assert silu_cols % tn == 0
    return pl.pallas_call(
        functools.partial(_proj_kernel, act_tiles=silu_cols // tn),
        grid=(m // tm, n // tn),
        in_specs=[pl.BlockSpec((tm, k), lambda i, j: (i, 0)),
                  pl.BlockSpec((None, k, tn), lambda i, j: (layer, 0, j))],
        out_specs=pl.BlockSpec((tm, tn), lambda i, j: (i, j)),
        out_shape=jax.ShapeDtypeStruct((m, n), out_dtype),
        compiler_params=_params("parallel", "arbitrary"),
        name="proj",
    )(x, w_stack)


def _out_ln_kernel(y_ref, w_ref, x_ref, g_ref, b_ref, o_ref):
    half = o_ref.shape[0] // ROW_SPLIT
    for r in range(ROW_SPLIT):
        rows = slice(r * half, (r + 1) * half)
        f = _mm(y_ref[rows, :], w_ref[...])
        o_ref[rows, :] = _layer_norm(DN_ALPHA * x_ref[rows, :] + f, g_ref[...], b_ref[...])


def _out_ln(y, w, x, ln, tm):
    m, k = y.shape
    d = w.shape[1]
    g, b, layer = ln
    return pl.pallas_call(
        _out_ln_kernel,
        grid=(m // tm,),
        in_specs=[pl.BlockSpec((tm, k), lambda i: (i, 0)),
                  _resident((k, d)),
                  pl.BlockSpec((tm, d), lambda i: (i, 0)),
                  _layer_row((1, d), layer),
                  _layer_row((1, d), layer)],
        out_specs=pl.BlockSpec((tm, d), lambda i: (i, 0)),
        out_shape=jax.ShapeDtypeStruct((m, d), F32),
        compiler_params=_params("parallel"),
        name="out_ln",
    )(y, w, x, g, b)


def _mlp_kernel(x_ref, w1_ref, w2_ref, g_ref, b_ref, o_ref, *, nf):
    j = pl.program_id(1)

    @pl.when(j == 0)
    def _():
        o_ref[...] = DN_ALPHA * x_ref[...]

    h = _mm(x_ref[...].astype(MXU_DTYPE), w1_ref[...].astype(MXU_DTYPE))
    h = jnp.square(jnp.maximum(h, 0.0)).astype(MXU_DTYPE)
    slab = o_ref.shape[1] // MLP_OUT_SLABS
    for s in range(MLP_OUT_SLABS):
        cols = slice(s * slab, (s + 1) * slab)
        o_ref[:, cols] += _mm(h, w2_ref[:, cols].astype(MXU_DTYPE))

    @pl.when(j == nf - 1)
    def _():
        o_ref[...] = _layer_norm(o_ref[...], g_ref[...], b_ref[...])


def _mlp(x, w1_stack, w2_stack, layer, ln, tm, tf):
    m, d = x.shape
    f = w1_stack.shape[2]
    nf = f // tf
    g, b, ln_layer = ln
    return pl.pallas_call(
        functools.partial(_mlp_kernel, nf=nf),
        grid=(m // tm, nf),
        in_specs=[pl.BlockSpec((tm, d), lambda i, j: (i, 0)),
                  pl.BlockSpec((None, d, tf), lambda i, j: (layer, 0, j)),
                  pl.BlockSpec((None, tf, d), lambda i, j: (layer, j, 0)),
                  _layer_row((1, d), ln_layer),
                  _layer_row((1, d), ln_layer)],
        out_specs=pl.BlockSpec((tm, d), lambda i, j: (i, 0)),
        out_shape=jax.ShapeDtypeStruct((m, d), F32),
        compiler_params=_params("parallel", "arbitrary"),
        name="mlp",
    )(x, w1_stack, w2_stack, g, b)


def _s5_in_kernel(x_ref, w_ref, o_ref, wb_ref):
    @pl.when(pl.program_id(0) == 0)
    def _():
        wb_ref[...] = w_ref[...].astype(MXU_DTYPE)

    u = _mm(x_ref[...].astype(MXU_DTYPE), wb_ref[...])
    for gs in range(S5_SETS):
        o_ref[gs] = u[:, gs * LANES:(gs + 1) * LANES]


def _s5_in(x, w, layer, tm):
    m = x.shape[0]
    return pl.pallas_call(
        _s5_in_kernel,
        grid=(m // tm,),
        in_specs=[pl.BlockSpec((tm, D_MODEL), lambda i: (i, 0)),
                  _layer_row((D_MODEL, D_MODEL), layer)],
        out_specs=pl.BlockSpec((S5_SETS, tm, LANES), lambda i: (0, i, 0)),
        out_shape=jax.ShapeDtypeStruct((S5_SETS, m, LANES), F32),
        scratch_shapes=[pltpu.VMEM((D_MODEL, D_MODEL), MXU_DTYPE)],
        compiler_params=_params("arbitrary"),
        name="s5_in",
    )(x, w)


def _s5_core_kernel(u_ref, lb_ref, rc_ref, cre_ref, cim_ref, a16_ref, d_ref, o_ref,
                    wt_ref, win_ref, wout_ref, v_ref, sp_ref, yin_ref, *, batch, chunks_per_batch):
    nst = S5_SET_STATE
    kdim = S5_CHUNK * LANES
    lb = lb_ref[0]

    rc = rc_ref[0]
    lb_hi = lb.astype(MXU_DTYPE)
    lb_lo = (lb - lb_hi.astype(F32)).astype(MXU_DTYPE)
    rc_hi = rc.astype(MXU_DTYPE)
    rc_lo = (rc - rc_hi.astype(F32)).astype(MXU_DTYPE)
    drev = _mm(jnp.concatenate([lb_hi, lb_lo], axis=1), jnp.concatenate([rc_hi, rc_hi], axis=0))
    drev = drev + _mm(lb_hi, rc_lo)
    row = lax.broadcasted_iota(jnp.int32, (kdim, LANES), 0)
    lane = lax.broadcasted_iota(jnp.int32, (kdim, LANES), 1)
    row_group = (row % LANES) // S5_GROUP
    drev = jnp.where(row_group == lane // S5_GROUP, drev, 0.0).astype(MXU_DTYPE)
    for t in range(S5_CHUNK):
        used = (t + 1) * LANES
        group_rows = (t // S5_TOEPLITZ_GROUP + 1) * S5_TOEPLITZ_GROUP * LANES
        wt_ref[0:used, t * LANES:(t + 1) * LANES] = drev[kdim - used:, :]
        if used < group_rows:
            wt_ref[used:group_rows, t * LANES:(t + 1) * LANES] = jnp.zeros((group_rows - used, LANES), MXU_DTYPE)

    swapped = pltpu.roll(lb, S5_STATE, axis=1)
    low = lane < S5_STATE
    re_dup = jnp.where(low, lb, swapped)
    im_dup = jnp.where(low, swapped, lb)
    for v in range(nst // LANES):
        sel = row_group == 2 * v + (lane >= S5_STATE).astype(jnp.int32)
        win_ref[:, v * LANES:(v + 1) * LANES] = jnp.where(sel, re_dup, 0.0).astype(MXU_DTYPE)
        win_ref[:, nst + v * LANES:nst + (v + 1) * LANES] = jnp.where(sel, im_dup, 0.0).astype(MXU_DTYPE)

    lane_group = (lax.broadcasted_iota(jnp.int32, (S5_STATE, kdim), 1) % LANES) // S5_GROUP
    for ri, part in enumerate((cre_ref[0], -cim_ref[0])):
        for g in range(S5_SET_GROUPS):
            r0 = ri * nst + g * S5_STATE
            wout_ref[r0:r0 + S5_STATE, :] = jnp.where(lane_group == g, part, 0.0).astype(MXU_DTYPE)

    nchunks = batch * chunks_per_batch
    u = jnp.concatenate([u_ref[pl.ds(t, nchunks, stride=S5_CHUNK), :] for t in range(S5_CHUNK)], axis=1)
    xb = u.astype(MXU_DTYPE)
    vin = _mm(xb, win_ref[...])
    ntile = nst // LANES
    for k in range(2 * ntile):
        v_ref[k] = vin[:, k * LANES:(k + 1) * LANES]
    gcols = S5_TOEPLITZ_GROUP * LANES
    for j in range(S5_CHUNK // S5_TOEPLITZ_GROUP):
        cols = slice(j * gcols, (j + 1) * gcols)
        krows = (j + 1) * gcols
        yin_ref[:, cols] = _mm(xb[:, :krows], wt_ref[0:krows, cols]) + d_ref[0][:, cols] * u[:, cols]

    a_re = [a16_ref[0, :, k * LANES:(k + 1) * LANES] for k in range(ntile)]
    a_im = [a16_ref[0, :, nst + k * LANES:nst + (k + 1) * LANES] for k in range(ntile)]

    def step(c, carry):
        rows = pl.ds(c, batch, stride=chunks_per_batch)
        new = []
        for k in range(ntile):
            s_re, s_im = carry[k], carry[ntile + k]
            sp_ref[k, rows, :] = s_re
            sp_ref[ntile + k, rows, :] = s_im
            new.append((a_re[k] * s_re - a_im[k] * s_im + v_ref[k, rows, :],
                        a_re[k] * s_im + a_im[k] * s_re + v_ref[ntile + k, rows, :]))
        return tuple(n[0] for n in new) + tuple(n[1] for n in new)

    zero = jnp.zeros((batch, LANES), F32)
    lax.fori_loop(0, chunks_per_batch, step, (zero,) * (2 * ntile), unroll=S5_SCAN_UNROLL)

    sp = jnp.concatenate([sp_ref[k] for k in range(2 * ntile)], axis=1)
    h = jax.nn.gelu(yin_ref[...] + _mm(sp.astype(MXU_DTYPE), wout_ref[...]))
    for t in range(S5_CHUNK):
        o_ref[pl.ds(t, nchunks, stride=S5_CHUNK), :] = h[:, t * LANES:(t + 1) * LANES]


def _s5_core(ug, tables, layer, batch):
    nsets, m, _ = ug.shape
    nchunks = m // S5_CHUNK
    kdim = S5_CHUNK * LANES
    kern = functools.partial(_s5_core_kernel, batch=batch, chunks_per_batch=nchunks // batch)
    table = lambda rows, cols: pl.BlockSpec((None, 1, rows, cols), lambda s: (layer, s, 0, 0))
    return pl.pallas_call(
        kern,
        grid=(nsets,),
        in_specs=[pl.BlockSpec((None, m, LANES), lambda s: (s, 0, 0)),
                  table(kdim, LANES),
                  table(LANES, LANES),
                  table(S5_STATE, kdim),
                  table(S5_STATE, kdim),
                  table(1, 2 * S5_SET_STATE),
                  table(1, kdim)],
        out_specs=pl.BlockSpec((None, m, LANES), lambda s: (s, 0, 0)),
        out_shape=jax.ShapeDtypeStruct((nsets, m, LANES), F32),
        scratch_shapes=[pltpu.VMEM((kdim, kdim), MXU_DTYPE),
                        pltpu.VMEM((kdim, 2 * S5_SET_STATE), MXU_DTYPE),
                        pltpu.VMEM((2 * S5_SET_STATE, kdim), MXU_DTYPE),
                        pltpu.VMEM((2 * S5_SET_STATE // LANES, nchunks, LANES), F32),
                        pltpu.VMEM((2 * S5_SET_STATE // LANES, nchunks, LANES), F32),
                        pltpu.VMEM((nchunks, kdim), F32)],
        compiler_params=_params("parallel"),
        name="s5_core",
    )(ug, *tables)


def _s5_out_kernel(h_ref, wo_ref, wg_ref, x_ref, g_ref, b_ref, o_ref):
    half = o_ref.shape[0] // ROW_SPLIT
    for r in range(ROW_SPLIT):
        rows = slice(r * half, (r + 1) * half)
        h = jnp.concatenate([h_ref[gs, rows, :] for gs in range(S5_SETS)], axis=1).astype(MXU_DTYPE)
        f = _mm(h, wo_ref[...]) * jax.nn.sigmoid(_mm(h, wg_ref[...]))
        o_ref[rows, :] = _layer_norm(DN_ALPHA * x_ref[rows, :] + f, g_ref[...], b_ref[...])


def _s5_out(hg, wo, wg, w_layer, x, ln, tm):
    m = x.shape[0]
    g, b, layer = ln
    return pl.pallas_call(
        _s5_out_kernel,
        grid=(m // tm,),
        in_specs=[pl.BlockSpec((S5_SETS, tm, LANES), lambda i: (0, i, 0)),
                  _layer_row((D_MODEL, D_MODEL), w_layer),
                  _layer_row((D_MODEL, D_MODEL), w_layer),
                  pl.BlockSpec((tm, D_MODEL), lambda i: (i, 0)),
                  _layer_row((1, D_MODEL), layer),
                  _layer_row((1, D_MODEL), layer)],
        out_specs=pl.BlockSpec((tm, D_MODEL), lambda i: (i, 0)),
        out_shape=jax.ShapeDtypeStruct(x.shape, F32),
        compiler_params=_params("parallel"),
        name="s5_out",
    )(hg, wo, wg, x, g, b)


def _s5_tables(lam_re, lam_im, log_dt, b_re, b_im, c_re, c_im, d_skip):
    lr = lam_re.astype(F32)
    li = lam_im.astype(F32)
    dt = jnp.exp(log_dt.astype(F32))[:, None]
    mag = jnp.exp(lr * dt)
    ar = mag * jnp.cos(li * dt)
    ai = mag * jnp.sin(li * dt)
    den = lr * lr + li * li
    zr = ((ar - 1.0) * lr + ai * li) / den
    zi = (ai * lr - (ar - 1.0) * li) / den
    br_ = b_re.astype(F32)
    bi_ = b_im.astype(F32)
    bbr = zr[..., None] * br_ - zi[..., None] * bi_
    bbi = zr[..., None] * bi_ + zi[..., None] * br_
    k = jnp.arange(S5_CHUNK + 1, dtype=F32)[:, None, None]
    pmag = jnp.exp(k * (lr * dt))
    pr = pmag * jnp.cos(k * (li * dt))
    pi = pmag * jnp.sin(k * (li * dt))

    bt_re = jnp.swapaxes(bbr, 1, 2)
    bt_im = jnp.swapaxes(bbi, 1, 2)
    shape_b = (S5_SETS, 1, S5_SET_GROUPS, S5_GROUP, 2 * S5_STATE)
    b_with_re = jnp.concatenate([bt_re, bt_im], axis=-1).reshape(shape_b)
    b_with_im = jnp.concatenate([-bt_im, bt_re], axis=-1).reshape(shape_b)
    def power_rows(t):
        t = jnp.concatenate([t, t], axis=-1).reshape(S5_CHUNK, S5_SETS, S5_SET_GROUPS, 1, 2 * S5_STATE)
        return t.transpose(1, 0, 2, 3, 4)
    lb = power_rows(pr[S5_CHUNK - 1::-1]) * b_with_re + power_rows(pi[S5_CHUNK - 1::-1]) * b_with_im
    lb = lb.reshape(S5_SETS, S5_CHUNK * LANES, 2 * S5_STATE)

    def rows_p(t):
        return t.astype(F32).reshape(S5_SETS, S5_SET_GROUPS, S5_GROUP, S5_STATE).transpose(0, 3, 1, 2)
    ct_re = rows_p(c_re)
    ct_im = rows_p(c_im)
    rc = jnp.concatenate([ct_re, -ct_im], axis=1).reshape(S5_SETS, 2 * S5_STATE, LANES)

    def power_lanes(t):
        return t.reshape(S5_CHUNK, S5_SETS, S5_SET_GROUPS, S5_STATE).transpose(1, 3, 0, 2)[..., None]
    prs = power_lanes(pr[1:])
    pis = power_lanes(pi[1:])
    ca_re = ct_re[:, :, None] * prs - ct_im[:, :, None] * pis
    ca_im = ct_re[:, :, None] * pis + ct_im[:, :, None] * prs
    ca_re = ca_re.reshape(S5_SETS, S5_STATE, S5_CHUNK * LANES)
    ca_im = ca_im.reshape(S5_SETS, S5_STATE, S5_CHUNK * LANES)

    a16 = jnp.concatenate([pr[S5_CHUNK].reshape(S5_SETS, 1, S5_SET_STATE),
                           pi[S5_CHUNK].reshape(S5_SETS, 1, S5_SET_STATE)], axis=2)
    drow = jnp.tile(d_skip.astype(F32).reshape(S5_SETS, 1, LANES), (1, 1, S5_CHUNK))
    return lb, rc, ca_re, ca_im, a16, drow


def _s5_layer(x, batch, w_in, tables, layer, w_out, w_gate, ln):
    ug = _s5_in(x, w_in, layer, tm=512)
    hg = _s5_core(ug, tables, layer, batch)
    return _s5_out(hg, w_out, w_gate, layer, x, ln, tm=512)


def _split_hi_lo(v):
    hi = v.astype(MXU_DTYPE)
    lo = (v - hi.astype(F32)).astype(MXU_DTYPE)
    return jnp.concatenate([hi, lo], axis=1)


def _ssd_core_kernel(z_ref, xs_ref, bc_ref, xin_ref, wdt_ref, cwx_ref, cwb_ref, cbx_ref, cbb_ref,
                     dtb_ref, a_ref, dsk_ref, nw_ref, e2_ref, o_ref,
                     xbuf_ref, bbuf_ref, st_ref, yd_ref):
    q = SSD_CHUNK
    pad = 8

    @pl.when(pl.program_id(1) == 0)
    def _():
        xbuf_ref[...] = jnp.zeros_like(xbuf_ref)
        bbuf_ref[...] = jnp.zeros_like(bbuf_ref)
        st_ref[...] = jnp.zeros_like(st_ref)

    def conv(in_ref, tail_ref, w_ref, b_ref):
        cur = in_ref[...]
        tail = tail_ref[...]
        row8 = lax.broadcasted_iota(jnp.int32, tail.shape, 0)
        acc = b_ref[...] + w_ref[SSD_CONV - 1:SSD_CONV, :] * cur
        for k in range(1, SSD_CONV):
            back = pltpu.roll(cur, k, axis=0)
            head = jnp.where(row8 < k, pltpu.roll(tail, k, axis=0), back[0:pad])
            back = jnp.concatenate([head, back[pad:]], axis=0)
            acc = acc + w_ref[SSD_CONV - 1 - k:SSD_CONV - k, :] * back
        tail_ref[...] = cur[q - pad:, :]
        return _silu(acc)

    xs = conv(xs_ref, xbuf_ref, cwx_ref, cbx_ref)
    bc = conv(bc_ref, bbuf_ref, cwb_ref, cbb_ref)

    dtr = _mm(xin_ref[...].astype(MXU_DTYPE), wdt_ref[...]) + dtb_ref[...]
    dt = jnp.maximum(dtr, 0.0) + jnp.log1p(jnp.exp(-jnp.abs(dtr)))
    adt = dt * a_ref[...]
    ri = lax.broadcasted_iota(jnp.int32, (q, q), 0)
    ci = lax.broadcasted_iota(jnp.int32, (q, q), 1)
    causal = ri >= ci
    tri = causal.astype(F32)
    cs = jnp.dot(tri, adt, preferred_element_type=F32, precision=HI)
    eye = (ri == ci).astype(F32)
    cs2 = cs * LOG2_E
    r_rows = _mm_nt(eye, cs2, precision=HI) - jnp.log2(_mm_nt(eye, dt, precision=HI))
    tot = cs[q - 1:q, :]
    w_col = dt * jnp.exp(tot - cs)
    din = jnp.exp(cs)
    w_x = _mm(_split_hi_lo(w_col), e2_ref[...])
    din_x = _mm(_split_hi_lo(din), e2_ref[...])

    lane = lax.broadcasted_iota(jnp.int32, (q, LANES), 1)
    low = lane < SSD_HEADDIM

    xw = (xs * w_x).astype(MXU_DTYPE)
    xsb = xs.astype(MXU_DTYPE)
    for g in range(SSD_GROUPS):
        bg = bc[:, g * SSD_STATE:(g + 1) * SSD_STATE].astype(MXU_DTYPE)
        cg = bc[:, SSD_GROUPS * SSD_STATE + g * SSD_STATE:
                SSD_GROUPS * SSD_STATE + (g + 1) * SSD_STATE].astype(MXU_DTYPE)
        cb = _mm_nt(cg, bg)
        for pair in range(SSD_HEADS_PER_GROUP // 2):
            h0 = g * SSD_HEADS_PER_GROUP + 2 * pair
            ms = []
            for h in (h0, h0 + 1):
                seg = cs2[:, h:h + 1] - r_rows[h:h + 1, :]
                ms.append((cb * jnp.exp2(jnp.where(causal, seg, -jnp.inf))).astype(MXU_DTYPE))
            lhs = jnp.concatenate(ms, axis=1)
            xp = xsb[:, h0 * SSD_HEADDIM:(h0 + 2) * SSD_HEADDIM]
            zero = jnp.zeros_like(xp)
            rhs = jnp.concatenate([jnp.where(low, xp, zero), jnp.where(low, zero, xp)], axis=0)
            yd_ref[:, h0 * SSD_HEADDIM:(h0 + 2) * SSD_HEADDIM] = _mm(lhs, rhs)
        gsl = slice(g * SSD_GROUP_DIM, (g + 1) * SSD_GROUP_DIM)
        st = st_ref[g]
        y_off = _mm(cg, st.astype(MXU_DTYPE)) * din_x[:, gsl]
        yd_ref[:, gsl] = yd_ref[:, gsl] + y_off
        st_ref[g] = din_x[q - 1:q, gsl] * st + _mm_tn(bg, xw[:, gsl])

    y = yd_ref[...] + dsk_ref[...] * xs
    y = y * z_ref[...]
    for g in range(SSD_GROUPS):
        gsl = slice(g * SSD_GROUP_DIM, (g + 1) * SSD_GROUP_DIM)
        yg = y[:, gsl]
        yg = yg * lax.rsqrt(jnp.mean(yg * yg, axis=-1, keepdims=True) + LN_EPS)
        o_ref[:, gsl] = (yg * nw_ref[:, gsl]).astype(o_ref.dtype)


def _ssd_core(proj, x, w_dt, batch, conv_w, conv_b, dt_bias, a_log, d_skip, norm_w):
    m = proj.shape[0]
    q = SSD_CHUNK
    ncb = m // batch // q
    di = SSD_D_INNER
    wdt = jnp.pad(w_dt.astype(MXU_DTYPE), ((0, 0), (0, LANES - SSD_HEADS)))
    cw = conv_w.astype(F32)
    cbias = conv_b.astype(F32)[None, :]
    padh = LANES - SSD_HEADS
    dtb = jnp.pad(dt_bias.astype(F32), (0, padh))[None, :]
    a = jnp.pad(-jnp.exp(a_log.astype(F32)), (0, padh))[None, :]
    dsk = jnp.repeat(d_skip.astype(F32), SSD_HEADDIM)[None, :]
    nw = norm_w.astype(F32)[None, :]
    expand = (np.arange(LANES)[:, None] == (np.arange(di) // SSD_HEADDIM)[None, :]).astype(np.float32)
    e2 = jnp.asarray(np.concatenate([expand, expand], axis=0), dtype=MXU_DTYPE)

    row = lambda b, c: b * ncb + c
    full = _resident
    return pl.pallas_call(
        _ssd_core_kernel,
        grid=(batch, ncb),
        in_specs=[pl.BlockSpec((q, di), lambda b, c: (row(b, c), 0)),
                  pl.BlockSpec((q, di), lambda b, c: (row(b, c), 1)),
                  pl.BlockSpec((q, SSD_BC), lambda b, c: (row(b, c), 2 * di // SSD_BC)),
                  pl.BlockSpec((q, D_MODEL), lambda b, c: (row(b, c), 0)),
                  full((D_MODEL, LANES)),
                  full((SSD_CONV, di)), full((SSD_CONV, SSD_BC)), full((1, di)), full((1, SSD_BC)),
                  full((1, LANES)), full((1, LANES)), full((1, di)), full((1, di)),
                  full((2 * LANES, di))],
        out_specs=pl.BlockSpec((q, di), lambda b, c: (row(b, c), 0)),
        out_shape=jax.ShapeDtypeStruct((m, di), MXU_DTYPE),
        scratch_shapes=[pltpu.VMEM((8, di), F32), pltpu.VMEM((8, SSD_BC), F32),
                        pltpu.VMEM((SSD_GROUPS, SSD_STATE, SSD_GROUP_DIM), F32),
                        pltpu.VMEM((q, di), F32)],
        compiler_params=_params("parallel", "arbitrary"),
        name="ssd_core",
    )(proj, proj, proj, x, wdt, cw[:, :di], cw[:, di:], cbias[:, :di], cbias[:, di:], dtb, a, dsk, nw, e2)


def _ssd_layer(x, batch, w_in_stack, layer, conv_w, conv_b, dt_bias, a_log, d_skip, norm_w, w_out, ln):
    n_main = SSD_D_INNER + SSD_CONV_DIM
    proj = _proj(x, w_in_stack.astype(MXU_DTYPE), layer, n_main, tm=min(1024, x.shape[0]), tn=1024,
                 out_dtype=F32, silu_cols=SSD_D_INNER)
    w_dt = w_in_stack[layer, :, n_main:]
    y = _ssd_core(proj, x, w_dt, batch, conv_w, conv_b, dt_bias, a_log, d_skip, norm_w)
    return _out_ln(y, w_out.astype(MXU_DTYPE), x, ln, tm=512)


def _ret_core_kernel(cd_ref, q_ref, k_ref, v_ref, g_ref, cos_ref, sin_ref, dmat_ref, xi_ref, zeta_ref,
                     o_ref, st_ref):
    @pl.when(pl.program_id(1) == 0)
    def _():
        st_ref[...] = jnp.zeros_like(st_ref)

    cos = cos_ref[...]
    sin = sin_ref[...]
    half = RET_DK // 2

    def rotate(t):
        t1 = t[:, :half]
        t2 = t[:, half:]
        return jnp.concatenate([t1 * cos - t2 * sin, t1 * sin + t2 * cos], axis=1)

    for h in range(RET_HEADS):
        qh = rotate(q_ref[:, h * RET_DK:(h + 1) * RET_DK].astype(F32))
        kh = rotate(k_ref[:, h * RET_DK:(h + 1) * RET_DK].astype(F32) * (RET_DK ** -0.5))
        vh = v_ref[:, h * RET_DV:(h + 1) * RET_DV].astype(MXU_DTYPE)
        qb = qh.astype(MXU_DTYPE)
        scores = _mm_nt(qb, kh.astype(MXU_DTYPE)) * dmat_ref[h]
        inner = _mm(scores.astype(MXU_DTYPE), vh)
        st = st_ref[h]
        xi = jnp.concatenate([xi_ref[h]] * (RET_DV // LANES), axis=1)
        cross = _mm(qb, st.astype(MXU_DTYPE)) * xi
        kz = kh * jnp.concatenate([zeta_ref[h]] * (RET_DK // LANES), axis=1)
        st_ref[h] = cd_ref[h] * st + _mm_tn(kz.astype(MXU_DTYPE), vh)
        out = inner + cross
        mu = jnp.mean(out, axis=-1, keepdims=True)
        d = out - mu
        var = jnp.mean(d * d, axis=-1, keepdims=True)
        out = d * lax.rsqrt(var + LN_EPS)
        gate = _silu(g_ref[:, h * RET_DV:(h + 1) * RET_DV].astype(F32))
        o_ref[:, h * RET_DV:(h + 1) * RET_DV] = (gate * out).astype(o_ref.dtype)


def _ret_core(proj, batch):
    m = proj.shape[0]
    seq = m // batch
    q = RET_CHUNK
    ncb = seq // q
    theta = (1.0 / (RET_ROPE_BASE ** np.linspace(0.0, 1.0, RET_DK // 2))).astype(np.float32)
    ang = (np.arange(seq, dtype=np.float32)[:, None] * theta[None, :]).astype(np.float64)
    cos, sin = np.cos(ang).astype(np.float32), np.sin(ang).astype(np.float32)
    lg = np.log1p(-np.exp2(-5.0 - np.arange(RET_HEADS)))
    pos = np.arange(q, dtype=np.float64)
    diff = pos[:, None] - pos[None, :]
    dmat = np.where(diff >= 0, np.exp(np.maximum(diff, 0.0)[None] * lg[:, None, None]), 0.0).astype(np.float32)
    xi = np.exp((pos[None, :] + 1.0) * lg[:, None])
    zeta = np.exp((q - 1.0 - pos[None, :]) * lg[:, None])
    xi_b = np.broadcast_to(xi[:, :, None], (RET_HEADS, q, LANES)).astype(np.float32)
    zeta_b = np.broadcast_to(zeta[:, :, None], (RET_HEADS, q, LANES)).astype(np.float32)
    chunk_decay = np.exp(q * lg).astype(np.float32)

    row = lambda b, c: b * ncb + c
    d = D_MODEL
    return pl.pallas_call(
        _ret_core_kernel,
        grid=(batch, ncb),
        in_specs=[pl.BlockSpec(memory_space=pltpu.SMEM),
                  pl.BlockSpec((q, d), lambda b, c: (row(b, c), 0)),
                  pl.BlockSpec((q, d), lambda b, c: (row(b, c), 1)),
                  pl.BlockSpec((q, 2 * d), lambda b, c: (row(b, c), 1)),
                  pl.BlockSpec((q, 2 * d), lambda b, c: (row(b, c), 2)),
                  pl.BlockSpec((q, RET_DK // 2), lambda b, c: (c, 0)),
                  pl.BlockSpec((q, RET_DK // 2), lambda b, c: (c, 0)),
                  _resident((RET_HEADS, q, q)),
                  _resident((RET_HEADS, q, LANES)),
                  _resident((RET_HEADS, q, LANES))],
        out_specs=pl.BlockSpec((q, 2 * d), lambda b, c: (row(b, c), 0)),
        out_shape=jax.ShapeDtypeStruct((m, 2 * d), MXU_DTYPE),
        scratch_shapes=[pltpu.VMEM((RET_HEADS, RET_DK, RET_DV), F32)],
        compiler_params=_params("parallel", "arbitrary"),
        name="ret_core",
    )(chunk_decay, proj, proj, proj, proj, cos, sin, dmat, xi_b, zeta_b)


def _ret_layer(x, batch, w_in_stack, layer, w_out, ln):
    proj = _proj(x, w_in_stack, layer, RET_IN_DIM, tm=min(1024, x.shape[0]), tn=1024, out_dtype=MXU_DTYPE)
    y = _ret_core(proj, batch)
    return _out_ln(y, w_out.astype(MXU_DTYPE), x, ln, tm=512)


def kernel(x, ln1_g, ln1_b, ln2_g, ln2_b, mlp_w1, mlp_w2, s5_w_in, s5_lam_re, s5_lam_im, s5_log_dt, s5_b_re, s5_b_im, s5_c_re, s5_c_im, s5_d, s5_w_out, s5_w_gate, ssd_w_in, ssd_conv_w, ssd_conv_b, ssd_dt_bias, ssd_a_log, ssd_d, ssd_norm_w, ssd_w_out, ret_w_in, ret_w_out):
    batch, seq, d = x.shape
    h = x.reshape(batch * seq, d)
    ln1 = (ln1_g[:, None, :], ln1_b[:, None, :])
    ln2 = (ln2_g[:, None, :], ln2_b[:, None, :])
    s5_tables = jax.vmap(_s5_tables)(s5_lam_re, s5_lam_im, s5_log_dt, s5_b_re, s5_b_im, s5_c_re, s5_c_im, s5_d)
    s5_w_out_b = s5_w_out.astype(MXU_DTYPE)
    s5_w_gate_b = s5_w_gate.astype(MXU_DTYPE)
    for i in range(DEPTH):
        kind = i % 3
        j = i // 3
        if kind == 0:
            h = _s5_layer(h, batch, s5_w_in, s5_tables, j, s5_w_out_b, s5_w_gate_b, ln1 + (i,))
        elif kind == 1:
            h = _ssd_layer(h, batch, ssd_w_in, j, ssd_conv_w[j], ssd_conv_b[j], ssd_dt_bias[j], ssd_a_log[j],
                           ssd_d[j], ssd_norm_w[j], ssd_w_out[j], ln1 + (i,))
        else:
            h = _ret_layer(h, batch, ret_w_in, j, ret_w_out[j], ln1 + (i,))
        h = _mlp(h, mlp_w1, mlp_w2, i, ln2 + (i,), tm=min(1024, h.shape[0]), tf=512)
    return h.reshape(batch, seq, d)
```

```python
import functools
import math

import jax
import jax.numpy as jnp
import numpy as np
from jax import lax
from jax.experimental import pallas as pl
from jax.experimental.pallas import tpu as pltpu

F32 = jnp.float32
MXU_DTYPE = jnp.bfloat16
HI = lax.Precision.HIGHEST
LOG2_E = 1.0 / math.log(2.0)

LANES = 128
VMEM_PHYSICAL_BYTES = 64 * 1024 * 1024
VMEM_LIMIT_BYTES = VMEM_PHYSICAL_BYTES - 6 * 1024 * 1024

D_MODEL = 2048
DEPTH = 4
DN_ALPHA = (2 * DEPTH) ** 0.25
LN_EPS = 1e-5
D_FF = 4 * D_MODEL
MLP_OUT_SLABS = 4
ROW_SPLIT = 2

S5_GROUP = 16
S5_GROUPS = D_MODEL // S5_GROUP
S5_STATE = 64
S5_CHUNK = 16
S5_SETS = D_MODEL // LANES
S5_SET_GROUPS = LANES // S5_GROUP
S5_SET_STATE = S5_SET_GROUPS * S5_STATE
S5_SCAN_UNROLL = 8
S5_TOEPLITZ_GROUP = 4

SSD_D_INNER = 2 * D_MODEL
SSD_HEADDIM = 64
SSD_HEADS = SSD_D_INNER // SSD_HEADDIM
SSD_GROUPS = 8
SSD_STATE = 128
SSD_CONV = 4
SSD_CHUNK = 128
SSD_BC = 2 * SSD_GROUPS * SSD_STATE
SSD_CONV_DIM = SSD_D_INNER + SSD_BC
SSD_GROUP_DIM = SSD_D_INNER // SSD_GROUPS
SSD_HEADS_PER_GROUP = SSD_HEADS // SSD_GROUPS

RET_HEADS = 8
RET_DK = D_MODEL // RET_HEADS
RET_DV = 2 * D_MODEL // RET_HEADS
RET_CHUNK = 128
RET_IN_DIM = 6 * D_MODEL
RET_ROPE_BASE = 10000.0


def _params(*semantics):
    return pltpu.CompilerParams(dimension_semantics=semantics, vmem_limit_bytes=VMEM_LIMIT_BYTES)


def _layer_norm(v, g, b):
    mu = jnp.mean(v, axis=-1, keepdims=True)
    d = v - mu
    var = jnp.mean(d * d, axis=-1, keepdims=True)
    return d * lax.rsqrt(var + LN_EPS) * g + b


def _silu(v):
    h = 0.5 * v
    return h + h * jnp.tanh(h)


def _mm(a, b):
    return jnp.dot(a, b, preferred_element_type=F32)


def _mm_nt(a, b, precision=None):
    return lax.dot_general(a, b, (((1,), (1,)), ((), ())), preferred_element_type=F32, precision=precision)


def _mm_tn(a, b):
    return lax.dot_general(a, b, (((0,), (0,)), ((), ())), preferred_element_type=F32)


def _resident(shape):
    return pl.BlockSpec(shape, lambda *_: (0,) * len(shape), pipeline_mode=pl.Buffered(1))


def _layer_row(shape, layer):
    return pl.BlockSpec((None,) + shape, lambda *_: (layer, 0, 0), pipeline_mode=pl.Buffered(1))


def _proj_kernel(x_ref, w_ref, o_ref, *, act_tiles, w_transposed):
    j = pl.program_id(1)

    def product():
        xb = x_ref[...].astype(MXU_DTYPE)
        wb = w_ref[...].astype(MXU_DTYPE)
        return _mm_nt(xb, wb) if w_transposed else _mm(xb, wb)

    if act_tiles == 0:
        o_ref[...] = product().astype(o_ref.dtype)
    else:
        @pl.when(j < act_tiles)
        def _():
            o_ref[...] = _silu(product()).astype(o_ref.dtype)

        @pl.when(j >= act_tiles)
        def _():
            o_ref[...] = product().astype(o_ref.dtype)


def _proj(x, w_stack, layer, n, tm, tn, out_dtype, silu_cols=0, w_transposed=False):
    m, k = x.shape
    assert silu_cols % tn == 0
    w_spec = (pl.BlockSpec((None, tn, k), lambda i, j: (layer, j, 0)) if w_transposed
              else pl.BlockSpec((None, k, tn), lambda i, j: (layer, 0, j)))
    return pl.pallas_call(
        functools.partial(_proj_kernel, act_tiles=silu_cols // tn, w_transposed=w_transposed),
        grid=(m // tm, n // tn),
        in_specs=[pl.BlockSpec((tm, k), lambda i, j: (i, 0)), w_spec],
        out_specs=pl.BlockSpec((tm, tn), lambda i, j: (i, j)),
        out_shape=jax.ShapeDtypeStruct((m, n), out_dtype),
        compiler_params=_params("parallel", "arbitrary"),
        name="proj",
    )(x, w_stack)


def _out_ln_kernel(y_ref, w_ref, x_ref, g_ref, b_ref, o_ref):
    half = o_ref.shape[0] // ROW_SPLIT
    for r in range(ROW_SPLIT):
        rows = slice(r * half, (r + 1) * half)
        f = _mm(y_ref[rows, :], w_ref[...])
        o_ref[rows, :] = _layer_norm(DN_ALPHA * x_ref[rows, :] + f, g_ref[...], b_ref[...])


def _out_ln(y, w, x, ln, tm):
    m, k = y.shape
    d = w.shape[1]
    g, b, layer = ln
    return pl.pallas_call(
        _out_ln_kernel,
        grid=(m // tm,),
        in_specs=[pl.BlockSpec((tm, k), lambda i: (i, 0)),
                  _resident((k, d)),
                  pl.BlockSpec((tm, d), lambda i: (i, 0)),
                  _layer_row((1, d), layer),
                  _layer_row((1, d), layer)],
        out_specs=pl.BlockSpec((tm, d), lambda i: (i, 0)),
        out_shape=jax.ShapeDtypeStruct((m, d), F32),
        compiler_params=_params("parallel"),
        name="out_ln",
    )(y, w, x, g, b)


def _mlp_kernel(x_ref, w1_ref, w2_ref, g_ref, b_ref, o_ref, *, nf):
    j = pl.program_id(1)

    @pl.when(j == 0)
    def _():
        o_ref[...] = DN_ALPHA * x_ref[...]

    h = _mm(x_ref[...].astype(MXU_DTYPE), w1_ref[...].astype(MXU_DTYPE))
    h = jnp.square(jnp.maximum(h, 0.0)).astype(MXU_DTYPE)
    slab = o_ref.shape[1] // MLP_OUT_SLABS
    for s in range(MLP_OUT_SLABS):
        cols = slice(s * slab, (s + 1) * slab)
        o_ref[:, cols] += _mm(h, w2_ref[:, cols].astype(MXU_DTYPE))

    @pl.when(j == nf - 1)
    def _():
        o_ref[...] = _layer_norm(o_ref[...], g_ref[...], b_ref[...])


def _mlp(x, w1_stack, w2_stack, layer, ln, tm, tf):
    m, d = x.shape
    f = w1_stack.shape[2]
    nf = f // tf
    g, b, ln_layer = ln
    return pl.pallas_call(
        functools.partial(_mlp_kernel, nf=nf),
        grid=(m // tm, nf),
        in_specs=[pl.BlockSpec((tm, d), lambda i, j: (i, 0)),
                  pl.BlockSpec((None, d, tf), lambda i, j: (layer, 0, j)),
                  pl.BlockSpec((None, tf, d), lambda i, j: (layer, j, 0)),
                  _layer_row((1, d), ln_layer),
                  _layer_row((1, d), ln_layer)],
        out_specs=pl.BlockSpec((tm, d), lambda i, j: (i, 0)),
        out_shape=jax.ShapeDtypeStruct((m, d), F32),
        compiler_params=_params("parallel", "arbitrary"),
        name="mlp",
    )(x, w1_stack, w2_stack, g, b)


def _s5_in_kernel(x_ref, w_ref, o_ref, wb_ref):
    @pl.when(pl.program_id(0) == 0)
    def _():
        wb_ref[...] = w_ref[...].astype(MXU_DTYPE)

    u = _mm(x_ref[...].astype(MXU_DTYPE), wb_ref[...])
    for gs in range(S5_SETS):
        o_ref[gs] = u[:, gs * LANES:(gs + 1) * LANES]


def _s5_in(x, w, layer, tm):
    m = x.shape[0]
    return pl.pallas_call(
        _s5_in_kernel,
        grid=(m // tm,),
        in_specs=[pl.BlockSpec((tm, D_MODEL), lambda i: (i, 0)),
                  _layer_row((D_MODEL, D_MODEL), layer)],
        out_specs=pl.BlockSpec((S5_SETS, tm, LANES), lambda i: (0, i, 0)),
        out_shape=jax.ShapeDtypeStruct((S5_SETS, m, LANES), F32),
        scratch_shapes=[pltpu.VMEM((D_MODEL, D_MODEL), MXU_DTYPE)],
        compiler_params=_params("arbitrary"),
        name="s5_in",
    )(x, w)


def _s5_core_kernel(u_ref, lb_ref, rc_ref, cre_ref, cim_ref, a16_ref, d_ref, o_ref,
                    wt_ref, win_ref, wout_ref, v_ref, sp_ref, yin_ref, *, batch, chunks_per_batch):
    nst = S5_SET_STATE
    kdim = S5_CHUNK * LANES
    lb = lb_ref[0]

    rc = rc_ref[0]
    lb_hi = lb.astype(MXU_DTYPE)
    lb_lo = (lb - lb_hi.astype(F32)).astype(MXU_DTYPE)
    rc_hi = rc.astype(MXU_DTYPE)
    rc_lo = (rc - rc_hi.astype(F32)).astype(MXU_DTYPE)
    drev = _mm(jnp.concatenate([lb_hi, lb_lo], axis=1), jnp.concatenate([rc_hi, rc_hi], axis=0))
    drev = drev + _mm(lb_hi, rc_lo)
    row = lax.broadcasted_iota(jnp.int32, (kdim, LANES), 0)
    lane = lax.broadcasted_iota(jnp.int32, (kdim, LANES), 1)
    row_group = (row % LANES) // S5_GROUP
    drev = jnp.where(row_group == lane // S5_GROUP, drev, 0.0).astype(MXU_DTYPE)
    for t in range(S5_CHUNK):
        used = (t + 1) * LANES
        group_rows = (t // S5_TOEPLITZ_GROUP + 1) * S5_TOEPLITZ_GROUP * LANES
        wt_ref[0:used, t * LANES:(t + 1) * LANES] = drev[kdim - used:, :]
        if used < group_rows:
            wt_ref[used:group_rows, t * LANES:(t + 1) * LANES] = jnp.zeros((group_rows - used, LANES), MXU_DTYPE)

    swapped = pltpu.roll(lb, S5_STATE, axis=1)
    low = lane < S5_STATE
    re_dup = jnp.where(low, lb, swapped)
    im_dup = jnp.where(low, swapped, lb)
    for v in range(nst // LANES):
        sel = row_group == 2 * v + (lane >= S5_STATE).astype(jnp.int32)
        win_ref[:, v * LANES:(v + 1) * LANES] = jnp.where(sel, re_dup, 0.0).astype(MXU_DTYPE)
        win_ref[:, nst + v * LANES:nst + (v + 1) * LANES] = jnp.where(sel, im_dup, 0.0).astype(MXU_DTYPE)

    lane_group = (lax.broadcasted_iota(jnp.int32, (S5_STATE, kdim), 1) % LANES) // S5_GROUP
    for ri, part in enumerate((cre_ref[0], -cim_ref[0])):
        for g in range(S5_SET_GROUPS):
            r0 = ri * nst + g * S5_STATE
            wout_ref[r0:r0 + S5_STATE, :] = jnp.where(lane_group == g, part, 0.0).astype(MXU_DTYPE)

    nchunks = batch * chunks_per_batch
    u = jnp.concatenate([u_ref[pl.ds(t, nchunks, stride=S5_CHUNK), :] for t in range(S5_CHUNK)], axis=1)
    xb = u.astype(MXU_DTYPE)
    vin = _mm(xb, win_ref[...])
    ntile = nst // LANES
    for k in range(2 * ntile):
        v_ref[k] = vin[:, k * LANES:(k + 1) * LANES]
    gcols = S5_TOEPLITZ_GROUP * LANES
    for j in range(S5_CHUNK // S5_TOEPLITZ_GROUP):
        cols = slice(j * gcols, (j + 1) * gcols)
        krows = (j + 1) * gcols
        yin_ref[:, cols] = _mm(xb[:, :krows], wt_ref[0:krows, cols]) + d_ref[0][:, cols] * u[:, cols]

    a_re = [a16_ref[0, :, k * LANES:(k + 1) * LANES] for k in range(ntile)]
    a_im = [a16_ref[0, :, nst + k * LANES:nst + (k + 1) * LANES] for k in range(ntile)]

    def step(c, carry):
        rows = pl.ds(c, batch, stride=chunks_per_batch)
        new = []
        for k in range(ntile):
            s_re, s_im = carry[k], carry[ntile + k]
            sp_ref[k, rows, :] = s_re
            sp_ref[ntile + k, rows, :] = s_im
            new.append((a_re[k] * s_re - a_im[k] * s_im + v_ref[k, rows, :],
                        a_re[k] * s_im + a_im[k] * s_re + v_ref[ntile + k, rows, :]))
        return tuple(n[0] for n in new) + tuple(n[1] for n in new)

    zero = jnp.zeros((batch, LANES), F32)
    lax.fori_loop(0, chunks_per_batch, step, (zero,) * (2 * ntile), unroll=S5_SCAN_UNROLL)

    sp = jnp.concatenate([sp_ref[k] for k in range(2 * ntile)], axis=1)
    h = jax.nn.gelu(yin_ref[...] + _mm(sp.astype(MXU_DTYPE), wout_ref[...]))
    for t in range(S5_CHUNK):
        o_ref[pl.ds(t, nchunks, stride=S5_CHUNK), :] = h[:, t * LANES:(t + 1) * LANES]


def _s5_core(ug, tables, layer, batch):
    nsets, m, _ = ug.shape
    nchunks = m // S5_CHUNK
    kdim = S5_CHUNK * LANES
    kern = functools.partial(_s5_core_kernel, batch=batch, chunks_per_batch=nchunks // batch)
    table = lambda rows, cols: pl.BlockSpec((None, 1, rows, cols), lambda s: (layer, s, 0, 0))
    return pl.pallas_call(
        kern,
        grid=(nsets,),
        in_specs=[pl.BlockSpec((None, m, LANES), lambda s: (s, 0, 0)),
                  table(kdim, LANES),
                  table(LANES, LANES),
                  table(S5_STATE, kdim),
                  table(S5_STATE, kdim),
                  table(1, 2 * S5_SET_STATE),
                  table(1, kdim)],
        out_specs=pl.BlockSpec((None, m, LANES), lambda s: (s, 0, 0)),
        out_shape=jax.ShapeDtypeStruct((nsets, m, LANES), F32),
        scratch_shapes=[pltpu.VMEM((kdim, kdim), MXU_DTYPE),
                        pltpu.VMEM((kdim, 2 * S5_SET_STATE), MXU_DTYPE),
                        pltpu.VMEM((2 * S5_SET_STATE, kdim), MXU_DTYPE),
                        pltpu.VMEM((2 * S5_SET_STATE // LANES, nchunks, LANES), F32),
                        pltpu.VMEM((2 * S5_SET_STATE // LANES, nchunks, LANES), F32),
                        pltpu.VMEM((nchunks, kdim), F32)],
        compiler_params=_params("parallel"),
        name="s5_core",
    )(ug, *tables)


def _s5_out_kernel(h_ref, wo_ref, wg_ref, x_ref, g_ref, b_ref, o_ref):
    half = o_ref.shape[0] // ROW_SPLIT
    for r in range(ROW_SPLIT):
        rows = slice(r * half, (r + 1) * half)
        h = jnp.concatenate([h_ref[gs, rows, :] for gs in range(S5_SETS)], axis=1).astype(MXU_DTYPE)
        f = _mm(h, wo_ref[...]) * jax.nn.sigmoid(_mm(h, wg_ref[...]))
        o_ref[rows, :] = _layer_norm(DN_ALPHA * x_ref[rows, :] + f, g_ref[...], b_ref[...])


def _s5_out(hg, wo, wg, w_layer, x, ln, tm):
    m = x.shape[0]
    g, b, layer = ln
    return pl.pallas_call(
        _s5_out_kernel,
        grid=(m // tm,),
        in_specs=[pl.BlockSpec((S5_SETS, tm, LANES), lambda i: (0, i, 0)),
                  _layer_row((D_MODEL, D_MODEL), w_layer),
                  _layer_row((D_MODEL, D_MODEL), w_layer),
                  pl.BlockSpec((tm, D_MODEL), lambda i: (i, 0)),
                  _layer_row((1, D_MODEL), layer),
                  _layer_row((1, D_MODEL), layer)],
        out_specs=pl.BlockSpec((tm, D_MODEL), lambda i: (i, 0)),
        out_shape=jax.ShapeDtypeStruct(x.shape, F32),
        compiler_params=_params("parallel"),
        name="s5_out",
    )(hg, wo, wg, x, g, b)


def _s5_tables(lam_re, lam_im, log_dt, b_re, b_im, c_re, c_im, d_skip):
    lr = lam_re.astype(F32)
    li = lam_im.astype(F32)
    dt = jnp.exp(log_dt.astype(F32))[:, None]
    mag = jnp.exp(lr * dt)
    ar = mag * jnp.cos(li * dt)
    ai = mag * jnp.sin(li * dt)
    den = lr * lr + li * li
    zr = ((ar - 1.0) * lr + ai * li) / den
    zi = (ai * lr - (ar - 1.0) * li) / den
    br_ = b_re.astype(F32)
    bi_ = b_im.astype(F32)
    bbr = zr[..., None] * br_ - zi[..., None] * bi_
    bbi = zr[..., None] * bi_ + zi[..., None] * br_
    k = jnp.arange(S5_CHUNK + 1, dtype=F32)[:, None, None]
    pmag = jnp.exp(k * (lr * dt))
    pr = pmag * jnp.cos(k * (li * dt))
    pi = pmag * jnp.sin(k * (li * dt))

    bt_re = jnp.swapaxes(bbr, 1, 2)
    bt_im = jnp.swapaxes(bbi, 1, 2)
    shape_b = (S5_SETS, 1, S5_SET_GROUPS, S5_GROUP, 2 * S5_STATE)
    b_with_re = jnp.concatenate([bt_re, bt_im], axis=-1).reshape(shape_b)
    b_with_im = jnp.concatenate([-bt_im, bt_re], axis=-1).reshape(shape_b)
    def power_rows(t):
        t = jnp.concatenate([t, t], axis=-1).reshape(S5_CHUNK, S5_SETS, S5_SET_GROUPS, 1, 2 * S5_STATE)
        return t.transpose(1, 0, 2, 3, 4)
    lb = power_rows(pr[S5_CHUNK - 1::-1]) * b_with_re + power_rows(pi[S5_CHUNK - 1::-1]) * b_with_im
    lb = lb.reshape(S5_SETS, S5_CHUNK * LANES, 2 * S5_STATE)

    def rows_p(t):
        return t.astype(F32).reshape(S5_SETS, S5_SET_GROUPS, S5_GROUP, S5_STATE).transpose(0, 3, 1, 2)
    ct_re = rows_p(c_re)
    ct_im = rows_p(c_im)
    rc = jnp.concatenate([ct_re, -ct_im], axis=1).reshape(S5_SETS, 2 * S5_STATE, LANES)

    def power_lanes(t):
        return t.reshape(S5_CHUNK, S5_SETS, S5_SET_GROUPS, S5_STATE).transpose(1, 3, 0, 2)[..., None]
    prs = power_lanes(pr[1:])
    pis = power_lanes(pi[1:])
    ca_re = ct_re[:, :, None] * prs - ct_im[:, :, None] * pis
    ca_im = ct_re[:, :, None] * pis + ct_im[:, :, None] * prs
    ca_re = ca_re.reshape(S5_SETS, S5_STATE, S5_CHUNK * LANES)
    ca_im = ca_im.reshape(S5_SETS, S5_STATE, S5_CHUNK * LANES)

    a16 = jnp.concatenate([pr[S5_CHUNK].reshape(S5_SETS, 1, S5_SET_STATE),
                           pi[S5_CHUNK].reshape(S5_SETS, 1, S5_SET_STATE)], axis=2)
    drow = jnp.tile(d_skip.astype(F32).reshape(S5_SETS, 1, LANES), (1, 1, S5_CHUNK))
    return lb, rc, ca_re, ca_im, a16, drow


def _s5_layer(x, batch, w_in, tables, layer, w_out, w_gate, ln):
    ug = _s5_in(x, w_in, layer, tm=512)
    hg = _s5_core(ug, tables, layer, batch)
    return _s5_out(hg, w_out, w_gate, layer, x, ln, tm=512)


def _split_hi_lo(v):
    hi = v.astype(MXU_DTYPE)
    lo = (v - hi.astype(F32)).astype(MXU_DTYPE)
    return jnp.concatenate([hi, lo], axis=1)


def _ssd_core_kernel(z_ref, xs_ref, bc_ref, xin_ref, wdt_ref, cwx_ref, cwb_ref, cbx_ref, cbb_ref,
                     dtb_ref, a_ref, dsk_ref, nw_ref, e2_ref, o_ref,
                     xbuf_ref, bbuf_ref, st_ref, yd_ref):
    q = SSD_CHUNK
    pad = 8

    @pl.when(pl.program_id(1) == 0)
    def _():
        xbuf_ref[...] = jnp.zeros_like(xbuf_ref)
        bbuf_ref[...] = jnp.zeros_like(bbuf_ref)
        st_ref[...] = jnp.zeros_like(st_ref)

    def conv(in_ref, tail_ref, w_ref, b_ref):
        cur = in_ref[...]
        tail = tail_ref[...]
        row8 = lax.broadcasted_iota(jnp.int32, tail.shape, 0)
        acc = b_ref[...] + w_ref[SSD_CONV - 1:SSD_CONV, :] * cur
        for k in range(1, SSD_CONV):
            back = pltpu.roll(cur, k, axis=0)
            head = jnp.where(row8 < k, pltpu.roll(tail, k, axis=0), back[0:pad])
            back = jnp.concatenate([head, back[pad:]], axis=0)
            acc = acc + w_ref[SSD_CONV - 1 - k:SSD_CONV - k, :] * back
        tail_ref[...] = cur[q - pad:, :]
        return _silu(acc)

    xs = conv(xs_ref, xbuf_ref, cwx_ref, cbx_ref)
    bc = conv(bc_ref, bbuf_ref, cwb_ref, cbb_ref)

    dtr = _mm(xin_ref[...].astype(MXU_DTYPE), wdt_ref[...]) + dtb_ref[...]
    dt = jnp.maximum(dtr, 0.0) + jnp.log1p(jnp.exp(-jnp.abs(dtr)))
    adt = dt * a_ref[...]
    ri = lax.broadcasted_iota(jnp.int32, (q, q), 0)
    ci = lax.broadcasted_iota(jnp.int32, (q, q), 1)
    causal = ri >= ci
    tri = causal.astype(F32)
    cs = jnp.dot(tri, adt, preferred_element_type=F32, precision=HI)
    eye = (ri == ci).astype(F32)
    cs2 = cs * LOG2_E
    r_rows = _mm_nt(eye, cs2, precision=HI) - jnp.log2(_mm_nt(eye, dt, precision=HI))
    tot = cs[q - 1:q, :]
    w_col = dt * jnp.exp(tot - cs)
    din = jnp.exp(cs)
    w_x = _mm(_split_hi_lo(w_col), e2_ref[...])
    din_x = _mm(_split_hi_lo(din), e2_ref[...])

    lane = lax.broadcasted_iota(jnp.int32, (q, LANES), 1)
    low = lane < SSD_HEADDIM

    xw = (xs * w_x).astype(MXU_DTYPE)
    xsb = xs.astype(MXU_DTYPE)
    for g in range(SSD_GROUPS):
        bg = bc[:, g * SSD_STATE:(g + 1) * SSD_STATE].astype(MXU_DTYPE)
        cg = bc[:, SSD_GROUPS * SSD_STATE + g * SSD_STATE:
                SSD_GROUPS * SSD_STATE + (g + 1) * SSD_STATE].astype(MXU_DTYPE)
        cb = _mm_nt(cg, bg)
        for pair in range(SSD_HEADS_PER_GROUP // 2):
            h0 = g * SSD_HEADS_PER_GROUP + 2 * pair
            ms = []
            for h in (h0, h0 + 1):
                seg = cs2[:, h:h + 1] - r_rows[h:h + 1, :]
                ms.append((cb * jnp.exp2(jnp.where(causal, seg, -jnp.inf))).astype(MXU_DTYPE))
            lhs = jnp.concatenate(ms, axis=1)
            xp = xsb[:, h0 * SSD_HEADDIM:(h0 + 2) * SSD_HEADDIM]
            zero = jnp.zeros_like(xp)
            rhs = jnp.concatenate([jnp.where(low, xp, zero), jnp.where(low, zero, xp)], axis=0)
            yd_ref[:, h0 * SSD_HEADDIM:(h0 + 2) * SSD_HEADDIM] = _mm(lhs, rhs)
        gsl = slice(g * SSD_GROUP_DIM, (g + 1) * SSD_GROUP_DIM)
        st = st_ref[g]
        y_off = _mm(cg, st.astype(MXU_DTYPE)) * din_x[:, gsl]
        yd_ref[:, gsl] = yd_ref[:, gsl] + y_off
        st_ref[g] = din_x[q - 1:q, gsl] * st + _mm_tn(bg, xw[:, gsl])

    y = yd_ref[...] + dsk_ref[...] * xs
    y = y * z_ref[...]
    for g in range(SSD_GROUPS):
        gsl = slice(g * SSD_GROUP_DIM, (g + 1) * SSD_GROUP_DIM)
        yg = y[:, gsl]
        yg = yg * lax.rsqrt(jnp.mean(yg * yg, axis=-1, keepdims=True) + LN_EPS)
        o_ref[:, gsl] = (yg * nw_ref[:, gsl]).astype(o_ref.dtype)


def _ssd_core(proj, x, w_dt, batch, conv_w, conv_b, dt_bias, a_log, d_skip, norm_w):
    m = proj.shape[0]
    q = SSD_CHUNK
    ncb = m // batch // q
    di = SSD_D_INNER
    wdt = jnp.pad(w_dt.astype(MXU_DTYPE), ((0, 0), (0, LANES - SSD_HEADS)))
    cw = conv_w.astype(F32)
    cbias = conv_b.astype(F32)[None, :]
    padh = LANES - SSD_HEADS
    dtb = jnp.pad(dt_bias.astype(F32), (0, padh))[None, :]
    a = jnp.pad(-jnp.exp(a_log.astype(F32)), (0, padh))[None, :]
    dsk = jnp.repeat(d_skip.astype(F32), SSD_HEADDIM)[None, :]
    nw = norm_w.astype(F32)[None, :]
    expand = (np.arange(LANES)[:, None] == (np.arange(di) // SSD_HEADDIM)[None, :]).astype(np.float32)
    e2 = jnp.asarray(np.concatenate([expand, expand], axis=0), dtype=MXU_DTYPE)

    row = lambda b, c: b * ncb + c
    full = _resident
    return pl.pallas_call(
        _ssd_core_kernel,
        grid=(batch, ncb),
        in_specs=[pl.BlockSpec((q, di), lambda b, c: (row(b, c), 0)),
                  pl.BlockSpec((q, di), lambda b, c: (row(b, c), 1)),
                  pl.BlockSpec((q, SSD_BC), lambda b, c: (row(b, c), 2 * di // SSD_BC)),
                  pl.BlockSpec((q, D_MODEL), lambda b, c: (row(b, c), 0)),
                  full((D_MODEL, LANES)),
                  full((SSD_CONV, di)), full((SSD_CONV, SSD_BC)), full((1, di)), full((1, SSD_BC)),
                  full((1, LANES)), full((1, LANES)), full((1, di)), full((1, di)),
                  full((2 * LANES, di))],
        out_specs=pl.BlockSpec((q, di), lambda b, c: (row(b, c), 0)),
        out_shape=jax.ShapeDtypeStruct((m, di), MXU_DTYPE),
        scratch_shapes=[pltpu.VMEM((8, di), F32), pltpu.VMEM((8, SSD_BC), F32),
                        pltpu.VMEM((SSD_GROUPS, SSD_STATE, SSD_GROUP_DIM), F32),
                        pltpu.VMEM((q, di), F32)],
        compiler_params=_params("parallel", "arbitrary"),
        name="ssd_core",
    )(proj, proj, proj, x, wdt, cw[:, :di], cw[:, di:], cbias[:, :di], cbias[:, di:], dtb, a, dsk, nw, e2)


def _ssd_layer(x, batch, w_in_stack, layer, conv_w, conv_b, dt_bias, a_log, d_skip, norm_w, w_out, ln):
    n_main = SSD_D_INNER + SSD_CONV_DIM
    proj = _proj(x, jnp.swapaxes(w_in_stack, 1, 2), layer, n_main, tm=min(1024, x.shape[0]), tn=1024,
                 out_dtype=F32, silu_cols=SSD_D_INNER, w_transposed=True)
    w_dt = w_in_stack[layer, :, n_main:]
    y = _ssd_core(proj, x, w_dt, batch, conv_w, conv_b, dt_bias, a_log, d_skip, norm_w)
    return _out_ln(y, w_out.astype(MXU_DTYPE), x, ln, tm=512)


def _ret_core_kernel(cd_ref, q_ref, k_ref, v_ref, g_ref, cos_ref, sin_ref, dmat_ref, xi_ref, zeta_ref,
                     o_ref, st_ref):
    @pl.when(pl.program_id(1) == 0)
    def _():
        st_ref[...] = jnp.zeros_like(st_ref)

    cos = cos_ref[...]
    sin = sin_ref[...]
    half = RET_DK // 2

    def rotate(t):
        t1 = t[:, :half]
        t2 = t[:, half:]
        return jnp.concatenate([t1 * cos - t2 * sin, t1 * sin + t2 * cos], axis=1)

    for h in range(RET_HEADS):
        qh = rotate(q_ref[:, h * RET_DK:(h + 1) * RET_DK].astype(F32))
        kh = rotate(k_ref[:, h * RET_DK:(h + 1) * RET_DK].astype(F32) * (RET_DK ** -0.5))
        vh = v_ref[:, h * RET_DV:(h + 1) * RET_DV].astype(MXU_DTYPE)
        qb = qh.astype(MXU_DTYPE)
        scores = _mm_nt(qb, kh.astype(MXU_DTYPE)) * dmat_ref[h]
        inner = _mm(scores.astype(MXU_DTYPE), vh)
        st = st_ref[h]
        xi = jnp.concatenate([xi_ref[h]] * (RET_DV // LANES), axis=1)
        cross = _mm(qb, st.astype(MXU_DTYPE)) * xi
        kz = kh * jnp.concatenate([zeta_ref[h]] * (RET_DK // LANES), axis=1)
        st_ref[h] = cd_ref[h] * st + _mm_tn(kz.astype(MXU_DTYPE), vh)
        out = inner + cross
        mu = jnp.mean(out, axis=-1, keepdims=True)
        d = out - mu
        var = jnp.mean(d * d, axis=-1, keepdims=True)
        out = d * lax.rsqrt(var + LN_EPS)
        gate = _silu(g_ref[:, h * RET_DV:(h + 1) * RET_DV].astype(F32))
        o_ref[:, h * RET_DV:(h + 1) * RET_DV] = (gate * out).astype(o_ref.dtype)


def _ret_core(proj, batch):
    m = proj.shape[0]
    seq = m // batch
    q = RET_CHUNK
    ncb = seq // q
    theta = (1.0 / (RET_ROPE_BASE ** np.linspace(0.0, 1.0, RET_DK // 2))).astype(np.float32)
    ang = (np.arange(seq, dtype=np.float32)[:, None] * theta[None, :]).astype(np.float64)
    cos, sin = np.cos(ang).astype(np.float32), np.sin(ang).astype(np.float32)
    lg = np.log1p(-np.exp2(-5.0 - np.arange(RET_HEADS)))
    pos = np.arange(q, dtype=np.float64)
    diff = pos[:, None] - pos[None, :]
    dmat = np.where(diff >= 0, np.exp(np.maximum(diff, 0.0)[None] * lg[:, None, None]), 0.0).astype(np.float32)
    xi = np.exp((pos[None, :] + 1.0) * lg[:, None])
    zeta = np.exp((q - 1.0 - pos[None, :]) * lg[:, None])
    xi_b = np.broadcast_to(xi[:, :, None], (RET_HEADS, q, LANES)).astype(np.float32)
    zeta_b = np.broadcast_to(zeta[:, :, None], (RET_HEADS, q, LANES)).astype(np.float32)
    chunk_decay = np.exp(q * lg).astype(np.float32)

    row = lambda b, c: b * ncb + c
    d = D_MODEL
    return pl.pallas_call(
        _ret_core_kernel,
        grid=(batch, ncb),
        in_specs=[pl.BlockSpec(memory_space=pltpu.SMEM),
                  pl.BlockSpec((q, d), lambda b, c: (row(b, c), 0)),
                  pl.BlockSpec((q, d), lambda b, c: (row(b, c), 1)),
                  pl.BlockSpec((q, 2 * d), lambda b, c: (row(b, c), 1)),
                  pl.BlockSpec((q, 2 * d), lambda b, c: (row(b, c), 2)),
                  pl.BlockSpec((q, RET_DK // 2), lambda b, c: (c, 0)),
                  pl.BlockSpec((q, RET_DK // 2), lambda b, c: (c, 0)),
                  _resident((RET_HEADS, q, q)),
                  _resident((RET_HEADS, q, LANES)),
                  _resident((RET_HEADS, q, LANES))],
        out_specs=pl.BlockSpec((q, 2 * d), lambda b, c: (row(b, c), 0)),
        out_shape=jax.ShapeDtypeStruct((m, 2 * d), MXU_DTYPE),
        scratch_shapes=[pltpu.VMEM((RET_HEADS, RET_DK, RET_DV), F32)],
        compiler_params=_params("parallel", "arbitrary"),
        name="ret_core",
    )(chunk_decay, proj, proj, proj, proj, cos, sin, dmat, xi_b, zeta_b)


def _ret_layer(x, batch, w_in_stack, layer, w_out, ln):
    proj = _proj(x, w_in_stack, layer, RET_IN_DIM, tm=min(1024, x.shape[0]), tn=1024, out_dtype=MXU_DTYPE)
    y = _ret_core(proj, batch)
    return _out_ln(y, w_out.astype(MXU_DTYPE), x, ln, tm=512)


def kernel(x, ln1_g, ln1_b, ln2_g, ln2_b, mlp_w1, mlp_w2, s5_w_in, s5_lam_re, s5_lam_im, s5_log_dt, s5_b_re, s5_b_im, s5_c_re, s5_c_im, s5_d, s5_w_out, s5_w_gate, ssd_w_in, ssd_conv_w, ssd_conv_b, ssd_dt_bias, ssd_a_log, ssd_d, ssd_norm_w, ssd_w_out, ret_w_in, ret_w_out):
    batch, seq, d = x.shape
    h = x.reshape(batch * seq, d)
    ln1 = (ln1_g[:, None, :], ln1_b[:, None, :])
    ln2 = (ln2_g[:, None, :], ln2_b[:, None, :])
    s5_tables = jax.vmap(_s5_tables)(s5_lam_re, s5_lam_im, s5_log_dt, s5_b_re, s5_b_im, s5_c_re, s5_c_im, s5_d)
    s5_w_out_b = s5_w_out.astype(MXU_DTYPE)
    s5_w_gate_b = s5_w_gate.astype(MXU_DTYPE)
    for i in range(DEPTH):
        kind = i % 3
        j = i // 3
        if kind == 0:
            h = _s5_layer(h, batch, s5_w_in, s5_tables, j, s5_w_out_b, s5_w_gate_b, ln1 + (i,))
        elif kind == 1:
            h = _ssd_layer(h, batch, ssd_w_in, j, ssd_conv_w[j], ssd_conv_b[j], ssd_dt_bias[j], ssd_a_log[j],
                           ssd_d[j], ssd_norm_w[j], ssd_w_out[j], ln1 + (i,))
        else:
            h = _ret_layer(h, batch, ret_w_in, j, ret_w_out[j], ln1 + (i,))
        h = _mlp(h, mlp_w1, mlp_w2, i, ln2 + (i,), tm=min(1024, h.shape[0]), tf=512)
    return h.reshape(batch, seq, d)
```

```python
import functools
import math

import jax
import jax.numpy as jnp
import numpy as np
from jax import lax
from jax.experimental import pallas as pl
from jax.experimental.pallas import tpu as pltpu

F32 = jnp.float32
MXU_DTYPE = jnp.bfloat16
HI = lax.Precision.HIGHEST
LOG2_E = 1.0 / math.log(2.0)

LANES = 128
VMEM_PHYSICAL_BYTES = 64 * 1024 * 1024
VMEM_LIMIT_BYTES = VMEM_PHYSICAL_BYTES - 6 * 1024 * 1024

D_MODEL = 2048
DEPTH = 4
DN_ALPHA = (2 * DEPTH) ** 0.25
LN_EPS = 1e-5
D_FF = 4 * D_MODEL
MLP_OUT_SLABS = 4
ROW_SPLIT = 2

S5_GROUP = 16
S5_GROUPS = D_MODEL // S5_GROUP
S5_STATE = 64
S5_CHUNK = 16
S5_SETS = D_MODEL // LANES
S5_SET_GROUPS = LANES // S5_GROUP
S5_SET_STATE = S5_SET_GROUPS * S5_STATE
S5_SCAN_UNROLL = 8
S5_TOEPLITZ_GROUP = 4

SSD_D_INNER = 2 * D_MODEL
SSD_HEADDIM = 64
SSD_HEADS = SSD_D_INNER // SSD_HEADDIM
SSD_GROUPS = 8
SSD_STATE = 128
SSD_CONV = 4
SSD_CHUNK = 128
SSD_BC = 2 * SSD_GROUPS * SSD_STATE
SSD_CONV_DIM = SSD_D_INNER + SSD_BC
SSD_GROUP_DIM = SSD_D_INNER // SSD_GROUPS
SSD_HEADS_PER_GROUP = SSD_HEADS // SSD_GROUPS

RET_HEADS = 8
RET_DK = D_MODEL // RET_HEADS
RET_DV = 2 * D_MODEL // RET_HEADS
RET_CHUNK = 128
RET_IN_DIM = 6 * D_MODEL
RET_ROPE_BASE = 10000.0


def _params(*semantics):
    return pltpu.CompilerParams(dimension_semantics=semantics, vmem_limit_bytes=VMEM_LIMIT_BYTES)


def _layer_norm(v, g, b):
    mu = jnp.mean(v, axis=-1, keepdims=True)
    d = v - mu
    var = jnp.mean(d * d, axis=-1, keepdims=True)
    return d * lax.rsqrt(var + LN_EPS) * g + b


def _silu(v):
    h = 0.5 * v
    return h + h * jnp.tanh(h)


def _mm(a, b):
    return jnp.dot(a, b, preferred_element_type=F32)


def _mm_nt(a, b, precision=None):
    return lax.dot_general(a, b, (((1,), (1,)), ((), ())), preferred_element_type=F32, precision=precision)


def _mm_tn(a, b):
    return lax.dot_general(a, b, (((0,), (0,)), ((), ())), preferred_element_type=F32)


def _resident(shape):
    return pl.BlockSpec(shape, lambda *_: (0,) * len(shape), pipeline_mode=pl.Buffered(1))


def _layer_row(shape, layer):
    return pl.BlockSpec((None,) + shape, lambda *_: (layer, 0, 0), pipeline_mode=pl.Buffered(1))


def _proj_kernel(x_ref, w_ref, o_ref, *, act_tiles, w_transposed):
    j = pl.program_id(1)

    def product():
        xb = x_ref[...].astype(MXU_DTYPE)
        wb = w_ref[...].astype(MXU_DTYPE)
        return _mm_nt(xb, wb) if w_transposed else _mm(xb, wb)

    if act_tiles == 0:
        o_ref[...] = product().astype(o_ref.dtype)
    else:
        @pl.when(j < act_tiles)
        def _():
            o_ref[...] = _silu(product()).astype(o_ref.dtype)

        @pl.when(j >= act_tiles)
        def _():
            o_ref[...] = product().astype(o_ref.dtype)


def _proj(x, w_stack, layer, n, tm, tn, out_dtype, silu_cols=0, w_transposed=False):
    m, k = x.shape
    assert silu_cols % tn == 0
    w_spec = (pl.BlockSpec((None, tn, k), lambda i, j: (layer, j, 0)) if w_transposed
              else pl.BlockSpec((None, k, tn), lambda i, j: (layer, 0, j)))
    return pl.pallas_call(
        functools.partial(_proj_kernel, act_tiles=silu_cols // tn, w_transposed=w_transposed),
        grid=(m // tm, n // tn),
        in_specs=[pl.BlockSpec((tm, k), lambda i, j: (i, 0)), w_spec],
        out_specs=pl.BlockSpec((tm, tn), lambda i, j: (i, j)),
        out_shape=jax.ShapeDtypeStruct((m, n), out_dtype),
        compiler_params=_params("parallel", "arbitrary"),
        name="proj",
    )(x, w_stack)


def _out_ln_kernel(y_ref, w_ref, x_ref, g_ref, b_ref, o_ref):
    half = o_ref.shape[0] // ROW_SPLIT
    for r in range(ROW_SPLIT):
        rows = slice(r * half, (r + 1) * half)
        f = _mm(y_ref[rows, :], w_ref[...])
        o_ref[rows, :] = _layer_norm(DN_ALPHA * x_ref[rows, :] + f, g_ref[...], b_ref[...])


def _out_ln(y, w, x, ln, tm):
    m, k = y.shape
    d = w.shape[1]
    g, b, layer = ln
    return pl.pallas_call(
        _out_ln_kernel,
        grid=(m // tm,),
        in_specs=[pl.BlockSpec((tm, k), lambda i: (i, 0)),
                  _resident((k, d)),
                  pl.BlockSpec((tm, d), lambda i: (i, 0)),
                  _layer_row((1, d), layer),
                  _layer_row((1, d), layer)],
        out_specs=pl.BlockSpec((tm, d), lambda i: (i, 0)),
        out_shape=jax.ShapeDtypeStruct((m, d), F32),
        compiler_params=_params("parallel"),
        name="out_ln",
    )(y, w, x, g, b)


def _mlp_kernel(x_ref, w1_ref, w2_ref, g_ref, b_ref, o_ref, *, nf):
    j = pl.program_id(1)

    @pl.when(j == 0)
    def _():
        o_ref[...] = DN_ALPHA * x_ref[...]

    h = _mm(x_ref[...].astype(MXU_DTYPE), w1_ref[...].astype(MXU_DTYPE))
    h = jnp.square(jnp.maximum(h, 0.0)).astype(MXU_DTYPE)
    slab = o_ref.shape[1] // MLP_OUT_SLABS
    for s in range(MLP_OUT_SLABS):
        cols = slice(s * slab, (s + 1) * slab)
        o_ref[:, cols] += _mm(h, w2_ref[:, cols].astype(MXU_DTYPE))

    @pl.when(j == nf - 1)
    def _():
        o_ref[...] = _layer_norm(o_ref[...], g_ref[...], b_ref[...])


def _mlp(x, w1_stack, w2_stack, layer, ln, tm, tf):
    m, d = x.shape
    f = w1_stack.shape[2]
    nf = f // tf
    g, b, ln_layer = ln
    return pl.pallas_call(
        functools.partial(_mlp_kernel, nf=nf),
        grid=(m // tm, nf),
        in_specs=[pl.BlockSpec((tm, d), lambda i, j: (i, 0)),
                  pl.BlockSpec((None, d, tf), lambda i, j: (layer, 0, j)),
                  pl.BlockSpec((None, tf, d), lambda i, j: (layer, j, 0)),
                  _layer_row((1, d), ln_layer),
                  _layer_row((1, d), ln_layer)],
        out_specs=pl.BlockSpec((tm, d), lambda i, j: (i, 0)),
        out_shape=jax.ShapeDtypeStruct((m, d), F32),
        compiler_params=_params("parallel", "arbitrary"),
        name="mlp",
    )(x, w1_stack, w2_stack, g, b)


def _s5_in_kernel(x_ref, w_ref, o_ref, wb_ref):
    @pl.when(pl.program_id(0) == 0)
    def _():
        wb_ref[...] = w_ref[...].astype(MXU_DTYPE)

    u = _mm(x_ref[...].astype(MXU_DTYPE), wb_ref[...])
    for gs in range(S5_SETS):
        o_ref[gs] = u[:, gs * LANES:(gs + 1) * LANES]


def _s5_in(x, w, layer, tm):
    m = x.shape[0]
    return pl.pallas_call(
        _s5_in_kernel,
        grid=(m // tm,),
        in_specs=[pl.BlockSpec((tm, D_MODEL), lambda i: (i, 0)),
                  _layer_row((D_MODEL, D_MODEL), layer)],
        out_specs=pl.BlockSpec((S5_SETS, tm, LANES), lambda i: (0, i, 0)),
        out_shape=jax.ShapeDtypeStruct((S5_SETS, m, LANES), F32),
        scratch_shapes=[pltpu.VMEM((D_MODEL, D_MODEL), MXU_DTYPE)],
        compiler_params=_params("arbitrary"),
        name="s5_in",
    )(x, w)


def _s5_core_kernel(u_ref, lb_ref, rc_ref, cre_ref, cim_ref, a16_ref, d_ref, o_ref,
                    wt_ref, win_ref, wout_ref, v_ref, sp_ref, yin_ref, *, batch, chunks_per_batch):
    nst = S5_SET_STATE
    kdim = S5_CHUNK * LANES
    lb = lb_ref[0]

    rc = rc_ref[0]
    lb_hi = lb.astype(MXU_DTYPE)
    lb_lo = (lb - lb_hi.astype(F32)).astype(MXU_DTYPE)
    rc_hi = rc.astype(MXU_DTYPE)
    rc_lo = (rc - rc_hi.astype(F32)).astype(MXU_DTYPE)
    drev = _mm(jnp.concatenate([lb_hi, lb_lo], axis=1), jnp.concatenate([rc_hi, rc_hi], axis=0))
    drev = drev + _mm(lb_hi, rc_lo)
    row = lax.broadcasted_iota(jnp.int32, (kdim, LANES), 0)
    lane = lax.broadcasted_iota(jnp.int32, (kdim, LANES), 1)
    row_group = (row % LANES) // S5_GROUP
    drev = jnp.where(row_group == lane // S5_GROUP, drev, 0.0).astype(MXU_DTYPE)
    for t in range(S5_CHUNK):
        used = (t + 1) * LANES
        group_rows = (t // S5_TOEPLITZ_GROUP + 1) * S5_TOEPLITZ_GROUP * LANES
        wt_ref[0:used, t * LANES:(t + 1) * LANES] = drev[kdim - used:, :]
        if used < group_rows:
            wt_ref[used:group_rows, t * LANES:(t + 1) * LANES] = jnp.zeros((group_rows - used, LANES), MXU_DTYPE)

    swapped = pltpu.roll(lb, S5_STATE, axis=1)
    low = lane < S5_STATE
    re_dup = jnp.where(low, lb, swapped)
    im_dup = jnp.where(low, swapped, lb)
    for v in range(nst // LANES):
        sel = row_group == 2 * v + (lane >= S5_STATE).astype(jnp.int32)
        win_ref[:, v * LANES:(v + 1) * LANES] = jnp.where(sel, re_dup, 0.0).astype(MXU_DTYPE)
        win_ref[:, nst + v * LANES:nst + (v + 1) * LANES] = jnp.where(sel, im_dup, 0.0).astype(MXU_DTYPE)

    lane_group = (lax.broadcasted_iota(jnp.int32, (S5_STATE, kdim), 1) % LANES) // S5_GROUP
    for ri, part in enumerate((cre_ref[0], -cim_ref[0])):
        for g in range(S5_SET_GROUPS):
            r0 = ri * nst + g * S5_STATE
            wout_ref[r0:r0 + S5_STATE, :] = jnp.where(lane_group == g, part, jnp.zeros_like(part))

    nchunks = batch * chunks_per_batch
    u = jnp.concatenate([u_ref[pl.ds(t, nchunks, stride=S5_CHUNK), :] for t in range(S5_CHUNK)], axis=1)
    xb = u.astype(MXU_DTYPE)
    vin = _mm(xb, win_ref[...])
    ntile = nst // LANES
    for k in range(2 * ntile):
        v_ref[k] = vin[:, k * LANES:(k + 1) * LANES]
    gcols = S5_TOEPLITZ_GROUP * LANES
    for j in range(S5_CHUNK // S5_TOEPLITZ_GROUP):
        cols = slice(j * gcols, (j + 1) * gcols)
        krows = (j + 1) * gcols
        yin_ref[:, cols] = _mm(xb[:, :krows], wt_ref[0:krows, cols]) + d_ref[0][:, cols] * u[:, cols]

    a_re = [a16_ref[0, :, k * LANES:(k + 1) * LANES] for k in range(ntile)]
    a_im = [a16_ref[0, :, nst + k * LANES:nst + (k + 1) * LANES] for k in range(ntile)]

    def step(c, carry):
        rows = pl.ds(c, batch, stride=chunks_per_batch)
        new = []
        for k in range(ntile):
            s_re, s_im = carry[k], carry[ntile + k]
            sp_ref[k, rows, :] = s_re
            sp_ref[ntile + k, rows, :] = s_im
            new.append((a_re[k] * s_re - a_im[k] * s_im + v_ref[k, rows, :],
                        a_re[k] * s_im + a_im[k] * s_re + v_ref[ntile + k, rows, :]))
        return tuple(n[0] for n in new) + tuple(n[1] for n in new)

    zero = jnp.zeros((batch, LANES), F32)
    lax.fori_loop(0, chunks_per_batch, step, (zero,) * (2 * ntile), unroll=S5_SCAN_UNROLL)

    sp = jnp.concatenate([sp_ref[k] for k in range(2 * ntile)], axis=1)
    h = jax.nn.gelu(yin_ref[...] + _mm(sp.astype(MXU_DTYPE), wout_ref[...]))
    for t in range(S5_CHUNK):
        o_ref[pl.ds(t, nchunks, stride=S5_CHUNK), :] = h[:, t * LANES:(t + 1) * LANES]


def _s5_core(ug, tables, layer, batch):
    nsets, m, _ = ug.shape
    nchunks = m // S5_CHUNK
    kdim = S5_CHUNK * LANES
    kern = functools.partial(_s5_core_kernel, batch=batch, chunks_per_batch=nchunks // batch)
    table = lambda rows, cols: pl.BlockSpec((None, 1, rows, cols), lambda s: (layer, s, 0, 0))
    return pl.pallas_call(
        kern,
        grid=(nsets,),
        in_specs=[pl.BlockSpec((None, m, LANES), lambda s: (s, 0, 0)),
                  table(kdim, LANES),
                  table(LANES, LANES),
                  table(S5_STATE, kdim),
                  table(S5_STATE, kdim),
                  table(1, 2 * S5_SET_STATE),
                  table(1, kdim)],
        out_specs=pl.BlockSpec((None, m, LANES), lambda s: (s, 0, 0)),
        out_shape=jax.ShapeDtypeStruct((nsets, m, LANES), F32),
        scratch_shapes=[pltpu.VMEM((kdim, kdim), MXU_DTYPE),
                        pltpu.VMEM((kdim, 2 * S5_SET_STATE), MXU_DTYPE),
                        pltpu.VMEM((2 * S5_SET_STATE, kdim), MXU_DTYPE),
                        pltpu.VMEM((2 * S5_SET_STATE // LANES, nchunks, LANES), F32),
                        pltpu.VMEM((2 * S5_SET_STATE // LANES, nchunks, LANES), F32),
                        pltpu.VMEM((nchunks, kdim), F32)],
        compiler_params=_params("parallel"),
        name="s5_core",
    )(ug, *tables)


def _s5_out_kernel(h_ref, wo_ref, wg_ref, x_ref, g_ref, b_ref, o_ref):
    half = o_ref.shape[0] // ROW_SPLIT
    for r in range(ROW_SPLIT):
        rows = slice(r * half, (r + 1) * half)
        h = jnp.concatenate([h_ref[gs, rows, :] for gs in range(S5_SETS)], axis=1).astype(MXU_DTYPE)
        f = _mm(h, wo_ref[...]) * jax.nn.sigmoid(_mm(h, wg_ref[...]))
        o_ref[rows, :] = _layer_norm(DN_ALPHA * x_ref[rows, :] + f, g_ref[...], b_ref[...])


def _s5_out(hg, wo, wg, w_layer, x, ln, tm):
    m = x.shape[0]
    g, b, layer = ln
    return pl.pallas_call(
        _s5_out_kernel,
        grid=(m // tm,),
        in_specs=[pl.BlockSpec((S5_SETS, tm, LANES), lambda i: (0, i, 0)),
                  _layer_row((D_MODEL, D_MODEL), w_layer),
                  _layer_row((D_MODEL, D_MODEL), w_layer),
                  pl.BlockSpec((tm, D_MODEL), lambda i: (i, 0)),
                  _layer_row((1, D_MODEL), layer),
                  _layer_row((1, D_MODEL), layer)],
        out_specs=pl.BlockSpec((tm, D_MODEL), lambda i: (i, 0)),
        out_shape=jax.ShapeDtypeStruct(x.shape, F32),
        compiler_params=_params("parallel"),
        name="s5_out",
    )(hg, wo, wg, x, g, b)


def _s5_tables(lam_re, lam_im, log_dt, b_re, b_im, c_re, c_im, d_skip):
    lr = lam_re.astype(F32)
    li = lam_im.astype(F32)
    dt = jnp.exp(log_dt.astype(F32))[:, None]
    mag = jnp.exp(lr * dt)
    ar = mag * jnp.cos(li * dt)
    ai = mag * jnp.sin(li * dt)
    den = lr * lr + li * li
    zr = ((ar - 1.0) * lr + ai * li) / den
    zi = (ai * lr - (ar - 1.0) * li) / den
    br_ = b_re.astype(F32)
    bi_ = b_im.astype(F32)
    bbr = zr[..., None] * br_ - zi[..., None] * bi_
    bbi = zr[..., None] * bi_ + zi[..., None] * br_
    k = jnp.arange(S5_CHUNK + 1, dtype=F32)[:, None, None]
    pmag = jnp.exp(k * (lr * dt))
    pr = pmag * jnp.cos(k * (li * dt))
    pi = pmag * jnp.sin(k * (li * dt))

    bt_re = jnp.swapaxes(bbr, 1, 2)
    bt_im = jnp.swapaxes(bbi, 1, 2)
    shape_b = (S5_SETS, 1, S5_SET_GROUPS, S5_GROUP, 2 * S5_STATE)
    b_with_re = jnp.concatenate([bt_re, bt_im], axis=-1).reshape(shape_b)
    b_with_im = jnp.concatenate([-bt_im, bt_re], axis=-1).reshape(shape_b)
    def power_rows(t):
        t = jnp.concatenate([t, t], axis=-1).reshape(S5_CHUNK, S5_SETS, S5_SET_GROUPS, 1, 2 * S5_STATE)
        return t.transpose(1, 0, 2, 3, 4)
    lb = power_rows(pr[S5_CHUNK - 1::-1]) * b_with_re + power_rows(pi[S5_CHUNK - 1::-1]) * b_with_im
    lb = lb.reshape(S5_SETS, S5_CHUNK * LANES, 2 * S5_STATE)

    def rows_p(t):
        return t.astype(F32).reshape(S5_SETS, S5_SET_GROUPS, S5_GROUP, S5_STATE).transpose(0, 3, 1, 2)
    ct_re = rows_p(c_re)
    ct_im = rows_p(c_im)
    rc = jnp.concatenate([ct_re, -ct_im], axis=1).reshape(S5_SETS, 2 * S5_STATE, LANES)

    def power_lanes(t):
        return t.reshape(S5_CHUNK, S5_SETS, S5_SET_GROUPS, S5_STATE).transpose(1, 3, 0, 2)[..., None]
    prs = power_lanes(pr[1:])
    pis = power_lanes(pi[1:])
    ca_re = ct_re[:, :, None] * prs - ct_im[:, :, None] * pis
    ca_im = ct_re[:, :, None] * pis + ct_im[:, :, None] * prs
    ca_re = ca_re.reshape(S5_SETS, S5_STATE, S5_CHUNK * LANES).astype(MXU_DTYPE)
    ca_im = ca_im.reshape(S5_SETS, S5_STATE, S5_CHUNK * LANES).astype(MXU_DTYPE)

    a16 = jnp.concatenate([pr[S5_CHUNK].reshape(S5_SETS, 1, S5_SET_STATE),
                           pi[S5_CHUNK].reshape(S5_SETS, 1, S5_SET_STATE)], axis=2)
    drow = jnp.tile(d_skip.astype(F32).reshape(S5_SETS, 1, LANES), (1, 1, S5_CHUNK))
    return lb, rc, ca_re, ca_im, a16, drow


def _s5_layer(x, batch, w_in, tables, layer, w_out, w_gate, ln):
    ug = _s5_in(x, w_in, layer, tm=512)
    hg = _s5_core(ug, tables, layer, batch)
    return _s5_out(hg, w_out, w_gate, layer, x, ln, tm=512)


def _split_hi_lo(v):
    hi = v.astype(MXU_DTYPE)
    lo = (v - hi.astype(F32)).astype(MXU_DTYPE)
    return jnp.concatenate([hi, lo], axis=1)


def _ssd_core_kernel(z_ref, xs_ref, bc_ref, xin_ref, wdt_ref, cwx_ref, cwb_ref, cbx_ref, cbb_ref,
                     dtb_ref, a_ref, dsk_ref, nw_ref, e2_ref, o_ref,
                     xbuf_ref, bbuf_ref, st_ref, yd_ref):
    q = SSD_CHUNK
    pad = 8

    @pl.when(pl.program_id(1) == 0)
    def _():
        xbuf_ref[...] = jnp.zeros_like(xbuf_ref)
        bbuf_ref[...] = jnp.zeros_like(bbuf_ref)
        st_ref[...] = jnp.zeros_like(st_ref)

    def conv(in_ref, tail_ref, w_ref, b_ref):
        cur = in_ref[...]
        tail = tail_ref[...]
        row8 = lax.broadcasted_iota(jnp.int32, tail.shape, 0)
        acc = b_ref[...] + w_ref[SSD_CONV - 1:SSD_CONV, :] * cur
        for k in range(1, SSD_CONV):
            back = pltpu.roll(cur, k, axis=0)
            head = jnp.where(row8 < k, pltpu.roll(tail, k, axis=0), back[0:pad])
            back = jnp.concatenate([head, back[pad:]], axis=0)
            acc = acc + w_ref[SSD_CONV - 1 - k:SSD_CONV - k, :] * back
        tail_ref[...] = cur[q - pad:, :]
        return _silu(acc)

    xs = conv(xs_ref, xbuf_ref, cwx_ref, cbx_ref)
    bc = conv(bc_ref, bbuf_ref, cwb_ref, cbb_ref)

    dtr = _mm(xin_ref[...].astype(MXU_DTYPE), wdt_ref[...]) + dtb_ref[...]
    dt = jnp.maximum(dtr, 0.0) + jnp.log1p(jnp.exp(-jnp.abs(dtr)))
    adt = dt * a_ref[...]
    ri = lax.broadcasted_iota(jnp.int32, (q, q), 0)
    ci = lax.broadcasted_iota(jnp.int32, (q, q), 1)
    causal = ri >= ci
    tri = causal.astype(F32)
    cs = jnp.dot(tri, adt, preferred_element_type=F32, precision=HI)
    eye = (ri == ci).astype(F32)
    cs2 = cs * LOG2_E
    r_rows = _mm_nt(eye, cs2, precision=HI) - jnp.log2(_mm_nt(eye, dt, precision=HI))
    tot = cs[q - 1:q, :]
    w_col = dt * jnp.exp(tot - cs)
    din = jnp.exp(cs)
    w_x = _mm(_split_hi_lo(w_col), e2_ref[...])
    din_x = _mm(_split_hi_lo(din), e2_ref[...])

    lane = lax.broadcasted_iota(jnp.int32, (q, LANES), 1)
    low = lane < SSD_HEADDIM

    xw = (xs * w_x).astype(MXU_DTYPE)
    xsb = xs.astype(MXU_DTYPE)
    for g in range(SSD_GROUPS):
        bg = bc[:, g * SSD_STATE:(g + 1) * SSD_STATE].astype(MXU_DTYPE)
        cg = bc[:, SSD_GROUPS * SSD_STATE + g * SSD_STATE:
                SSD_GROUPS * SSD_STATE + (g + 1) * SSD_STATE].astype(MXU_DTYPE)
        cb = _mm_nt(cg, bg)
        for pair in range(SSD_HEADS_PER_GROUP // 2):
            h0 = g * SSD_HEADS_PER_GROUP + 2 * pair
            ms = []
            for h in (h0, h0 + 1):
                seg = cs2[:, h:h + 1] - r_rows[h:h + 1, :]
                ms.append((cb * jnp.exp2(jnp.where(causal, seg, -jnp.inf))).astype(MXU_DTYPE))
            lhs = jnp.concatenate(ms, axis=1)
            xp = xsb[:, h0 * SSD_HEADDIM:(h0 + 2) * SSD_HEADDIM]
            zero = jnp.zeros_like(xp)
            rhs = jnp.concatenate([jnp.where(low, xp, zero), jnp.where(low, zero, xp)], axis=0)
            yd_ref[:, h0 * SSD_HEADDIM:(h0 + 2) * SSD_HEADDIM] = _mm(lhs, rhs)
        gsl = slice(g * SSD_GROUP_DIM, (g + 1) * SSD_GROUP_DIM)
        st = st_ref[g]
        y_off = _mm(cg, st.astype(MXU_DTYPE)) * din_x[:, gsl]
        yd_ref[:, gsl] = yd_ref[:, gsl] + y_off
        st_ref[g] = din_x[q - 1:q, gsl] * st + _mm_tn(bg, xw[:, gsl])

    y = yd_ref[...] + dsk_ref[...] * xs
    y = y * z_ref[...]
    for g in range(SSD_GROUPS):
        gsl = slice(g * SSD_GROUP_DIM, (g + 1) * SSD_GROUP_DIM)
        yg = y[:, gsl]
        yg = yg * lax.rsqrt(jnp.mean(yg * yg, axis=-1, keepdims=True) + LN_EPS)
        o_ref[:, gsl] = (yg * nw_ref[:, gsl]).astype(o_ref.dtype)


def _ssd_core(proj, x, w_dt, batch, conv_w, conv_b, dt_bias, a_log, d_skip, norm_w):
    m = proj.shape[0]
    q = SSD_CHUNK
    ncb = m // batch // q
    di = SSD_D_INNER
    wdt = jnp.pad(w_dt.astype(MXU_DTYPE), ((0, 0), (0, LANES - SSD_HEADS)))
    cw = conv_w.astype(F32)
    cbias = conv_b.astype(F32)[None, :]
    padh = LANES - SSD_HEADS
    dtb = jnp.pad(dt_bias.astype(F32), (0, padh))[None, :]
    a = jnp.pad(-jnp.exp(a_log.astype(F32)), (0, padh))[None, :]
    dsk = jnp.repeat(d_skip.astype(F32), SSD_HEADDIM)[None, :]
    nw = norm_w.astype(F32)[None, :]
    expand = (np.arange(LANES)[:, None] == (np.arange(di) // SSD_HEADDIM)[None, :]).astype(np.float32)
    e2 = jnp.asarray(np.concatenate([expand, expand], axis=0), dtype=MXU_DTYPE)

    row = lambda b, c: b * ncb + c
    full = _resident
    return pl.pallas_call(
        _ssd_core_kernel,
        grid=(batch, ncb),
        in_specs=[pl.BlockSpec((q, di), lambda b, c: (row(b, c), 0)),
                  pl.BlockSpec((q, di), lambda b, c: (row(b, c), 1)),
                  pl.BlockSpec((q, SSD_BC), lambda b, c: (row(b, c), 2 * di // SSD_BC)),
                  pl.BlockSpec((q, D_MODEL), lambda b, c: (row(b, c), 0)),
                  full((D_MODEL, LANES)),
                  full((SSD_CONV, di)), full((SSD_CONV, SSD_BC)), full((1, di)), full((1, SSD_BC)),
                  full((1, LANES)), full((1, LANES)), full((1, di)), full((1, di)),
                  full((2 * LANES, di))],
        out_specs=pl.BlockSpec((q, di), lambda b, c: (row(b, c), 0)),
        out_shape=jax.ShapeDtypeStruct((m, di), MXU_DTYPE),
        scratch_shapes=[pltpu.VMEM((8, di), F32), pltpu.VMEM((8, SSD_BC), F32),
                        pltpu.VMEM((SSD_GROUPS, SSD_STATE, SSD_GROUP_DIM), F32),
                        pltpu.VMEM((q, di), F32)],
        compiler_params=_params("parallel", "arbitrary"),
        name="ssd_core",
    )(proj, proj, proj, x, wdt, cw[:, :di], cw[:, di:], cbias[:, :di], cbias[:, di:], dtb, a, dsk, nw, e2)


def _ssd_layer(x, batch, w_in_stack, layer, conv_w, conv_b, dt_bias, a_log, d_skip, norm_w, w_out, ln):
    n_main = SSD_D_INNER + SSD_CONV_DIM
    proj = _proj(x, jnp.swapaxes(w_in_stack, 1, 2), layer, n_main, tm=min(1024, x.shape[0]), tn=1024,
                 out_dtype=F32, silu_cols=SSD_D_INNER, w_transposed=True)
    w_dt = w_in_stack[layer, :, n_main:]
    y = _ssd_core(proj, x, w_dt, batch, conv_w, conv_b, dt_bias, a_log, d_skip, norm_w)
    return _out_ln(y, w_out.astype(MXU_DTYPE), x, ln, tm=512)


def _ret_core_kernel(cd_ref, q_ref, k_ref, v_ref, g_ref, cos_ref, sin_ref, dmat_ref, xi_ref, zeta_ref,
                     o_ref, st_ref):
    @pl.when(pl.program_id(1) == 0)
    def _():
        st_ref[...] = jnp.zeros_like(st_ref)

    cos = cos_ref[...]
    sin = sin_ref[...]
    half = RET_DK // 2

    def rotate(t):
        t1 = t[:, :half]
        t2 = t[:, half:]
        return jnp.concatenate([t1 * cos - t2 * sin, t1 * sin + t2 * cos], axis=1)

    for h in range(RET_HEADS):
        qh = rotate(q_ref[:, h * RET_DK:(h + 1) * RET_DK].astype(F32))
        kh = rotate(k_ref[:, h * RET_DK:(h + 1) * RET_DK].astype(F32) * (RET_DK ** -0.5))
        vh = v_ref[:, h * RET_DV:(h + 1) * RET_DV].astype(MXU_DTYPE)
        qb = qh.astype(MXU_DTYPE)
        scores = _mm_nt(qb, kh.astype(MXU_DTYPE)) * dmat_ref[h]
        inner = _mm(scores.astype(MXU_DTYPE), vh)
        st = st_ref[h]
        xi = jnp.concatenate([xi_ref[h]] * (RET_DV // LANES), axis=1)
        cross = _mm(qb, st.astype(MXU_DTYPE)) * xi
        kz = kh * jnp.concatenate([zeta_ref[h]] * (RET_DK // LANES), axis=1)
        st_ref[h] = cd_ref[h] * st + _mm_tn(kz.astype(MXU_DTYPE), vh)
        out = inner + cross
        mu = jnp.mean(out, axis=-1, keepdims=True)
        d = out - mu
        var = jnp.mean(d * d, axis=-1, keepdims=True)
        out = d * lax.rsqrt(var + LN_EPS)
        gate = _silu(g_ref[:, h * RET_DV:(h + 1) * RET_DV].astype(F32))
        o_ref[:, h * RET_DV:(h + 1) * RET_DV] = (gate * out).astype(o_ref.dtype)


def _ret_core(proj, batch):
    m = proj.shape[0]
    seq = m // batch
    q = RET_CHUNK
    ncb = seq // q
    theta = (1.0 / (RET_ROPE_BASE ** np.linspace(0.0, 1.0, RET_DK // 2))).astype(np.float32)
    ang = (np.arange(seq, dtype=np.float32)[:, None] * theta[None, :]).astype(np.float64)
    cos, sin = np.cos(ang).astype(np.float32), np.sin(ang).astype(np.float32)
    lg = np.log1p(-np.exp2(-5.0 - np.arange(RET_HEADS)))
    pos = np.arange(q, dtype=np.float64)
    diff = pos[:, None] - pos[None, :]
    dmat = np.where(diff >= 0, np.exp(np.maximum(diff, 0.0)[None] * lg[:, None, None]), 0.0).astype(np.float32)
    xi = np.exp((pos[None, :] + 1.0) * lg[:, None])
    zeta = np.exp((q - 1.0 - pos[None, :]) * lg[:, None])
    xi_b = np.broadcast_to(xi[:, :, None], (RET_HEADS, q, LANES)).astype(np.float32)
    zeta_b = np.broadcast_to(zeta[:, :, None], (RET_HEADS, q, LANES)).astype(np.float32)
    chunk_decay = np.exp(q * lg).astype(np.float32)

    row = lambda b, c: b * ncb + c
    d = D_MODEL
    return pl.pallas_call(
        _ret_core_kernel,
        grid=(batch, ncb),
        in_specs=[pl.BlockSpec(memory_space=pltpu.SMEM),
                  pl.BlockSpec((q, d), lambda b, c: (row(b, c), 0)),
                  pl.BlockSpec((q, d), lambda b, c: (row(b, c), 1)),
                  pl.BlockSpec((q, 2 * d), lambda b, c: (row(b, c), 1)),
                  pl.BlockSpec((q, 2 * d), lambda b, c: (row(b, c), 2)),
                  pl.BlockSpec((q, RET_DK // 2), lambda b, c: (c, 0)),
                  pl.BlockSpec((q, RET_DK // 2), lambda b, c: (c, 0)),
                  _resident((RET_HEADS, q, q)),
                  _resident((RET_HEADS, q, LANES)),
                  _resident((RET_HEADS, q, LANES))],
        out_specs=pl.BlockSpec((q, 2 * d), lambda b, c: (row(b, c), 0)),
        out_shape=jax.ShapeDtypeStruct((m, 2 * d), MXU_DTYPE),
        scratch_shapes=[pltpu.VMEM((RET_HEADS, RET_DK, RET_DV), F32)],
        compiler_params=_params("parallel", "arbitrary"),
        name="ret_core",
    )(chunk_decay, proj, proj, proj, proj, cos, sin, dmat, xi_b, zeta_b)


def _ret_layer(x, batch, w_in_stack, layer, w_out, ln):
    proj = _proj(x, w_in_stack, layer, RET_IN_DIM, tm=min(1024, x.shape[0]), tn=1024, out_dtype=MXU_DTYPE)
    y = _ret_core(proj, batch)
    return _out_ln(y, w_out.astype(MXU_DTYPE), x, ln, tm=512)


def kernel(x, ln1_g, ln1_b, ln2_g, ln2_b, mlp_w1, mlp_w2, s5_w_in, s5_lam_re, s5_lam_im, s5_log_dt, s5_b_re, s5_b_im, s5_c_re, s5_c_im, s5_d, s5_w_out, s5_w_gate, ssd_w_in, ssd_conv_w, ssd_conv_b, ssd_dt_bias, ssd_a_log, ssd_d, ssd_norm_w, ssd_w_out, ret_w_in, ret_w_out):
    batch, seq, d = x.shape
    h = x.reshape(batch * seq, d)
    ln1 = (ln1_g[:, None, :], ln1_b[:, None, :])
    ln2 = (ln2_g[:, None, :], ln2_b[:, None, :])
    s5_tables = jax.vmap(_s5_tables)(s5_lam_re, s5_lam_im, s5_log_dt, s5_b_re, s5_b_im, s5_c_re, s5_c_im, s5_d)
    s5_w_out_b = s5_w_out.astype(MXU_DTYPE)
    s5_w_gate_b = s5_w_gate.astype(MXU_DTYPE)
    for i in range(DEPTH):
        kind = i % 3
        j = i // 3
        if kind == 0:
            h = _s5_layer(h, batch, s5_w_in, s5_tables, j, s5_w_out_b, s5_w_gate_b, ln1 + (i,))
        elif kind == 1:
            h = _ssd_layer(h, batch, ssd_w_in, j, ssd_conv_w[j], ssd_conv_b[j], ssd_dt_bias[j], ssd_a_log[j],
                           ssd_d[j], ssd_norm_w[j], ssd_w_out[j], ln1 + (i,))
        else:
            h = _ret_layer(h, batch, ret_w_in, j, ret_w_out[j], ln1 + (i,))
        h = _mlp(h, mlp_w1, mlp_w2, i, ln2 + (i,), tm=min(1024, h.shape[0]), tf=512)
    return h.reshape(batch, seq, d)
```

```python
import functools
import math

import jax
import jax.numpy as jnp
import numpy as np
from jax import lax
from jax.experimental import pallas as pl
from jax.experimental.pallas import tpu as pltpu

F32 = jnp.float32
MXU_DTYPE = jnp.bfloat16
HI = lax.Precision.HIGHEST
LOG2_E = 1.0 / math.log(2.0)

LANES = 128
VMEM_PHYSICAL_BYTES = 64 * 1024 * 1024
VMEM_LIMIT_BYTES = VMEM_PHYSICAL_BYTES - 6 * 1024 * 1024

D_MODEL = 2048
DEPTH = 4
DN_ALPHA = (2 * DEPTH) ** 0.25
LN_EPS = 1e-5
D_FF = 4 * D_MODEL
MLP_OUT_SLABS = 4
ROW_SPLIT = 2

S5_GROUP = 16
S5_GROUPS = D_MODEL // S5_GROUP
S5_STATE = 64
S5_CHUNK = 16
S5_SETS = D_MODEL // LANES
S5_SET_GROUPS = LANES // S5_GROUP
S5_SET_STATE = S5_SET_GROUPS * S5_STATE
S5_SCAN_UNROLL = 8
S5_TOEPLITZ_GROUP = 4

SSD_D_INNER = 2 * D_MODEL
SSD_HEADDIM = 64
SSD_HEADS = SSD_D_INNER // SSD_HEADDIM
SSD_GROUPS = 8
SSD_STATE = 128
SSD_CONV = 4
SSD_CHUNK = 128
SSD_BC = 2 * SSD_GROUPS * SSD_STATE
SSD_CONV_DIM = SSD_D_INNER + SSD_BC
SSD_GROUP_DIM = SSD_D_INNER // SSD_GROUPS
SSD_HEADS_PER_GROUP = SSD_HEADS // SSD_GROUPS

RET_HEADS = 8
RET_DK = D_MODEL // RET_HEADS
RET_DV = 2 * D_MODEL // RET_HEADS
RET_CHUNK = 256
RET_IN_DIM = 6 * D_MODEL
RET_ROPE_BASE = 10000.0


def _params(*semantics):
    return pltpu.CompilerParams(dimension_semantics=semantics, vmem_limit_bytes=VMEM_LIMIT_BYTES)


def _layer_norm(v, g, b):
    mu = jnp.mean(v, axis=-1, keepdims=True)
    d = v - mu
    var = jnp.mean(d * d, axis=-1, keepdims=True)
    return d * lax.rsqrt(var + LN_EPS) * g + b


def _silu(v):
    h = 0.5 * v
    return h + h * jnp.tanh(h)


def _mm(a, b):
    return jnp.dot(a, b, preferred_element_type=F32)


def _mm_nt(a, b, precision=None):
    return lax.dot_general(a, b, (((1,), (1,)), ((), ())), preferred_element_type=F32, precision=precision)


def _mm_tn(a, b):
    return lax.dot_general(a, b, (((0,), (0,)), ((), ())), preferred_element_type=F32)


def _resident(shape):
    return pl.BlockSpec(shape, lambda *_: (0,) * len(shape), pipeline_mode=pl.Buffered(1))


def _layer_row(shape, layer):
    return pl.BlockSpec((None,) + shape, lambda *_: (layer, 0, 0), pipeline_mode=pl.Buffered(1))


def _proj_kernel(x_ref, w_ref, o_ref, *, act_tiles, w_transposed):
    j = pl.program_id(1)

    def product():
        xb = x_ref[...].astype(MXU_DTYPE)
        wb = w_ref[...].astype(MXU_DTYPE)
        return _mm_nt(xb, wb) if w_transposed else _mm(xb, wb)

    if act_tiles == 0:
        o_ref[...] = product().astype(o_ref.dtype)
    else:
        @pl.when(j < act_tiles)
        def _():
            o_ref[...] = _silu(product()).astype(o_ref.dtype)

        @pl.when(j >= act_tiles)
        def _():
            o_ref[...] = product().astype(o_ref.dtype)


def _proj(x, w_stack, layer, n, tm, tn, out_dtype, silu_cols=0, w_transposed=False):
    m, k = x.shape
    assert silu_cols % tn == 0
    w_spec = (pl.BlockSpec((None, tn, k), lambda i, j: (layer, j, 0)) if w_transposed
              else pl.BlockSpec((None, k, tn), lambda i, j: (layer, 0, j)))
    return pl.pallas_call(
        functools.partial(_proj_kernel, act_tiles=silu_cols // tn, w_transposed=w_transposed),
        grid=(m // tm, n // tn),
        in_specs=[pl.BlockSpec((tm, k), lambda i, j: (i, 0)), w_spec],
        out_specs=pl.BlockSpec((tm, tn), lambda i, j: (i, j)),
        out_shape=jax.ShapeDtypeStruct((m, n), out_dtype),
        compiler_params=_params("parallel", "arbitrary"),
        name="proj",
    )(x, w_stack)


def _out_ln_kernel(y_ref, w_ref, x_ref, g_ref, b_ref, o_ref):
    half = o_ref.shape[0] // ROW_SPLIT
    for r in range(ROW_SPLIT):
        rows = slice(r * half, (r + 1) * half)
        f = _mm(y_ref[rows, :], w_ref[...])
        o_ref[rows, :] = _layer_norm(DN_ALPHA * x_ref[rows, :] + f, g_ref[...], b_ref[...])


def _out_ln(y, w, x, ln, tm):
    m, k = y.shape
    d = w.shape[1]
    g, b, layer = ln
    return pl.pallas_call(
        _out_ln_kernel,
        grid=(m // tm,),
        in_specs=[pl.BlockSpec((tm, k), lambda i: (i, 0)),
                  _resident((k, d)),
                  pl.BlockSpec((tm, d), lambda i: (i, 0)),
                  _layer_row((1, d), layer),
                  _layer_row((1, d), layer)],
        out_specs=pl.BlockSpec((tm, d), lambda i: (i, 0)),
        out_shape=jax.ShapeDtypeStruct((m, d), F32),
        compiler_params=_params("parallel"),
        name="out_ln",
    )(y, w, x, g, b)


def _mlp_kernel(x_ref, w1_ref, w2_ref, g_ref, b_ref, o_ref, *, nf):
    j = pl.program_id(1)

    @pl.when(j == 0)
    def _():
        o_ref[...] = DN_ALPHA * x_ref[...]

    h = _mm(x_ref[...].astype(MXU_DTYPE), w1_ref[...].astype(MXU_DTYPE))
    h = jnp.square(jnp.maximum(h, 0.0)).astype(MXU_DTYPE)
    slab = o_ref.shape[1] // MLP_OUT_SLABS
    for s in range(MLP_OUT_SLABS):
        cols = slice(s * slab, (s + 1) * slab)
        o_ref[:, cols] += _mm(h, w2_ref[:, cols].astype(MXU_DTYPE))

    @pl.when(j == nf - 1)
    def _():
        o_ref[...] = _layer_norm(o_ref[...], g_ref[...], b_ref[...])


def _mlp(x, w1_stack, w2_stack, layer, ln, tm, tf):
    m, d = x.shape
    f = w1_stack.shape[2]
    nf = f // tf
    g, b, ln_layer = ln
    return pl.pallas_call(
        functools.partial(_mlp_kernel, nf=nf),
        grid=(m // tm, nf),
        in_specs=[pl.BlockSpec((tm, d), lambda i, j: (i, 0)),
                  pl.BlockSpec((None, d, tf), lambda i, j: (layer, 0, j)),
                  pl.BlockSpec((None, tf, d), lambda i, j: (layer, j, 0)),
                  _layer_row((1, d), ln_layer),
                  _layer_row((1, d), ln_layer)],
        out_specs=pl.BlockSpec((tm, d), lambda i, j: (i, 0)),
        out_shape=jax.ShapeDtypeStruct((m, d), F32),
        compiler_params=_params("parallel", "arbitrary"),
        name="mlp",
    )(x, w1_stack, w2_stack, g, b)


def _s5_in_kernel(x_ref, w_ref, o_ref, wb_ref):
    @pl.when(pl.program_id(0) == 0)
    def _():
        wb_ref[...] = w_ref[...].astype(MXU_DTYPE)

    u = _mm(x_ref[...].astype(MXU_DTYPE), wb_ref[...])
    for gs in range(S5_SETS):
        o_ref[gs] = u[:, gs * LANES:(gs + 1) * LANES]


def _s5_in(x, w, layer, tm):
    m = x.shape[0]
    return pl.pallas_call(
        _s5_in_kernel,
        grid=(m // tm,),
        in_specs=[pl.BlockSpec((tm, D_MODEL), lambda i: (i, 0)),
                  _layer_row((D_MODEL, D_MODEL), layer)],
        out_specs=pl.BlockSpec((S5_SETS, tm, LANES), lambda i: (0, i, 0)),
        out_shape=jax.ShapeDtypeStruct((S5_SETS, m, LANES), F32),
        scratch_shapes=[pltpu.VMEM((D_MODEL, D_MODEL), MXU_DTYPE)],
        compiler_params=_params("arbitrary"),
        name="s5_in",
    )(x, w)


def _s5_core_kernel(u_ref, lb_ref, rc_ref, cre_ref, cim_ref, a16_ref, d_ref, o_ref,
                    wt_ref, win_ref, wout_ref, v_ref, sp_ref, yin_ref, *, batch, chunks_per_batch):
    nst = S5_SET_STATE
    kdim = S5_CHUNK * LANES
    lb = lb_ref[0]

    rc = rc_ref[0]
    lb_hi = lb.astype(MXU_DTYPE)
    lb_lo = (lb - lb_hi.astype(F32)).astype(MXU_DTYPE)
    rc_hi = rc.astype(MXU_DTYPE)
    rc_lo = (rc - rc_hi.astype(F32)).astype(MXU_DTYPE)
    drev = _mm(jnp.concatenate([lb_hi, lb_lo], axis=1), jnp.concatenate([rc_hi, rc_hi], axis=0))
    drev = drev + _mm(lb_hi, rc_lo)
    row = lax.broadcasted_iota(jnp.int32, (kdim, LANES), 0)
    lane = lax.broadcasted_iota(jnp.int32, (kdim, LANES), 1)
    row_group = (row % LANES) // S5_GROUP
    drev = jnp.where(row_group == lane // S5_GROUP, drev, 0.0).astype(MXU_DTYPE)
    for t in range(S5_CHUNK):
        used = (t + 1) * LANES
        group_rows = (t // S5_TOEPLITZ_GROUP + 1) * S5_TOEPLITZ_GROUP * LANES
        wt_ref[0:used, t * LANES:(t + 1) * LANES] = drev[kdim - used:, :]
        if used < group_rows:
            wt_ref[used:group_rows, t * LANES:(t + 1) * LANES] = jnp.zeros((group_rows - used, LANES), MXU_DTYPE)

    swapped = pltpu.roll(lb, S5_STATE, axis=1)
    low = lane < S5_STATE
    re_dup = jnp.where(low, lb, swapped)
    im_dup = jnp.where(low, swapped, lb)
    for v in range(nst // LANES):
        sel = row_group == 2 * v + (lane >= S5_STATE).astype(jnp.int32)
        win_ref[:, v * LANES:(v + 1) * LANES] = jnp.where(sel, re_dup, 0.0).astype(MXU_DTYPE)
        win_ref[:, nst + v * LANES:nst + (v + 1) * LANES] = jnp.where(sel, im_dup, 0.0).astype(MXU_DTYPE)

    lane_group = (lax.broadcasted_iota(jnp.int32, (S5_STATE, kdim), 1) % LANES) // S5_GROUP
    for ri, part in enumerate((cre_ref[0], -cim_ref[0])):
        for g in range(S5_SET_GROUPS):
            r0 = ri * nst + g * S5_STATE
            wout_ref[r0:r0 + S5_STATE, :] = jnp.where(lane_group == g, part, jnp.zeros_like(part))

    nchunks = batch * chunks_per_batch
    u = jnp.concatenate([u_ref[pl.ds(t, nchunks, stride=S5_CHUNK), :] for t in range(S5_CHUNK)], axis=1)
    xb = u.astype(MXU_DTYPE)
    vin = _mm(xb, win_ref[...])
    ntile = nst // LANES
    for k in range(2 * ntile):
        v_ref[k] = vin[:, k * LANES:(k + 1) * LANES]
    gcols = S5_TOEPLITZ_GROUP * LANES
    for j in range(S5_CHUNK // S5_TOEPLITZ_GROUP):
        cols = slice(j * gcols, (j + 1) * gcols)
        krows = (j + 1) * gcols
        yin_ref[:, cols] = _mm(xb[:, :krows], wt_ref[0:krows, cols]) + d_ref[0][:, cols] * u[:, cols]

    a_re = [a16_ref[0, :, k * LANES:(k + 1) * LANES] for k in range(ntile)]
    a_im = [a16_ref[0, :, nst + k * LANES:nst + (k + 1) * LANES] for k in range(ntile)]

    def step(c, carry):
        rows = pl.ds(c, batch, stride=chunks_per_batch)
        new = []
        for k in range(ntile):
            s_re, s_im = carry[k], carry[ntile + k]
            sp_ref[k, rows, :] = s_re
            sp_ref[ntile + k, rows, :] = s_im
            new.append((a_re[k] * s_re - a_im[k] * s_im + v_ref[k, rows, :],
                        a_re[k] * s_im + a_im[k] * s_re + v_ref[ntile + k, rows, :]))
        return tuple(n[0] for n in new) + tuple(n[1] for n in new)

    zero = jnp.zeros((batch, LANES), F32)
    lax.fori_loop(0, chunks_per_batch, step, (zero,) * (2 * ntile), unroll=S5_SCAN_UNROLL)

    sp = jnp.concatenate([sp_ref[k] for k in range(2 * ntile)], axis=1)
    h = jax.nn.gelu(yin_ref[...] + _mm(sp.astype(MXU_DTYPE), wout_ref[...]))
    for t in range(S5_CHUNK):
        o_ref[pl.ds(t, nchunks, stride=S5_CHUNK), :] = h[:, t * LANES:(t + 1) * LANES]


def _s5_core(ug, tables, layer, batch):
    nsets, m, _ = ug.shape
    nchunks = m // S5_CHUNK
    kdim = S5_CHUNK * LANES
    kern = functools.partial(_s5_core_kernel, batch=batch, chunks_per_batch=nchunks // batch)
    table = lambda rows, cols: pl.BlockSpec((None, 1, rows, cols), lambda s: (layer, s, 0, 0))
    return pl.pallas_call(
        kern,
        grid=(nsets,),
        in_specs=[pl.BlockSpec((None, m, LANES), lambda s: (s, 0, 0)),
                  table(kdim, LANES),
                  table(LANES, LANES),
                  table(S5_STATE, kdim),
                  table(S5_STATE, kdim),
                  table(1, 2 * S5_SET_STATE),
                  table(1, kdim)],
        out_specs=pl.BlockSpec((None, m, LANES), lambda s: (s, 0, 0)),
        out_shape=jax.ShapeDtypeStruct((nsets, m, LANES), F32),
        scratch_shapes=[pltpu.VMEM((kdim, kdim), MXU_DTYPE),
                        pltpu.VMEM((kdim, 2 * S5_SET_STATE), MXU_DTYPE),
                        pltpu.VMEM((2 * S5_SET_STATE, kdim), MXU_DTYPE),
                        pltpu.VMEM((2 * S5_SET_STATE // LANES, nchunks, LANES), F32),
                        pltpu.VMEM((2 * S5_SET_STATE // LANES, nchunks, LANES), F32),
                        pltpu.VMEM((nchunks, kdim), F32)],
        compiler_params=_params("parallel"),
        name="s5_core",
    )(ug, *tables)


def _s5_out_kernel(h_ref, wo_ref, wg_ref, x_ref, g_ref, b_ref, o_ref):
    half = o_ref.shape[0] // ROW_SPLIT
    for r in range(ROW_SPLIT):
        rows = slice(r * half, (r + 1) * half)
        h = jnp.concatenate([h_ref[gs, rows, :] for gs in range(S5_SETS)], axis=1).astype(MXU_DTYPE)
        f = _mm(h, wo_ref[...]) * jax.nn.sigmoid(_mm(h, wg_ref[...]))
        o_ref[rows, :] = _layer_norm(DN_ALPHA * x_ref[rows, :] + f, g_ref[...], b_ref[...])


def _s5_out(hg, wo, wg, w_layer, x, ln, tm):
    m = x.shape[0]
    g, b, layer = ln
    return pl.pallas_call(
        _s5_out_kernel,
        grid=(m // tm,),
        in_specs=[pl.BlockSpec((S5_SETS, tm, LANES), lambda i: (0, i, 0)),
                  _layer_row((D_MODEL, D_MODEL), w_layer),
                  _layer_row((D_MODEL, D_MODEL), w_layer),
                  pl.BlockSpec((tm, D_MODEL), lambda i: (i, 0)),
                  _layer_row((1, D_MODEL), layer),
                  _layer_row((1, D_MODEL), layer)],
        out_specs=pl.BlockSpec((tm, D_MODEL), lambda i: (i, 0)),
        out_shape=jax.ShapeDtypeStruct(x.shape, F32),
        compiler_params=_params("parallel"),
        name="s5_out",
    )(hg, wo, wg, x, g, b)


def _s5_tables(lam_re, lam_im, log_dt, b_re, b_im, c_re, c_im, d_skip):
    lr = lam_re.astype(F32)
    li = lam_im.astype(F32)
    dt = jnp.exp(log_dt.astype(F32))[:, None]
    mag = jnp.exp(lr * dt)
    ar = mag * jnp.cos(li * dt)
    ai = mag * jnp.sin(li * dt)
    den = lr * lr + li * li
    zr = ((ar - 1.0) * lr + ai * li) / den
    zi = (ai * lr - (ar - 1.0) * li) / den
    br_ = b_re.astype(F32)
    bi_ = b_im.astype(F32)
    bbr = zr[..., None] * br_ - zi[..., None] * bi_
    bbi = zr[..., None] * bi_ + zi[..., None] * br_
    k = jnp.arange(S5_CHUNK + 1, dtype=F32)[:, None, None]
    pmag = jnp.exp(k * (lr * dt))
    pr = pmag * jnp.cos(k * (li * dt))
    pi = pmag * jnp.sin(k * (li * dt))

    bt_re = jnp.swapaxes(bbr, 1, 2)
    bt_im = jnp.swapaxes(bbi, 1, 2)
    shape_b = (S5_SETS, 1, S5_SET_GROUPS, S5_GROUP, 2 * S5_STATE)
    b_with_re = jnp.concatenate([bt_re, bt_im], axis=-1).reshape(shape_b)
    b_with_im = jnp.concatenate([-bt_im, bt_re], axis=-1).reshape(shape_b)
    def power_rows(t):
        t = jnp.concatenate([t, t], axis=-1).reshape(S5_CHUNK, S5_SETS, S5_SET_GROUPS, 1, 2 * S5_STATE)
        return t.transpose(1, 0, 2, 3, 4)
    lb = power_rows(pr[S5_CHUNK - 1::-1]) * b_with_re + power_rows(pi[S5_CHUNK - 1::-1]) * b_with_im
    lb = lb.reshape(S5_SETS, S5_CHUNK * LANES, 2 * S5_STATE)

    def rows_p(t):
        return t.astype(F32).reshape(S5_SETS, S5_SET_GROUPS, S5_GROUP, S5_STATE).transpose(0, 3, 1, 2)
    ct_re = rows_p(c_re)
    ct_im = rows_p(c_im)
    rc = jnp.concatenate([ct_re, -ct_im], axis=1).reshape(S5_SETS, 2 * S5_STATE, LANES)

    def power_lanes(t):
        return t.reshape(S5_CHUNK, S5_SETS, S5_SET_GROUPS, S5_STATE).transpose(1, 3, 0, 2)[..., None]
    prs = power_lanes(pr[1:])
    pis = power_lanes(pi[1:])
    ca_re = ct_re[:, :, None] * prs - ct_im[:, :, None] * pis
    ca_im = ct_re[:, :, None] * pis + ct_im[:, :, None] * prs
    ca_re = ca_re.reshape(S5_SETS, S5_STATE, S5_CHUNK * LANES).astype(MXU_DTYPE)
    ca_im = ca_im.reshape(S5_SETS, S5_STATE, S5_CHUNK * LANES).astype(MXU_DTYPE)

    a16 = jnp.concatenate([pr[S5_CHUNK].reshape(S5_SETS, 1, S5_SET_STATE),
                           pi[S5_CHUNK].reshape(S5_SETS, 1, S5_SET_STATE)], axis=2)
    drow = jnp.tile(d_skip.astype(F32).reshape(S5_SETS, 1, LANES), (1, 1, S5_CHUNK))
    return lb, rc, ca_re, ca_im, a16, drow


def _s5_layer(x, batch, w_in, tables, layer, w_out, w_gate, ln):
    ug = _s5_in(x, w_in, layer, tm=512)
    hg = _s5_core(ug, tables, layer, batch)
    return _s5_out(hg, w_out, w_gate, layer, x, ln, tm=512)


def _split_hi_lo(v):
    hi = v.astype(MXU_DTYPE)
    lo = (v - hi.astype(F32)).astype(MXU_DTYPE)
    return jnp.concatenate([hi, lo], axis=1)


def _ssd_core_kernel(z_ref, xs_ref, bc_ref, xin_ref, wdt_ref, cwx_ref, cwb_ref, cbx_ref, cbb_ref,
                     dtb_ref, a_ref, dsk_ref, nw_ref, e2_ref, o_ref,
                     xbuf_ref, bbuf_ref, st_ref, yd_ref):
    q = SSD_CHUNK
    pad = 8

    @pl.when(pl.program_id(1) == 0)
    def _():
        xbuf_ref[...] = jnp.zeros_like(xbuf_ref)
        bbuf_ref[...] = jnp.zeros_like(bbuf_ref)
        st_ref[...] = jnp.zeros_like(st_ref)

    def conv(in_ref, tail_ref, w_ref, b_ref):
        cur = in_ref[...]
        tail = tail_ref[...]
        row8 = lax.broadcasted_iota(jnp.int32, tail.shape, 0)
        acc = b_ref[...] + w_ref[SSD_CONV - 1:SSD_CONV, :] * cur
        for k in range(1, SSD_CONV):
            back = pltpu.roll(cur, k, axis=0)
            head = jnp.where(row8 < k, pltpu.roll(tail, k, axis=0), back[0:pad])
            back = jnp.concatenate([head, back[pad:]], axis=0)
            acc = acc + w_ref[SSD_CONV - 1 - k:SSD_CONV - k, :] * back
        tail_ref[...] = cur[q - pad:, :]
        return _silu(acc)

    xs = conv(xs_ref, xbuf_ref, cwx_ref, cbx_ref)
    bc = conv(bc_ref, bbuf_ref, cwb_ref, cbb_ref)

    dtr = _mm(xin_ref[...].astype(MXU_DTYPE), wdt_ref[...]) + dtb_ref[...]
    dt = jnp.maximum(dtr, 0.0) + jnp.log1p(jnp.exp(-jnp.abs(dtr)))
    adt = dt * a_ref[...]
    ri = lax.broadcasted_iota(jnp.int32, (q, q), 0)
    ci = lax.broadcasted_iota(jnp.int32, (q, q), 1)
    causal = ri >= ci
    tri = causal.astype(F32)
    cs = jnp.dot(tri, adt, preferred_element_type=F32, precision=HI)
    eye = (ri == ci).astype(F32)
    cs2 = cs * LOG2_E
    r_rows = _mm_nt(eye, cs2, precision=HI) - jnp.log2(_mm_nt(eye, dt, precision=HI))
    tot = cs[q - 1:q, :]
    w_col = dt * jnp.exp(tot - cs)
    din = jnp.exp(cs)
    w_x = _mm(_split_hi_lo(w_col), e2_ref[...])
    din_x = _mm(_split_hi_lo(din), e2_ref[...])

    lane = lax.broadcasted_iota(jnp.int32, (q, LANES), 1)
    low = lane < SSD_HEADDIM

    xw = (xs * w_x).astype(MXU_DTYPE)
    xsb = xs.astype(MXU_DTYPE)
    for g in range(SSD_GROUPS):
        bg = bc[:, g * SSD_STATE:(g + 1) * SSD_STATE].astype(MXU_DTYPE)
        cg = bc[:, SSD_GROUPS * SSD_STATE + g * SSD_STATE:
                SSD_GROUPS * SSD_STATE + (g + 1) * SSD_STATE].astype(MXU_DTYPE)
        cb = _mm_nt(cg, bg)
        for pair in range(SSD_HEADS_PER_GROUP // 2):
            h0 = g * SSD_HEADS_PER_GROUP + 2 * pair
            ms = []
            for h in (h0, h0 + 1):
                seg = cs2[:, h:h + 1] - r_rows[h:h + 1, :]
                ms.append((cb * jnp.exp2(jnp.where(causal, seg, -jnp.inf))).astype(MXU_DTYPE))
            lhs = jnp.concatenate(ms, axis=1)
            xp = xsb[:, h0 * SSD_HEADDIM:(h0 + 2) * SSD_HEADDIM]
            zero = jnp.zeros_like(xp)
            rhs = jnp.concatenate([jnp.where(low, xp, zero), jnp.where(low, zero, xp)], axis=0)
            yd_ref[:, h0 * SSD_HEADDIM:(h0 + 2) * SSD_HEADDIM] = _mm(lhs, rhs)
        gsl = slice(g * SSD_GROUP_DIM, (g + 1) * SSD_GROUP_DIM)
        st = st_ref[g]
        y_off = _mm(cg, st.astype(MXU_DTYPE)) * din_x[:, gsl]
        yd_ref[:, gsl] = yd_ref[:, gsl] + y_off
        st_ref[g] = din_x[q - 1:q, gsl] * st + _mm_tn(bg, xw[:, gsl])

    y = yd_ref[...] + dsk_ref[...] * xs
    y = y * z_ref[...]
    for g in range(SSD_GROUPS):
        gsl = slice(g * SSD_GROUP_DIM, (g + 1) * SSD_GROUP_DIM)
        yg = y[:, gsl]
        yg = yg * lax.rsqrt(jnp.mean(yg * yg, axis=-1, keepdims=True) + LN_EPS)
        o_ref[:, gsl] = (yg * nw_ref[:, gsl]).astype(o_ref.dtype)


def _ssd_core(proj, x, w_dt, batch, conv_w, conv_b, dt_bias, a_log, d_skip, norm_w):
    m = proj.shape[0]
    q = SSD_CHUNK
    ncb = m // batch // q
    di = SSD_D_INNER
    wdt = jnp.pad(w_dt.astype(MXU_DTYPE), ((0, 0), (0, LANES - SSD_HEADS)))
    cw = conv_w.astype(F32)
    cbias = conv_b.astype(F32)[None, :]
    padh = LANES - SSD_HEADS
    dtb = jnp.pad(dt_bias.astype(F32), (0, padh))[None, :]
    a = jnp.pad(-jnp.exp(a_log.astype(F32)), (0, padh))[None, :]
    dsk = jnp.repeat(d_skip.astype(F32), SSD_HEADDIM)[None, :]
    nw = norm_w.astype(F32)[None, :]
    expand = (np.arange(LANES)[:, None] == (np.arange(di) // SSD_HEADDIM)[None, :]).astype(np.float32)
    e2 = jnp.asarray(np.concatenate([expand, expand], axis=0), dtype=MXU_DTYPE)

    row = lambda b, c: b * ncb + c
    full = _resident
    return pl.pallas_call(
        _ssd_core_kernel,
        grid=(batch, ncb),
        in_specs=[pl.BlockSpec((q, di), lambda b, c: (row(b, c), 0)),
                  pl.BlockSpec((q, di), lambda b, c: (row(b, c), 1)),
                  pl.BlockSpec((q, SSD_BC), lambda b, c: (row(b, c), 2 * di // SSD_BC)),
                  pl.BlockSpec((q, D_MODEL), lambda b, c: (row(b, c), 0)),
                  full((D_MODEL, LANES)),
                  full((SSD_CONV, di)), full((SSD_CONV, SSD_BC)), full((1, di)), full((1, SSD_BC)),
                  full((1, LANES)), full((1, LANES)), full((1, di)), full((1, di)),
                  full((2 * LANES, di))],
        out_specs=pl.BlockSpec((q, di), lambda b, c: (row(b, c), 0)),
        out_shape=jax.ShapeDtypeStruct((m, di), MXU_DTYPE),
        scratch_shapes=[pltpu.VMEM((8, di), F32), pltpu.VMEM((8, SSD_BC), F32),
                        pltpu.VMEM((SSD_GROUPS, SSD_STATE, SSD_GROUP_DIM), F32),
                        pltpu.VMEM((q, di), F32)],
        compiler_params=_params("parallel", "arbitrary"),
        name="ssd_core",
    )(proj, proj, proj, x, wdt, cw[:, :di], cw[:, di:], cbias[:, :di], cbias[:, di:], dtb, a, dsk, nw, e2)


def _ssd_layer(x, batch, w_in_stack, layer, conv_w, conv_b, dt_bias, a_log, d_skip, norm_w, w_out, ln):
    n_main = SSD_D_INNER + SSD_CONV_DIM
    proj = _proj(x, jnp.swapaxes(w_in_stack, 1, 2), layer, n_main, tm=min(1024, x.shape[0]), tn=1024,
                 out_dtype=F32, silu_cols=SSD_D_INNER, w_transposed=True)
    w_dt = w_in_stack[layer, :, n_main:]
    y = _ssd_core(proj, x, w_dt, batch, conv_w, conv_b, dt_bias, a_log, d_skip, norm_w)
    return _out_ln(y, w_out.astype(MXU_DTYPE), x, ln, tm=512)


def _ret_core_kernel(cd_ref, q_ref, k_ref, v_ref, g_ref, cos_ref, sin_ref, dmat_ref, xi_ref, zeta_ref,
                     o_ref, st_ref):
    @pl.when(pl.program_id(1) == 0)
    def _():
        st_ref[...] = jnp.zeros_like(st_ref)

    cos = cos_ref[...]
    sin = sin_ref[...]
    half = RET_DK // 2

    def rotate(t):
        t1 = t[:, :half]
        t2 = t[:, half:]
        return jnp.concatenate([t1 * cos - t2 * sin, t1 * sin + t2 * cos], axis=1)

    for h in range(RET_HEADS):
        qh = rotate(q_ref[:, h * RET_DK:(h + 1) * RET_DK].astype(F32))
        kh = rotate(k_ref[:, h * RET_DK:(h + 1) * RET_DK].astype(F32) * (RET_DK ** -0.5))
        vh = v_ref[:, h * RET_DV:(h + 1) * RET_DV].astype(MXU_DTYPE)
        qb = qh.astype(MXU_DTYPE)
        scores = _mm_nt(qb, kh.astype(MXU_DTYPE)) * dmat_ref[h]
        inner = _mm(scores.astype(MXU_DTYPE), vh)
        st = st_ref[h]
        xi = jnp.concatenate([xi_ref[h]] * (RET_DV // LANES), axis=1)
        cross = _mm(qb, st.astype(MXU_DTYPE)) * xi
        kz = kh * jnp.concatenate([zeta_ref[h]] * (RET_DK // LANES), axis=1)
        st_ref[h] = cd_ref[h] * st + _mm_tn(kz.astype(MXU_DTYPE), vh)
        out = inner + cross
        mu = jnp.mean(out, axis=-1, keepdims=True)
        d = out - mu
        var = jnp.mean(d * d, axis=-1, keepdims=True)
        out = d * lax.rsqrt(var + LN_EPS)
        gate = _silu(g_ref[:, h * RET_DV:(h + 1) * RET_DV].astype(F32))
        o_ref[:, h * RET_DV:(h + 1) * RET_DV] = (gate * out).astype(o_ref.dtype)


def _ret_core(proj, batch):
    m = proj.shape[0]
    seq = m // batch
    q = RET_CHUNK
    ncb = seq // q
    theta = (1.0 / (RET_ROPE_BASE ** np.linspace(0.0, 1.0, RET_DK // 2))).astype(np.float32)
    ang = (np.arange(seq, dtype=np.float32)[:, None] * theta[None, :]).astype(np.float64)
    cos, sin = np.cos(ang).astype(np.float32), np.sin(ang).astype(np.float32)
    lg = np.log1p(-np.exp2(-5.0 - np.arange(RET_HEADS)))
    pos = np.arange(q, dtype=np.float64)
    diff = pos[:, None] - pos[None, :]
    dmat = np.where(diff >= 0, np.exp(np.maximum(diff, 0.0)[None] * lg[:, None, None]), 0.0).astype(np.float32)
    xi = np.exp((pos[None, :] + 1.0) * lg[:, None])
    zeta = np.exp((q - 1.0 - pos[None, :]) * lg[:, None])
    xi_b = np.broadcast_to(xi[:, :, None], (RET_HEADS, q, LANES)).astype(np.float32)
    zeta_b = np.broadcast_to(zeta[:, :, None], (RET_HEADS, q, LANES)).astype(np.float32)
    chunk_decay = np.exp(q * lg).astype(np.float32)

    row = lambda b, c: b * ncb + c
    d = D_MODEL
    return pl.pallas_call(
        _ret_core_kernel,
        grid=(batch, ncb),
        in_specs=[pl.BlockSpec(memory_space=pltpu.SMEM),
                  pl.BlockSpec((q, d), lambda b, c: (row(b, c), 0)),
                  pl.BlockSpec((q, d), lambda b, c: (row(b, c), 1)),
                  pl.BlockSpec((q, 2 * d), lambda b, c: (row(b, c), 1)),
                  pl.BlockSpec((q, 2 * d), lambda b, c: (row(b, c), 2)),
                  pl.BlockSpec((q, RET_DK // 2), lambda b, c: (c, 0)),
                  pl.BlockSpec((q, RET_DK // 2), lambda b, c: (c, 0)),
                  _resident((RET_HEADS, q, q)),
                  _resident((RET_HEADS, q, LANES)),
                  _resident((RET_HEADS, q, LANES))],
        out_specs=pl.BlockSpec((q, 2 * d), lambda b, c: (row(b, c), 0)),
        out_shape=jax.ShapeDtypeStruct((m, 2 * d), MXU_DTYPE),
        scratch_shapes=[pltpu.VMEM((RET_HEADS, RET_DK, RET_DV), F32)],
        compiler_params=_params("parallel", "arbitrary"),
        name="ret_core",
    )(chunk_decay, proj, proj, proj, proj, cos, sin, dmat, xi_b, zeta_b)


def _ret_layer(x, batch, w_in_stack, layer, w_out, ln):
    proj = _proj(x, w_in_stack, layer, RET_IN_DIM, tm=min(1024, x.shape[0]), tn=1024, out_dtype=MXU_DTYPE)
    y = _ret_core(proj, batch)
    return _out_ln(y, w_out.astype(MXU_DTYPE), x, ln, tm=512)


def kernel(x, ln1_g, ln1_b, ln2_g, ln2_b, mlp_w1, mlp_w2, s5_w_in, s5_lam_re, s5_lam_im, s5_log_dt, s5_b_re, s5_b_im, s5_c_re, s5_c_im, s5_d, s5_w_out, s5_w_gate, ssd_w_in, ssd_conv_w, ssd_conv_b, ssd_dt_bias, ssd_a_log, ssd_d, ssd_norm_w, ssd_w_out, ret_w_in, ret_w_out):
    batch, seq, d = x.shape
    h = x.reshape(batch * seq, d)
    ln1 = (ln1_g[:, None, :], ln1_b[:, None, :])
    ln2 = (ln2_g[:, None, :], ln2_b[:, None, :])
    s5_tables = jax.vmap(_s5_tables)(s5_lam_re, s5_lam_im, s5_log_dt, s5_b_re, s5_b_im, s5_c_re, s5_c_im, s5_d)
    s5_w_out_b = s5_w_out.astype(MXU_DTYPE)
    s5_w_gate_b = s5_w_gate.astype(MXU_DTYPE)
    for i in range(DEPTH):
        kind = i % 3
        j = i // 3
        if kind == 0:
            h = _s5_layer(h, batch, s5_w_in, s5_tables, j, s5_w_out_b, s5_w_gate_b, ln1 + (i,))
        elif kind == 1:
            h = _ssd_layer(h, batch, ssd_w_in, j, ssd_conv_w[j], ssd_conv_b[j], ssd_dt_bias[j], ssd_a_log[j],
                           ssd_d[j], ssd_norm_w[j], ssd_w_out[j], ln1 + (i,))
        else:
            h = _ret_layer(h, batch, ret_w_in, j, ret_w_out[j], ln1 + (i,))
        h = _mlp(h, mlp_w1, mlp_w2, i, ln2 + (i,), tm=min(1024, h.shape[0]), tf=512)
    return h.reshape(batch, seq, d)
```

```python
import functools
import math

import jax
import jax.numpy as jnp
import numpy as np
from jax import lax
from jax.experimental import pallas as pl
from jax.experimental.pallas import tpu as pltpu

F32 = jnp.float32
MXU_DTYPE = jnp.bfloat16
HI = lax.Precision.HIGHEST
LOG2_E = 1.0 / math.log(2.0)

LANES = 128
VMEM_PHYSICAL_BYTES = 64 * 1024 * 1024
VMEM_LIMIT_BYTES = VMEM_PHYSICAL_BYTES - 6 * 1024 * 1024

D_MODEL = 2048
DEPTH = 4
DN_ALPHA = (2 * DEPTH) ** 0.25
LN_EPS = 1e-5
D_FF = 4 * D_MODEL
MLP_OUT_SLABS = 4
ROW_SPLIT = 2

S5_GROUP = 16
S5_GROUPS = D_MODEL // S5_GROUP
S5_STATE = 64
S5_CHUNK = 16
S5_SETS = D_MODEL // LANES
S5_SET_GROUPS = LANES // S5_GROUP
S5_SET_STATE = S5_SET_GROUPS * S5_STATE
S5_SCAN_UNROLL = 8
S5_TOEPLITZ_GROUP = 2

SSD_D_INNER = 2 * D_MODEL
SSD_HEADDIM = 64
SSD_HEADS = SSD_D_INNER // SSD_HEADDIM
SSD_GROUPS = 8
SSD_STATE = 128
SSD_CONV = 4
SSD_CHUNK = 128
SSD_BC = 2 * SSD_GROUPS * SSD_STATE
SSD_CONV_DIM = SSD_D_INNER + SSD_BC
SSD_GROUP_DIM = SSD_D_INNER // SSD_GROUPS
SSD_HEADS_PER_GROUP = SSD_HEADS // SSD_GROUPS

RET_HEADS = 8
RET_DK = D_MODEL // RET_HEADS
RET_DV = 2 * D_MODEL // RET_HEADS
RET_CHUNK = 256
RET_IN_DIM = 6 * D_MODEL
RET_ROPE_BASE = 10000.0


def _params(*semantics):
    return pltpu.CompilerParams(dimension_semantics=semantics, vmem_limit_bytes=VMEM_LIMIT_BYTES)


def _layer_norm(v, g, b):
    mu = jnp.mean(v, axis=-1, keepdims=True)
    d = v - mu
    var = jnp.mean(d * d, axis=-1, keepdims=True)
    return d * lax.rsqrt(var + LN_EPS) * g + b


def _silu(v):
    h = 0.5 * v
    return h + h * jnp.tanh(h)


def _mm(a, b):
    return jnp.dot(a, b, preferred_element_type=F32)


def _mm_nt(a, b, precision=None):
    return lax.dot_general(a, b, (((1,), (1,)), ((), ())), preferred_element_type=F32, precision=precision)


def _mm_tn(a, b):
    return lax.dot_general(a, b, (((0,), (0,)), ((), ())), preferred_element_type=F32)


def _resident(shape):
    return pl.BlockSpec(shape, lambda *_: (0,) * len(shape), pipeline_mode=pl.Buffered(1))


def _layer_row(shape, layer):
    return pl.BlockSpec((None,) + shape, lambda *_: (layer, 0, 0), pipeline_mode=pl.Buffered(1))


def _proj_kernel(x_ref, w_ref, o_ref, *, act_tiles, w_transposed):
    j = pl.program_id(1)

    def product():
        xb = x_ref[...].astype(MXU_DTYPE)
        wb = w_ref[...].astype(MXU_DTYPE)
        return _mm_nt(xb, wb) if w_transposed else _mm(xb, wb)

    if act_tiles == 0:
        o_ref[...] = product().astype(o_ref.dtype)
    else:
        @pl.when(j < act_tiles)
        def _():
            o_ref[...] = _silu(product()).astype(o_ref.dtype)

        @pl.when(j >= act_tiles)
        def _():
            o_ref[...] = product().astype(o_ref.dtype)


def _proj(x, w_stack, layer, n, tm, tn, out_dtype, silu_cols=0, w_transposed=False):
    m, k = x.shape
    assert silu_cols % tn == 0
    w_spec = (pl.BlockSpec((None, tn, k), lambda i, j: (layer, j, 0)) if w_transposed
              else pl.BlockSpec((None, k, tn), lambda i, j: (layer, 0, j)))
    return pl.pallas_call(
        functools.partial(_proj_kernel, act_tiles=silu_cols // tn, w_transposed=w_transposed),
        grid=(m // tm, n // tn),
        in_specs=[pl.BlockSpec((tm, k), lambda i, j: (i, 0)), w_spec],
        out_specs=pl.BlockSpec((tm, tn), lambda i, j: (i, j)),
        out_shape=jax.ShapeDtypeStruct((m, n), out_dtype),
        compiler_params=_params("parallel", "arbitrary"),
        name="proj",
    )(x, w_stack)


def _out_ln_kernel(y_ref, w_ref, x_ref, g_ref, b_ref, o_ref):
    half = o_ref.shape[0] // ROW_SPLIT
    for r in range(ROW_SPLIT):
        rows = slice(r * half, (r + 1) * half)
        f = _mm(y_ref[rows, :], w_ref[...])
        o_ref[rows, :] = _layer_norm(DN_ALPHA * x_ref[rows, :] + f, g_ref[...], b_ref[...])


def _out_ln(y, w, x, ln, tm):
    m, k = y.shape
    d = w.shape[1]
    g, b, layer = ln
    return pl.pallas_call(
        _out_ln_kernel,
        grid=(m // tm,),
        in_specs=[pl.BlockSpec((tm, k), lambda i: (i, 0)),
                  _resident((k, d)),
                  pl.BlockSpec((tm, d), lambda i: (i, 0)),
                  _layer_row((1, d), layer),
                  _layer_row((1, d), layer)],
        out_specs=pl.BlockSpec((tm, d), lambda i: (i, 0)),
        out_shape=jax.ShapeDtypeStruct((m, d), F32),
        compiler_params=_params("parallel"),
        name="out_ln",
    )(y, w, x, g, b)


def _mlp_kernel(x_ref, w1_ref, w2_ref, g_ref, b_ref, o_ref, *, nf):
    j = pl.program_id(1)

    @pl.when(j == 0)
    def _():
        o_ref[...] = DN_ALPHA * x_ref[...]

    h = _mm(x_ref[...].astype(MXU_DTYPE), w1_ref[...].astype(MXU_DTYPE))
    h = jnp.square(jnp.maximum(h, 0.0)).astype(MXU_DTYPE)
    slab = o_ref.shape[1] // MLP_OUT_SLABS
    for s in range(MLP_OUT_SLABS):
        cols = slice(s * slab, (s + 1) * slab)
        o_ref[:, cols] += _mm(h, w2_ref[:, cols].astype(MXU_DTYPE))

    @pl.when(j == nf - 1)
    def _():
        o_ref[...] = _layer_norm(o_ref[...], g_ref[...], b_ref[...])


def _mlp(x, w1_stack, w2_stack, layer, ln, tm, tf):
    m, d = x.shape
    f = w1_stack.shape[2]
    nf = f // tf
    g, b, ln_layer = ln
    return pl.pallas_call(
        functools.partial(_mlp_kernel, nf=nf),
        grid=(m // tm, nf),
        in_specs=[pl.BlockSpec((tm, d), lambda i, j: (i, 0)),
                  pl.BlockSpec((None, d, tf), lambda i, j: (layer, 0, j)),
                  pl.BlockSpec((None, tf, d), lambda i, j: (layer, j, 0)),
                  _layer_row((1, d), ln_layer),
                  _layer_row((1, d), ln_layer)],
        out_specs=pl.BlockSpec((tm, d), lambda i, j: (i, 0)),
        out_shape=jax.ShapeDtypeStruct((m, d), F32),
        compiler_params=_params("parallel", "arbitrary"),
        name="mlp",
    )(x, w1_stack, w2_stack, g, b)


def _s5_in_kernel(x_ref, w_ref, o_ref, wb_ref):
    @pl.when(pl.program_id(0) == 0)
    def _():
        wb_ref[...] = w_ref[...].astype(MXU_DTYPE)

    u = _mm(x_ref[...].astype(MXU_DTYPE), wb_ref[...])
    for gs in range(S5_SETS):
        o_ref[gs] = u[:, gs * LANES:(gs + 1) * LANES]


def _s5_in(x, w, layer, tm):
    m = x.shape[0]
    return pl.pallas_call(
        _s5_in_kernel,
        grid=(m // tm,),
        in_specs=[pl.BlockSpec((tm, D_MODEL), lambda i: (i, 0)),
                  _layer_row((D_MODEL, D_MODEL), layer)],
        out_specs=pl.BlockSpec((S5_SETS, tm, LANES), lambda i: (0, i, 0)),
        out_shape=jax.ShapeDtypeStruct((S5_SETS, m, LANES), F32),
        scratch_shapes=[pltpu.VMEM((D_MODEL, D_MODEL), MXU_DTYPE)],
        compiler_params=_params("arbitrary"),
        name="s5_in",
    )(x, w)


def _s5_core_kernel(u_ref, lb_ref, rc_ref, cre_ref, cim_ref, a16_ref, d_ref, o_ref,
                    wt_ref, win_ref, wout_ref, v_ref, sp_ref, yin_ref, *, batch, chunks_per_batch):
    nst = S5_SET_STATE
    kdim = S5_CHUNK * LANES
    lb = lb_ref[0]

    rc = rc_ref[0]
    lb_hi = lb.astype(MXU_DTYPE)
    lb_lo = (lb - lb_hi.astype(F32)).astype(MXU_DTYPE)
    rc_hi = rc.astype(MXU_DTYPE)
    rc_lo = (rc - rc_hi.astype(F32)).astype(MXU_DTYPE)
    drev = _mm(jnp.concatenate([lb_hi, lb_lo], axis=1), jnp.concatenate([rc_hi, rc_hi], axis=0))
    drev = drev + _mm(lb_hi, rc_lo)
    row = lax.broadcasted_iota(jnp.int32, (kdim, LANES), 0)
    lane = lax.broadcasted_iota(jnp.int32, (kdim, LANES), 1)
    row_group = (row % LANES) // S5_GROUP
    drev = jnp.where(row_group == lane // S5_GROUP, drev, 0.0).astype(MXU_DTYPE)
    for t in range(S5_CHUNK):
        used = (t + 1) * LANES
        group_rows = (t // S5_TOEPLITZ_GROUP + 1) * S5_TOEPLITZ_GROUP * LANES
        wt_ref[0:used, t * LANES:(t + 1) * LANES] = drev[kdim - used:, :]
        if used < group_rows:
            wt_ref[used:group_rows, t * LANES:(t + 1) * LANES] = jnp.zeros((group_rows - used, LANES), MXU_DTYPE)

    swapped = pltpu.roll(lb, S5_STATE, axis=1)
    low = lane < S5_STATE
    re_dup = jnp.where(low, lb, swapped)
    im_dup = jnp.where(low, swapped, lb)
    for v in range(nst // LANES):
        sel = row_group == 2 * v + (lane >= S5_STATE).astype(jnp.int32)
        win_ref[:, v * LANES:(v + 1) * LANES] = jnp.where(sel, re_dup, 0.0).astype(MXU_DTYPE)
        win_ref[:, nst + v * LANES:nst + (v + 1) * LANES] = jnp.where(sel, im_dup, 0.0).astype(MXU_DTYPE)

    lane_group = (lax.broadcasted_iota(jnp.int32, (S5_STATE, kdim), 1) % LANES) // S5_GROUP
    for ri, part in enumerate((cre_ref[0], -cim_ref[0])):
        for g in range(S5_SET_GROUPS):
            r0 = ri * nst + g * S5_STATE
            wout_ref[r0:r0 + S5_STATE, :] = jnp.where(lane_group == g, part, jnp.zeros_like(part))

    nchunks = batch * chunks_per_batch
    u = jnp.concatenate([u_ref[pl.ds(t, nchunks, stride=S5_CHUNK), :] for t in range(S5_CHUNK)], axis=1)
    xb = u.astype(MXU_DTYPE)
    vin = _mm(xb, win_ref[...])
    ntile = nst // LANES
    for k in range(2 * ntile):
        v_ref[k] = vin[:, k * LANES:(k + 1) * LANES]
    gcols = S5_TOEPLITZ_GROUP * LANES
    for j in range(S5_CHUNK // S5_TOEPLITZ_GROUP):
        cols = slice(j * gcols, (j + 1) * gcols)
        krows = (j + 1) * gcols
        yin_ref[:, cols] = _mm(xb[:, :krows], wt_ref[0:krows, cols]) + d_ref[0][:, cols] * u[:, cols]

    a_re = [a16_ref[0, :, k * LANES:(k + 1) * LANES] for k in range(ntile)]
    a_im = [a16_ref[0, :, nst + k * LANES:nst + (k + 1) * LANES] for k in range(ntile)]

    def step(c, carry):
        rows = pl.ds(c, batch, stride=chunks_per_batch)
        new = []
        for k in range(ntile):
            s_re, s_im = carry[k], carry[ntile + k]
            sp_ref[k, rows, :] = s_re
            sp_ref[ntile + k, rows, :] = s_im
            new.append((a_re[k] * s_re - a_im[k] * s_im + v_ref[k, rows, :],
                        a_re[k] * s_im + a_im[k] * s_re + v_ref[ntile + k, rows, :]))
        return tuple(n[0] for n in new) + tuple(n[1] for n in new)

    zero = jnp.zeros((batch, LANES), F32)
    lax.fori_loop(0, chunks_per_batch, step, (zero,) * (2 * ntile), unroll=S5_SCAN_UNROLL)

    sp = jnp.concatenate([sp_ref[k] for k in range(2 * ntile)], axis=1)
    h = jax.nn.gelu(yin_ref[...] + _mm(sp.astype(MXU_DTYPE), wout_ref[...]))
    for t in range(S5_CHUNK):
        o_ref[pl.ds(t, nchunks, stride=S5_CHUNK), :] = h[:, t * LANES:(t + 1) * LANES]


def _s5_core(ug, tables, layer, batch):
    nsets, m, _ = ug.shape
    nchunks = m // S5_CHUNK
    kdim = S5_CHUNK * LANES
    kern = functools.partial(_s5_core_kernel, batch=batch, chunks_per_batch=nchunks // batch)
    table = lambda rows, cols: pl.BlockSpec((None, 1, rows, cols), lambda s: (layer, s, 0, 0))
    return pl.pallas_call(
        kern,
        grid=(nsets,),
        in_specs=[pl.BlockSpec((None, m, LANES), lambda s: (s, 0, 0)),
                  table(kdim, LANES),
                  table(LANES, LANES),
                  table(S5_STATE, kdim),
                  table(S5_STATE, kdim),
                  table(1, 2 * S5_SET_STATE),
                  table(1, kdim)],
        out_specs=pl.BlockSpec((None, m, LANES), lambda s: (s, 0, 0)),
        out_shape=jax.ShapeDtypeStruct((nsets, m, LANES), F32),
        scratch_shapes=[pltpu.VMEM((kdim, kdim), MXU_DTYPE),
                        pltpu.VMEM((kdim, 2 * S5_SET_STATE), MXU_DTYPE),
                        pltpu.VMEM((2 * S5_SET_STATE, kdim), MXU_DTYPE),
                        pltpu.VMEM((2 * S5_SET_STATE // LANES, nchunks, LANES), F32),
                        pltpu.VMEM((2 * S5_SET_STATE // LANES, nchunks, LANES), F32),
                        pltpu.VMEM((nchunks, kdim), F32)],
        compiler_params=_params("parallel"),
        name="s5_core",
    )(ug, *tables)


def _s5_out_kernel(h_ref, wo_ref, wg_ref, x_ref, g_ref, b_ref, o_ref):
    half = o_ref.shape[0] // ROW_SPLIT
    for r in range(ROW_SPLIT):
        rows = slice(r * half, (r + 1) * half)
        h = jnp.concatenate([h_ref[gs, rows, :] for gs in range(S5_SETS)], axis=1).astype(MXU_DTYPE)
        f = _mm(h, wo_ref[...]) * jax.nn.sigmoid(_mm(h, wg_ref[...]))
        o_ref[rows, :] = _layer_norm(DN_ALPHA * x_ref[rows, :] + f, g_ref[...], b_ref[...])


def _s5_out(hg, wo, wg, w_layer, x, ln, tm):
    m = x.shape[0]
    g, b, layer = ln
    return pl.pallas_call(
        _s5_out_kernel,
        grid=(m // tm,),
        in_specs=[pl.BlockSpec((S5_SETS, tm, LANES), lambda i: (0, i, 0)),
                  _layer_row((D_MODEL, D_MODEL), w_layer),
                  _layer_row((D_MODEL, D_MODEL), w_layer),
                  pl.BlockSpec((tm, D_MODEL), lambda i: (i, 0)),
                  _layer_row((1, D_MODEL), layer),
                  _layer_row((1, D_MODEL), layer)],
        out_specs=pl.BlockSpec((tm, D_MODEL), lambda i: (i, 0)),
        out_shape=jax.ShapeDtypeStruct(x.shape, F32),
        compiler_params=_params("parallel"),
        name="s5_out",
    )(hg, wo, wg, x, g, b)


def _s5_tables(lam_re, lam_im, log_dt, b_re, b_im, c_re, c_im, d_skip):
    lr = lam_re.astype(F32)
    li = lam_im.astype(F32)
    dt = jnp.exp(log_dt.astype(F32))[:, None]
    mag = jnp.exp(lr * dt)
    ar = mag * jnp.cos(li * dt)
    ai = mag * jnp.sin(li * dt)
    den = lr * lr + li * li
    zr = ((ar - 1.0) * lr + ai * li) / den
    zi = (ai * lr - (ar - 1.0) * li) / den
    br_ = b_re.astype(F32)
    bi_ = b_im.astype(F32)
    bbr = zr[..., None] * br_ - zi[..., None] * bi_
    bbi = zr[..., None] * bi_ + zi[..., None] * br_
    k = jnp.arange(S5_CHUNK + 1, dtype=F32)[:, None, None]
    pmag = jnp.exp(k * (lr * dt))
    pr = pmag * jnp.cos(k * (li * dt))
    pi = pmag * jnp.sin(k * (li * dt))

    bt_re = jnp.swapaxes(bbr, 1, 2)
    bt_im = jnp.swapaxes(bbi, 1, 2)
    shape_b = (S5_SETS, 1, S5_SET_GROUPS, S5_GROUP, 2 * S5_STATE)
    b_with_re = jnp.concatenate([bt_re, bt_im], axis=-1).reshape(shape_b)
    b_with_im = jnp.concatenate([-bt_im, bt_re], axis=-1).reshape(shape_b)
    def power_rows(t):
        t = jnp.concatenate([t, t], axis=-1).reshape(S5_CHUNK, S5_SETS, S5_SET_GROUPS, 1, 2 * S5_STATE)
        return t.transpose(1, 0, 2, 3, 4)
    lb = power_rows(pr[S5_CHUNK - 1::-1]) * b_with_re + power_rows(pi[S5_CHUNK - 1::-1]) * b_with_im
    lb = lb.reshape(S5_SETS, S5_CHUNK * LANES, 2 * S5_STATE)

    def rows_p(t):
        return t.astype(F32).reshape(S5_SETS, S5_SET_GROUPS, S5_GROUP, S5_STATE).transpose(0, 3, 1, 2)
    ct_re = rows_p(c_re)
    ct_im = rows_p(c_im)
    rc = jnp.concatenate([ct_re, -ct_im], axis=1).reshape(S5_SETS, 2 * S5_STATE, LANES)

    def power_lanes(t):
        return t.reshape(S5_CHUNK, S5_SETS, S5_SET_GROUPS, S5_STATE).transpose(1, 3, 0, 2)[..., None]
    prs = power_lanes(pr[1:])
    pis = power_lanes(pi[1:])
    ca_re = ct_re[:, :, None] * prs - ct_im[:, :, None] * pis
    ca_im = ct_re[:, :, None] * pis + ct_im[:, :, None] * prs
    ca_re = ca_re.reshape(S5_SETS, S5_STATE, S5_CHUNK * LANES).astype(MXU_DTYPE)
    ca_im = ca_im.reshape(S5_SETS, S5_STATE, S5_CHUNK * LANES).astype(MXU_DTYPE)

    a16 = jnp.concatenate([pr[S5_CHUNK].reshape(S5_SETS, 1, S5_SET_STATE),
                           pi[S5_CHUNK].reshape(S5_SETS, 1, S5_SET_STATE)], axis=2)
    drow = jnp.tile(d_skip.astype(F32).reshape(S5_SETS, 1, LANES), (1, 1, S5_CHUNK))
    return lb, rc, ca_re, ca_im, a16, drow


def _s5_layer(x, batch, w_in, tables, layer, w_out, w_gate, ln):
    ug = _s5_in(x, w_in, layer, tm=512)
    hg = _s5_core(ug, tables, layer, batch)
    return _s5_out(hg, w_out, w_gate, layer, x, ln, tm=512)


def _split_hi_lo(v):
    hi = v.astype(MXU_DTYPE)
    lo = (v - hi.astype(F32)).astype(MXU_DTYPE)
    return jnp.concatenate([hi, lo], axis=1)


def _ssd_core_kernel(z_ref, xs_ref, bc_ref, xin_ref, wdt_ref, cwx_ref, cwb_ref, cbx_ref, cbb_ref,
                     dtb_ref, a_ref, dsk_ref, nw_ref, e2_ref, o_ref,
                     xbuf_ref, bbuf_ref, st_ref, yd_ref):
    q = SSD_CHUNK
    pad = 8

    @pl.when(pl.program_id(1) == 0)
    def _():
        xbuf_ref[...] = jnp.zeros_like(xbuf_ref)
        bbuf_ref[...] = jnp.zeros_like(bbuf_ref)
        st_ref[...] = jnp.zeros_like(st_ref)

    def conv(in_ref, tail_ref, w_ref, b_ref):
        cur = in_ref[...]
        tail = tail_ref[...]
        row8 = lax.broadcasted_iota(jnp.int32, tail.shape, 0)
        acc = b_ref[...] + w_ref[SSD_CONV - 1:SSD_CONV, :] * cur
        for k in range(1, SSD_CONV):
            back = pltpu.roll(cur, k, axis=0)
            head = jnp.where(row8 < k, pltpu.roll(tail, k, axis=0), back[0:pad])
            back = jnp.concatenate([head, back[pad:]], axis=0)
            acc = acc + w_ref[SSD_CONV - 1 - k:SSD_CONV - k, :] * back
        tail_ref[...] = cur[q - pad:, :]
        return _silu(acc)

    xs = conv(xs_ref, xbuf_ref, cwx_ref, cbx_ref)
    bc = conv(bc_ref, bbuf_ref, cwb_ref, cbb_ref)

    dtr = _mm(xin_ref[...].astype(MXU_DTYPE), wdt_ref[...]) + dtb_ref[...]
    dt = jnp.maximum(dtr, 0.0) + jnp.log1p(jnp.exp(-jnp.abs(dtr)))
    adt = dt * a_ref[...]
    ri = lax.broadcasted_iota(jnp.int32, (q, q), 0)
    ci = lax.broadcasted_iota(jnp.int32, (q, q), 1)
    causal = ri >= ci
    tri = causal.astype(F32)
    cs = jnp.dot(tri, adt, preferred_element_type=F32, precision=HI)
    eye = (ri == ci).astype(F32)
    cs2 = cs * LOG2_E
    r_rows = _mm_nt(eye, cs2, precision=HI) - jnp.log2(_mm_nt(eye, dt, precision=HI))
    tot = cs[q - 1:q, :]
    w_col = dt * jnp.exp(tot - cs)
    din = jnp.exp(cs)
    w_x = _mm(_split_hi_lo(w_col), e2_ref[...])
    din_x = _mm(_split_hi_lo(din), e2_ref[...])

    lane = lax.broadcasted_iota(jnp.int32, (q, LANES), 1)
    low = lane < SSD_HEADDIM

    xw = (xs * w_x).astype(MXU_DTYPE)
    xsb = xs.astype(MXU_DTYPE)
    for g in range(SSD_GROUPS):
        bg = bc[:, g * SSD_STATE:(g + 1) * SSD_STATE].astype(MXU_DTYPE)
        cg = bc[:, SSD_GROUPS * SSD_STATE + g * SSD_STATE:
                SSD_GROUPS * SSD_STATE + (g + 1) * SSD_STATE].astype(MXU_DTYPE)
        cb = _mm_nt(cg, bg)
        for pair in range(SSD_HEADS_PER_GROUP // 2):
            h0 = g * SSD_HEADS_PER_GROUP + 2 * pair
            ms = []
            for h in (h0, h0 + 1):
                seg = cs2[:, h:h + 1] - r_rows[h:h + 1, :]
                ms.append((cb * jnp.exp2(jnp.where(causal, seg, -jnp.inf))).astype(MXU_DTYPE))
            lhs = jnp.concatenate(ms, axis=1)
            xp = xsb[:, h0 * SSD_HEADDIM:(h0 + 2) * SSD_HEADDIM]
            zero = jnp.zeros_like(xp)
            rhs = jnp.concatenate([jnp.where(low, xp, zero), jnp.where(low, zero, xp)], axis=0)
            yd_ref[:, h0 * SSD_HEADDIM:(h0 + 2) * SSD_HEADDIM] = _mm(lhs, rhs)
        gsl = slice(g * SSD_GROUP_DIM, (g + 1) * SSD_GROUP_DIM)
        st = st_ref[g]
        y_off = _mm(cg, st.astype(MXU_DTYPE)) * din_x[:, gsl]
        yd_ref[:, gsl] = yd_ref[:, gsl] + y_off
        st_ref[g] = din_x[q - 1:q, gsl] * st + _mm_tn(bg, xw[:, gsl])

    y = yd_ref[...] + dsk_ref[...] * xs
    y = y * z_ref[...]
    for g in range(SSD_GROUPS):
        gsl = slice(g * SSD_GROUP_DIM, (g + 1) * SSD_GROUP_DIM)
        yg = y[:, gsl]
        yg = yg * lax.rsqrt(jnp.mean(yg * yg, axis=-1, keepdims=True) + LN_EPS)
        o_ref[:, gsl] = (yg * nw_ref[:, gsl]).astype(o_ref.dtype)


def _ssd_core(proj, x, w_dt, batch, conv_w, conv_b, dt_bias, a_log, d_skip, norm_w):
    m = proj.shape[0]
    q = SSD_CHUNK
    ncb = m // batch // q
    di = SSD_D_INNER
    wdt = jnp.pad(w_dt.astype(MXU_DTYPE), ((0, 0), (0, LANES - SSD_HEADS)))
    cw = conv_w.astype(F32)
    cbias = conv_b.astype(F32)[None, :]
    padh = LANES - SSD_HEADS
    dtb = jnp.pad(dt_bias.astype(F32), (0, padh))[None, :]
    a = jnp.pad(-jnp.exp(a_log.astype(F32)), (0, padh))[None, :]
    dsk = jnp.repeat(d_skip.astype(F32), SSD_HEADDIM)[None, :]
    nw = norm_w.astype(F32)[None, :]
    expand = (np.arange(LANES)[:, None] == (np.arange(di) // SSD_HEADDIM)[None, :]).astype(np.float32)
    e2 = jnp.asarray(np.concatenate([expand, expand], axis=0), dtype=MXU_DTYPE)

    row = lambda b, c: b * ncb + c
    full = _resident
    return pl.pallas_call(
        _ssd_core_kernel,
        grid=(batch, ncb),
        in_specs=[pl.BlockSpec((q, di), lambda b, c: (row(b, c), 0)),
                  pl.BlockSpec((q, di), lambda b, c: (row(b, c), 1)),
                  pl.BlockSpec((q, SSD_BC), lambda b, c: (row(b, c), 2 * di // SSD_BC)),
                  pl.BlockSpec((q, D_MODEL), lambda b, c: (row(b, c), 0)),
                  full((D_MODEL, LANES)),
                  full((SSD_CONV, di)), full((SSD_CONV, SSD_BC)), full((1, di)), full((1, SSD_BC)),
                  full((1, LANES)), full((1, LANES)), full((1, di)), full((1, di)),
                  full((2 * LANES, di))],
        out_specs=pl.BlockSpec((q, di), lambda b, c: (row(b, c), 0)),
        out_shape=jax.ShapeDtypeStruct((m, di), MXU_DTYPE),
        scratch_shapes=[pltpu.VMEM((8, di), F32), pltpu.VMEM((8, SSD_BC), F32),
                        pltpu.VMEM((SSD_GROUPS, SSD_STATE, SSD_GROUP_DIM), F32),
                        pltpu.VMEM((q, di), F32)],
        compiler_params=_params("parallel", "arbitrary"),
        name="ssd_core",
    )(proj, proj, proj, x, wdt, cw[:, :di], cw[:, di:], cbias[:, :di], cbias[:, di:], dtb, a, dsk, nw, e2)


def _ssd_layer(x, batch, w_in_stack, layer, conv_w, conv_b, dt_bias, a_log, d_skip, norm_w, w_out, ln):
    n_main = SSD_D_INNER + SSD_CONV_DIM
    proj = _proj(x, jnp.swapaxes(w_in_stack, 1, 2), layer, n_main, tm=min(1024, x.shape[0]), tn=1024,
                 out_dtype=F32, silu_cols=SSD_D_INNER, w_transposed=True)
    w_dt = w_in_stack[layer, :, n_main:]
    y = _ssd_core(proj, x, w_dt, batch, conv_w, conv_b, dt_bias, a_log, d_skip, norm_w)
    return _out_ln(y, w_out.astype(MXU_DTYPE), x, ln, tm=512)


def _ret_core_kernel(cd_ref, q_ref, k_ref, v_ref, g_ref, cos_ref, sin_ref, dmat_ref, xi_ref, zeta_ref,
                     o_ref, st_ref):
    @pl.when(pl.program_id(1) == 0)
    def _():
        st_ref[...] = jnp.zeros_like(st_ref)

    cos = cos_ref[...]
    sin = sin_ref[...]
    half = RET_DK // 2

    def rotate(t):
        t1 = t[:, :half]
        t2 = t[:, half:]
        return jnp.concatenate([t1 * cos - t2 * sin, t1 * sin + t2 * cos], axis=1)

    for h in range(RET_HEADS):
        qh = rotate(q_ref[:, h * RET_DK:(h + 1) * RET_DK].astype(F32))
        kh = rotate(k_ref[:, h * RET_DK:(h + 1) * RET_DK].astype(F32) * (RET_DK ** -0.5))
        vh = v_ref[:, h * RET_DV:(h + 1) * RET_DV].astype(MXU_DTYPE)
        qb = qh.astype(MXU_DTYPE)
        scores = _mm_nt(qb, kh.astype(MXU_DTYPE)) * dmat_ref[h]
        inner = _mm(scores.astype(MXU_DTYPE), vh)
        st = st_ref[h]
        xi = jnp.concatenate([xi_ref[h]] * (RET_DV // LANES), axis=1)
        cross = _mm(qb, st.astype(MXU_DTYPE)) * xi
        kz = kh * jnp.concatenate([zeta_ref[h]] * (RET_DK // LANES), axis=1)
        st_ref[h] = cd_ref[h] * st + _mm_tn(kz.astype(MXU_DTYPE), vh)
        out = inner + cross
        mu = jnp.mean(out, axis=-1, keepdims=True)
        d = out - mu
        var = jnp.mean(d * d, axis=-1, keepdims=True)
        out = d * lax.rsqrt(var + LN_EPS)
        gate = _silu(g_ref[:, h * RET_DV:(h + 1) * RET_DV].astype(F32))
        o_ref[:, h * RET_DV:(h + 1) * RET_DV] = (gate * out).astype(o_ref.dtype)


def _ret_core(proj, batch):
    m = proj.shape[0]
    seq = m // batch
    q = RET_CHUNK
    ncb = seq // q
    theta = (1.0 / (RET_ROPE_BASE ** np.linspace(0.0, 1.0, RET_DK // 2))).astype(np.float32)
    ang = (np.arange(seq, dtype=np.float32)[:, None] * theta[None, :]).astype(np.float64)
    cos, sin = np.cos(ang).astype(np.float32), np.sin(ang).astype(np.float32)
    lg = np.log1p(-np.exp2(-5.0 - np.arange(RET_HEADS)))
    pos = np.arange(q, dtype=np.float64)
    diff = pos[:, None] - pos[None, :]
    dmat = np.where(diff >= 0, np.exp(np.maximum(diff, 0.0)[None] * lg[:, None, None]), 0.0).astype(np.float32)
    xi = np.exp((pos[None, :] + 1.0) * lg[:, None])
    zeta = np.exp((q - 1.0 - pos[None, :]) * lg[:, None])
    xi_b = np.broadcast_to(xi[:, :, None], (RET_HEADS, q, LANES)).astype(np.float32)
    zeta_b = np.broadcast_to(zeta[:, :, None], (RET_HEADS, q, LANES)).astype(np.float32)
    chunk_decay = np.exp(q * lg).astype(np.float32)

    row = lambda b, c: b * ncb + c
    d = D_MODEL
    return pl.pallas_call(
        _ret_core_kernel,
        grid=(batch, ncb),
        in_specs=[pl.BlockSpec(memory_space=pltpu.SMEM),
                  pl.BlockSpec((q, d), lambda b, c: (row(b, c), 0)),
                  pl.BlockSpec((q, d), lambda b, c: (row(b, c), 1)),
                  pl.BlockSpec((q, 2 * d), lambda b, c: (row(b, c), 1)),
                  pl.BlockSpec((q, 2 * d), lambda b, c: (row(b, c), 2)),
                  pl.BlockSpec((q, RET_DK // 2), lambda b, c: (c, 0)),
                  pl.BlockSpec((q, RET_DK // 2), lambda b, c: (c, 0)),
                  _resident((RET_HEADS, q, q)),
                  _resident((RET_HEADS, q, LANES)),
                  _resident((RET_HEADS, q, LANES))],
        out_specs=pl.BlockSpec((q, 2 * d), lambda b, c: (row(b, c), 0)),
        out_shape=jax.ShapeDtypeStruct((m, 2 * d), MXU_DTYPE),
        scratch_shapes=[pltpu.VMEM((RET_HEADS, RET_DK, RET_DV), F32)],
        compiler_params=_params("parallel", "arbitrary"),
        name="ret_core",
    )(chunk_decay, proj, proj, proj, proj, cos, sin, dmat, xi_b, zeta_b)


def _ret_layer(x, batch, w_in_stack, layer, w_out, ln):
    proj = _proj(x, w_in_stack, layer, RET_IN_DIM, tm=min(1024, x.shape[0]), tn=1024, out_dtype=MXU_DTYPE)
    y = _ret_core(proj, batch)
    return _out_ln(y, w_out.astype(MXU_DTYPE), x, ln, tm=512)


def kernel(x, ln1_g, ln1_b, ln2_g, ln2_b, mlp_w1, mlp_w2, s5_w_in, s5_lam_re, s5_lam_im, s5_log_dt, s5_b_re, s5_b_im, s5_c_re, s5_c_im, s5_d, s5_w_out, s5_w_gate, ssd_w_in, ssd_conv_w, ssd_conv_b, ssd_dt_bias, ssd_a_log, ssd_d, ssd_norm_w, ssd_w_out, ret_w_in, ret_w_out):
    batch, seq, d = x.shape
    h = x.reshape(batch * seq, d)
    ln1 = (ln1_g[:, None, :], ln1_b[:, None, :])
    ln2 = (ln2_g[:, None, :], ln2_b[:, None, :])
    s5_tables = jax.vmap(_s5_tables)(s5_lam_re, s5_lam_im, s5_log_dt, s5_b_re, s5_b_im, s5_c_re, s5_c_im, s5_d)
    s5_w_out_b = s5_w_out.astype(MXU_DTYPE)
    s5_w_gate_b = s5_w_gate.astype(MXU_DTYPE)
    for i in range(DEPTH):
        kind = i % 3
        j = i // 3
        if kind == 0:
            h = _s5_layer(h, batch, s5_w_in, s5_tables, j, s5_w_out_b, s5_w_gate_b, ln1 + (i,))
        elif kind == 1:
            h = _ssd_layer(h, batch, ssd_w_in, j, ssd_conv_w[j], ssd_conv_b[j], ssd_dt_bias[j], ssd_a_log[j],
                           ssd_d[j], ssd_norm_w[j], ssd_w_out[j], ln1 + (i,))
        else:
            h = _ret_layer(h, batch, ret_w_in, j, ret_w_out[j], ln1 + (i,))
        h = _mlp(h, mlp_w1, mlp_w2, i, ln2 + (i,), tm=min(1024, h.shape[0]), tf=512)
    return h.reshape(batch, seq, d)
```

```python
import functools
import math

import jax
import jax.numpy as jnp
import numpy as np
from jax import lax
from jax.experimental import pallas as pl
from jax.experimental.pallas import tpu as pltpu

F32 = jnp.float32
MXU_DTYPE = jnp.bfloat16
HI = lax.Precision.HIGHEST
LOG2_E = 1.0 / math.log(2.0)

LANES = 128
VMEM_PHYSICAL_BYTES = 64 * 1024 * 1024
VMEM_LIMIT_BYTES = VMEM_PHYSICAL_BYTES - 6 * 1024 * 1024

D_MODEL = 2048
DEPTH = 4
DN_ALPHA = (2 * DEPTH) ** 0.25
LN_EPS = 1e-5
D_FF = 4 * D_MODEL
MLP_OUT_SLABS = 4
ROW_SPLIT = 2

S5_GROUP = 16
S5_GROUPS = D_MODEL // S5_GROUP
S5_STATE = 64
S5_CHUNK = 16
S5_SETS = D_MODEL // LANES
S5_SET_GROUPS = LANES // S5_GROUP
S5_SET_STATE = S5_SET_GROUPS * S5_STATE
S5_SCAN_UNROLL = 8
S5_TOEPLITZ_GROUP = 2

SSD_D_INNER = 2 * D_MODEL
SSD_HEADDIM = 64
SSD_HEADS = SSD_D_INNER // SSD_HEADDIM
SSD_GROUPS = 8
SSD_STATE = 128
SSD_CONV = 4
SSD_CHUNK = 128
SSD_BC = 2 * SSD_GROUPS * SSD_STATE
SSD_CONV_DIM = SSD_D_INNER + SSD_BC
SSD_GROUP_DIM = SSD_D_INNER // SSD_GROUPS
SSD_HEADS_PER_GROUP = SSD_HEADS // SSD_GROUPS

RET_HEADS = 8
RET_DK = D_MODEL // RET_HEADS
RET_DV = 2 * D_MODEL // RET_HEADS
RET_CHUNK = 256
RET_IN_DIM = 6 * D_MODEL
RET_ROPE_BASE = 10000.0


def _params(*semantics):
    return pltpu.CompilerParams(dimension_semantics=semantics, vmem_limit_bytes=VMEM_LIMIT_BYTES)


def _layer_norm(v, g, b):
    mu = jnp.mean(v, axis=-1, keepdims=True)
    d = v - mu
    var = jnp.mean(d * d, axis=-1, keepdims=True)
    return d * lax.rsqrt(var + LN_EPS) * g + b


def _silu(v):
    h = 0.5 * v
    return h + h * jnp.tanh(h)


def _mm(a, b):
    return jnp.dot(a, b, preferred_element_type=F32)


def _mm_nt(a, b, precision=None):
    return lax.dot_general(a, b, (((1,), (1,)), ((), ())), preferred_element_type=F32, precision=precision)


def _mm_tn(a, b):
    return lax.dot_general(a, b, (((0,), (0,)), ((), ())), preferred_element_type=F32)


def _resident(shape):
    return pl.BlockSpec(shape, lambda *_: (0,) * len(shape), pipeline_mode=pl.Buffered(1))


def _layer_row(shape, layer):
    return pl.BlockSpec((None,) + shape, lambda *_: (layer, 0, 0), pipeline_mode=pl.Buffered(1))


def _proj_kernel(x_ref, w_ref, o_ref, *, act_tiles, w_transposed):
    j = pl.program_id(1)

    def product():
        xb = x_ref[...].astype(MXU_DTYPE)
        wb = w_ref[...].astype(MXU_DTYPE)
        return _mm_nt(xb, wb) if w_transposed else _mm(xb, wb)

    if act_tiles == 0:
        o_ref[...] = product().astype(o_ref.dtype)
    else:
        @pl.when(j < act_tiles)
        def _():
            o_ref[...] = _silu(product()).astype(o_ref.dtype)

        @pl.when(j >= act_tiles)
        def _():
            o_ref[...] = product().astype(o_ref.dtype)


def _proj(x, w_stack, layer, n, tm, tn, out_dtype, silu_cols=0, w_transposed=False):
    m, k = x.shape
    assert silu_cols % tn == 0
    w_spec = (pl.BlockSpec((None, tn, k), lambda i, j: (layer, j, 0)) if w_transposed
              else pl.BlockSpec((None, k, tn), lambda i, j: (layer, 0, j)))
    return pl.pallas_call(
        functools.partial(_proj_kernel, act_tiles=silu_cols // tn, w_transposed=w_transposed),
        grid=(m // tm, n // tn),
        in_specs=[pl.BlockSpec((tm, k), lambda i, j: (i, 0)), w_spec],
        out_specs=pl.BlockSpec((tm, tn), lambda i, j: (i, j)),
        out_shape=jax.ShapeDtypeStruct((m, n), out_dtype),
        compiler_params=_params("parallel", "arbitrary"),
        name="proj",
    )(x, w_stack)


def _out_ln_kernel(y_ref, w_ref, x_ref, g_ref, b_ref, o_ref):
    half = o_ref.shape[0] // ROW_SPLIT
    for r in range(ROW_SPLIT):
        rows = slice(r * half, (r + 1) * half)
        f = _mm(y_ref[rows, :], w_ref[...])
        o_ref[rows, :] = _layer_norm(DN_ALPHA * x_ref[rows, :] + f, g_ref[...], b_ref[...])


def _out_ln(y, w, x, ln, tm):
    m, k = y.shape
    d = w.shape[1]
    g, b, layer = ln
    return pl.pallas_call(
        _out_ln_kernel,
        grid=(m // tm,),
        in_specs=[pl.BlockSpec((tm, k), lambda i: (i, 0)),
                  _resident((k, d)),
                  pl.BlockSpec((tm, d), lambda i: (i, 0)),
                  _layer_row((1, d), layer),
                  _layer_row((1, d), layer)],
        out_specs=pl.BlockSpec((tm, d), lambda i: (i, 0)),
        out_shape=jax.ShapeDtypeStruct((m, d), F32),
        compiler_params=_params("parallel"),
        name="out_ln",
    )(y, w, x, g, b)


def _mlp_kernel(x_ref, w1_ref, w2_ref, g_ref, b_ref, o_ref, *, nf):
    j = pl.program_id(1)

    @pl.when(j == 0)
    def _():
        o_ref[...] = DN_ALPHA * x_ref[...]

    h = _mm(x_ref[...].astype(MXU_DTYPE), w1_ref[...].astype(MXU_DTYPE))
    h = jnp.square(jnp.maximum(h, 0.0)).astype(MXU_DTYPE)
    slab = o_ref.shape[1] // MLP_OUT_SLABS
    for s in range(MLP_OUT_SLABS):
        cols = slice(s * slab, (s + 1) * slab)
        o_ref[:, cols] += _mm(h, w2_ref[:, cols].astype(MXU_DTYPE))

    @pl.when(j == nf - 1)
    def _():
        o_ref[...] = _layer_norm(o_ref[...], g_ref[...], b_ref[...])


def _mlp(x, w1_stack, w2_stack, layer, ln, tm, tf):
    m, d = x.shape
    f = w1_stack.shape[2]
    nf = f // tf
    g, b, ln_layer = ln
    return pl.pallas_call(
        functools.partial(_mlp_kernel, nf=nf),
        grid=(m // tm, nf),
        in_specs=[pl.BlockSpec((tm, d), lambda i, j: (i, 0)),
                  pl.BlockSpec((None, d, tf), lambda i, j: (layer, 0, j)),
                  pl.BlockSpec((None, tf, d), lambda i, j: (layer, j, 0)),
                  _layer_row((1, d), ln_layer),
                  _layer_row((1, d), ln_layer)],
        out_specs=pl.BlockSpec((tm, d), lambda i, j: (i, 0)),
        out_shape=jax.ShapeDtypeStruct((m, d), F32),
        compiler_params=_params("parallel", "arbitrary"),
        name="mlp",
    )(x, w1_stack, w2_stack, g, b)


def _s5_in_kernel(x_ref, w_ref, o_ref, wb_ref):
    @pl.when(pl.program_id(0) == 0)
    def _():
        wb_ref[...] = w_ref[...].astype(MXU_DTYPE)

    u = _mm(x_ref[...].astype(MXU_DTYPE), wb_ref[...])
    for gs in range(S5_SETS):
        o_ref[gs] = u[:, gs * LANES:(gs + 1) * LANES]


def _s5_in(x, w, layer, tm):
    m = x.shape[0]
    return pl.pallas_call(
        _s5_in_kernel,
        grid=(m // tm,),
        in_specs=[pl.BlockSpec((tm, D_MODEL), lambda i: (i, 0)),
                  _layer_row((D_MODEL, D_MODEL), layer)],
        out_specs=pl.BlockSpec((S5_SETS, tm, LANES), lambda i: (0, i, 0)),
        out_shape=jax.ShapeDtypeStruct((S5_SETS, m, LANES), F32),
        scratch_shapes=[pltpu.VMEM((D_MODEL, D_MODEL), MXU_DTYPE)],
        compiler_params=_params("arbitrary"),
        name="s5_in",
    )(x, w)


def _s5_core_kernel(u_ref, lb_ref, rc_ref, cre_ref, cim_ref, a16_ref, d_ref, o_ref,
                    wt_ref, win_ref, wout_ref, v_ref, sp_ref, yin_ref, *, batch, chunks_per_batch):
    nst = S5_SET_STATE
    kdim = S5_CHUNK * LANES
    lb = lb_ref[0]

    rc = rc_ref[0]
    lb_hi = lb.astype(MXU_DTYPE)
    lb_lo = (lb - lb_hi.astype(F32)).astype(MXU_DTYPE)
    rc_hi = rc.astype(MXU_DTYPE)
    rc_lo = (rc - rc_hi.astype(F32)).astype(MXU_DTYPE)
    drev = _mm(jnp.concatenate([lb_hi, lb_lo], axis=1), jnp.concatenate([rc_hi, rc_hi], axis=0))
    drev = drev + _mm(lb_hi, rc_lo)
    row = lax.broadcasted_iota(jnp.int32, (kdim, LANES), 0)
    lane = lax.broadcasted_iota(jnp.int32, (kdim, LANES), 1)
    row_group = (row % LANES) // S5_GROUP
    drev = jnp.where(row_group == lane // S5_GROUP, drev, 0.0).astype(MXU_DTYPE)
    for t in range(S5_CHUNK):
        used = (t + 1) * LANES
        group_rows = (t // S5_TOEPLITZ_GROUP + 1) * S5_TOEPLITZ_GROUP * LANES
        wt_ref[0:used, t * LANES:(t + 1) * LANES] = drev[kdim - used:, :]
        if used < group_rows:
            wt_ref[used:group_rows, t * LANES:(t + 1) * LANES] = jnp.zeros((group_rows - used, LANES), MXU_DTYPE)

    swapped = pltpu.roll(lb, S5_STATE, axis=1)
    low = lane < S5_STATE
    re_dup = jnp.where(low, lb, swapped)
    im_dup = jnp.where(low, swapped, lb)
    for v in range(nst // LANES):
        sel = row_group == 2 * v + (lane >= S5_STATE).astype(jnp.int32)
        win_ref[:, v * LANES:(v + 1) * LANES] = jnp.where(sel, re_dup, 0.0).astype(MXU_DTYPE)
        win_ref[:, nst + v * LANES:nst + (v + 1) * LANES] = jnp.where(sel, im_dup, 0.0).astype(MXU_DTYPE)

    lane_group = (lax.broadcasted_iota(jnp.int32, (S5_STATE, kdim), 1) % LANES) // S5_GROUP
    for ri, part in enumerate((cre_ref[0], -cim_ref[0])):
        for g in range(S5_SET_GROUPS):
            r0 = ri * nst + g * S5_STATE
            wout_ref[r0:r0 + S5_STATE, :] = jnp.where(lane_group == g, part, jnp.zeros_like(part))

    nchunks = batch * chunks_per_batch
    u = jnp.concatenate([u_ref[pl.ds(t, nchunks, stride=S5_CHUNK), :] for t in range(S5_CHUNK)], axis=1)
    xb = u.astype(MXU_DTYPE)
    vin = _mm(xb, win_ref[...])
    ntile = nst // LANES
    for k in range(2 * ntile):
        v_ref[k] = vin[:, k * LANES:(k + 1) * LANES]
    gcols = S5_TOEPLITZ_GROUP * LANES
    for j in range(S5_CHUNK // S5_TOEPLITZ_GROUP):
        cols = slice(j * gcols, (j + 1) * gcols)
        krows = (j + 1) * gcols
        yin_ref[:, cols] = _mm(xb[:, :krows], wt_ref[0:krows, cols]) + d_ref[0][:, cols] * u[:, cols]

    a_re = [a16_ref[0, :, k * LANES:(k + 1) * LANES] for k in range(ntile)]
    a_im = [a16_ref[0, :, nst + k * LANES:nst + (k + 1) * LANES] for k in range(ntile)]

    def step(c, carry):
        rows = pl.ds(c, batch, stride=chunks_per_batch)
        new = []
        for k in range(ntile):
            s_re, s_im = carry[k], carry[ntile + k]
            sp_ref[k, rows, :] = s_re
            sp_ref[ntile + k, rows, :] = s_im
            new.append((a_re[k] * s_re - a_im[k] * s_im + v_ref[k, rows, :],
                        a_re[k] * s_im + a_im[k] * s_re + v_ref[ntile + k, rows, :]))
        return tuple(n[0] for n in new) + tuple(n[1] for n in new)

    zero = jnp.zeros((batch, LANES), F32)
    lax.fori_loop(0, chunks_per_batch, step, (zero,) * (2 * ntile), unroll=S5_SCAN_UNROLL)

    sp = jnp.concatenate([sp_ref[k] for k in range(2 * ntile)], axis=1)
    h = jax.nn.gelu(yin_ref[...] + _mm(sp.astype(MXU_DTYPE), wout_ref[...]))
    for t in range(S5_CHUNK):
        o_ref[pl.ds(t, nchunks, stride=S5_CHUNK), :] = h[:, t * LANES:(t + 1) * LANES]


def _s5_core(ug, tables, layer, batch):
    nsets, m, _ = ug.shape
    nchunks = m // S5_CHUNK
    kdim = S5_CHUNK * LANES
    kern = functools.partial(_s5_core_kernel, batch=batch, chunks_per_batch=nchunks // batch)
    table = lambda rows, cols: pl.BlockSpec((None, 1, rows, cols), lambda s: (layer, s, 0, 0))
    return pl.pallas_call(
        kern,
        grid=(nsets,),
        in_specs=[pl.BlockSpec((None, m, LANES), lambda s: (s, 0, 0)),
                  table(kdim, LANES),
                  table(LANES, LANES),
                  table(S5_STATE, kdim),
                  table(S5_STATE, kdim),
                  table(1, 2 * S5_SET_STATE),
                  table(1, kdim)],
        out_specs=pl.BlockSpec((None, m, LANES), lambda s: (s, 0, 0)),
        out_shape=jax.ShapeDtypeStruct((nsets, m, LANES), F32),
        scratch_shapes=[pltpu.VMEM((kdim, kdim), MXU_DTYPE),
                        pltpu.VMEM((kdim, 2 * S5_SET_STATE), MXU_DTYPE),
                        pltpu.VMEM((2 * S5_SET_STATE, kdim), MXU_DTYPE),
                        pltpu.VMEM((2 * S5_SET_STATE // LANES, nchunks, LANES), F32),
                        pltpu.VMEM((2 * S5_SET_STATE // LANES, nchunks, LANES), F32),
                        pltpu.VMEM((nchunks, kdim), F32)],
        compiler_params=_params("parallel"),
        name="s5_core",
    )(ug, *tables)


def _s5_out_kernel(h_ref, wo_ref, wg_ref, x_ref, g_ref, b_ref, o_ref):
    half = o_ref.shape[0] // ROW_SPLIT
    for r in range(ROW_SPLIT):
        rows = slice(r * half, (r + 1) * half)
        h = jnp.concatenate([h_ref[gs, rows, :] for gs in range(S5_SETS)], axis=1).astype(MXU_DTYPE)
        f = _mm(h, wo_ref[...]) * jax.nn.sigmoid(_mm(h, wg_ref[...]))
        o_ref[rows, :] = _layer_norm(DN_ALPHA * x_ref[rows, :] + f, g_ref[...], b_ref[...])


def _s5_out(hg, wo, wg, w_layer, x, ln, tm):
    m = x.shape[0]
    g, b, layer = ln
    return pl.pallas_call(
        _s5_out_kernel,
        grid=(m // tm,),
        in_specs=[pl.BlockSpec((S5_SETS, tm, LANES), lambda i: (0, i, 0)),
                  _layer_row((D_MODEL, D_MODEL), w_layer),
                  _layer_row((D_MODEL, D_MODEL), w_layer),
                  pl.BlockSpec((tm, D_MODEL), lambda i: (i, 0)),
                  _layer_row((1, D_MODEL), layer),
                  _layer_row((1, D_MODEL), layer)],
        out_specs=pl.BlockSpec((tm, D_MODEL), lambda i: (i, 0)),
        out_shape=jax.ShapeDtypeStruct(x.shape, F32),
        compiler_params=_params("parallel"),
        name="s5_out",
    )(hg, wo, wg, x, g, b)


def _s5_tables(lam_re, lam_im, log_dt, b_re, b_im, c_re, c_im, d_skip):
    lr = lam_re.astype(F32)
    li = lam_im.astype(F32)
    dt = jnp.exp(log_dt.astype(F32))[:, None]
    mag = jnp.exp(lr * dt)
    ar = mag * jnp.cos(li * dt)
    ai = mag * jnp.sin(li * dt)
    den = lr * lr + li * li
    zr = ((ar - 1.0) * lr + ai * li) / den
    zi = (ai * lr - (ar - 1.0) * li) / den
    br_ = b_re.astype(F32)
    bi_ = b_im.astype(F32)
    bbr = zr[..., None] * br_ - zi[..., None] * bi_
    bbi = zr[..., None] * bi_ + zi[..., None] * br_
    k = jnp.arange(S5_CHUNK + 1, dtype=F32)[:, None, None]
    pmag = jnp.exp(k * (lr * dt))
    pr = pmag * jnp.cos(k * (li * dt))
    pi = pmag * jnp.sin(k * (li * dt))

    bt_re = jnp.swapaxes(bbr, 1, 2)
    bt_im = jnp.swapaxes(bbi, 1, 2)
    shape_b = (S5_SETS, 1, S5_SET_GROUPS, S5_GROUP, 2 * S5_STATE)
    b_with_re = jnp.concatenate([bt_re, bt_im], axis=-1).reshape(shape_b)
    b_with_im = jnp.concatenate([-bt_im, bt_re], axis=-1).reshape(shape_b)
    def power_rows(t):
        t = jnp.concatenate([t, t], axis=-1).reshape(S5_CHUNK, S5_SETS, S5_SET_GROUPS, 1, 2 * S5_STATE)
        return t.transpose(1, 0, 2, 3, 4)
    lb = power_rows(pr[S5_CHUNK - 1::-1]) * b_with_re + power_rows(pi[S5_CHUNK - 1::-1]) * b_with_im
    lb = lb.reshape(S5_SETS, S5_CHUNK * LANES, 2 * S5_STATE)

    def rows_p(t):
        return t.astype(F32).reshape(S5_SETS, S5_SET_GROUPS, S5_GROUP, S5_STATE).transpose(0, 3, 1, 2)
    ct_re = rows_p(c_re)
    ct_im = rows_p(c_im)
    rc = jnp.concatenate([ct_re, -ct_im], axis=1).reshape(S5_SETS, 2 * S5_STATE, LANES)

    def power_lanes(t):
        return t.reshape(S5_CHUNK, S5_SETS, S5_SET_GROUPS, S5_STATE).transpose(1, 3, 0, 2)[..., None]
    prs = power_lanes(pr[1:])
    pis = power_lanes(pi[1:])
    ca_re = ct_re[:, :, None] * prs - ct_im[:, :, None] * pis
    ca_im = ct_re[:, :, None] * pis + ct_im[:, :, None] * prs
    ca_re = ca_re.reshape(S5_SETS, S5_STATE, S5_CHUNK * LANES).astype(MXU_DTYPE)
    ca_im = ca_im.reshape(S5_SETS, S5_STATE, S5_CHUNK * LANES).astype(MXU_DTYPE)

    a16 = jnp.concatenate([pr[S5_CHUNK].reshape(S5_SETS, 1, S5_SET_STATE),
                           pi[S5_CHUNK].reshape(S5_SETS, 1, S5_SET_STATE)], axis=2)
    drow = jnp.tile(d_skip.astype(F32).reshape(S5_SETS, 1, LANES), (1, 1, S5_CHUNK))
    return lb, rc, ca_re, ca_im, a16, drow


def _s5_layer(x, batch, w_in, tables, layer, w_out, w_gate, ln):
    ug = _s5_in(x, w_in, layer, tm=512)
    hg = _s5_core(ug, tables, layer, batch)
    return _s5_out(hg, w_out, w_gate, layer, x, ln, tm=512)


def _split_hi_lo(v):
    hi = v.astype(MXU_DTYPE)
    lo = (v - hi.astype(F32)).astype(MXU_DTYPE)
    return jnp.concatenate([hi, lo], axis=1)


def _ssd_core_kernel(z_ref, xs_ref, bc_ref, xin_ref, wdt_ref, cwx_ref, cwb_ref, cbx_ref, cbb_ref,
                     dtb_ref, a_ref, dsk_ref, nw_ref, e2_ref, o_ref,
                     xbuf_ref, bbuf_ref, st_ref, yd_ref):
    q = SSD_CHUNK
    pad = 8

    @pl.when(pl.program_id(1) == 0)
    def _():
        xbuf_ref[...] = jnp.zeros_like(xbuf_ref)
        bbuf_ref[...] = jnp.zeros_like(bbuf_ref)
        st_ref[...] = jnp.zeros_like(st_ref)

    def conv(in_ref, tail_ref, w_ref, b_ref):
        cur = in_ref[...]
        tail = tail_ref[...]
        row8 = lax.broadcasted_iota(jnp.int32, tail.shape, 0)
        acc = b_ref[...] + w_ref[SSD_CONV - 1:SSD_CONV, :] * cur
        for k in range(1, SSD_CONV):
            back = pltpu.roll(cur, k, axis=0)
            head = jnp.where(row8 < k, pltpu.roll(tail, k, axis=0), back[0:pad])
            back = jnp.concatenate([head, back[pad:]], axis=0)
            acc = acc + w_ref[SSD_CONV - 1 - k:SSD_CONV - k, :] * back
        tail_ref[...] = cur[q - pad:, :]
        return acc + acc * jnp.tanh(acc)

    xs = conv(xs_ref, xbuf_ref, cwx_ref, cbx_ref)
    bc = conv(bc_ref, bbuf_ref, cwb_ref, cbb_ref)

    dtr = _mm(xin_ref[...].astype(MXU_DTYPE), wdt_ref[...]) + dtb_ref[...]
    dt = jnp.maximum(dtr, 0.0) + jnp.log1p(jnp.exp(-jnp.abs(dtr)))
    adt = dt * a_ref[...]
    ri = lax.broadcasted_iota(jnp.int32, (q, q), 0)
    ci = lax.broadcasted_iota(jnp.int32, (q, q), 1)
    causal = ri >= ci
    tri = causal.astype(F32)
    cs = jnp.dot(tri, adt, preferred_element_type=F32, precision=HI)
    eye = (ri == ci).astype(F32)
    cs2 = cs * LOG2_E
    r_rows = _mm_nt(eye, cs2, precision=HI) - jnp.log2(_mm_nt(eye, dt, precision=HI))
    tot = cs[q - 1:q, :]
    w_col = dt * jnp.exp(tot - cs)
    din = jnp.exp(cs)
    w_x = _mm(_split_hi_lo(w_col), e2_ref[...])
    din_x = _mm(_split_hi_lo(din), e2_ref[...])

    lane = lax.broadcasted_iota(jnp.int32, (q, LANES), 1)
    low = lane < SSD_HEADDIM

    xw = (xs * w_x).astype(MXU_DTYPE)
    xsb = xs.astype(MXU_DTYPE)
    for g in range(SSD_GROUPS):
        bg = bc[:, g * SSD_STATE:(g + 1) * SSD_STATE].astype(MXU_DTYPE)
        cg = bc[:, SSD_GROUPS * SSD_STATE + g * SSD_STATE:
                SSD_GROUPS * SSD_STATE + (g + 1) * SSD_STATE].astype(MXU_DTYPE)
        cb = _mm_nt(cg, bg)
        for pair in range(SSD_HEADS_PER_GROUP // 2):
            h0 = g * SSD_HEADS_PER_GROUP + 2 * pair
            ms = []
            for h in (h0, h0 + 1):
                seg = cs2[:, h:h + 1] - r_rows[h:h + 1, :]
                ms.append((cb * jnp.exp2(jnp.where(causal, seg, -jnp.inf))).astype(MXU_DTYPE))
            lhs = jnp.concatenate(ms, axis=1)
            xp = xsb[:, h0 * SSD_HEADDIM:(h0 + 2) * SSD_HEADDIM]
            zero = jnp.zeros_like(xp)
            rhs = jnp.concatenate([jnp.where(low, xp, zero), jnp.where(low, zero, xp)], axis=0)
            yd_ref[:, h0 * SSD_HEADDIM:(h0 + 2) * SSD_HEADDIM] = _mm(lhs, rhs)
        gsl = slice(g * SSD_GROUP_DIM, (g + 1) * SSD_GROUP_DIM)
        st = st_ref[g]
        y_off = _mm(cg, st.astype(MXU_DTYPE)) * din_x[:, gsl]
        yd_ref[:, gsl] = yd_ref[:, gsl] + y_off
        st_ref[g] = din_x[q - 1:q, gsl] * st + _mm_tn(bg, xw[:, gsl])

    y = yd_ref[...] + dsk_ref[...] * xs
    y = y * z_ref[...]
    for g in range(SSD_GROUPS):
        gsl = slice(g * SSD_GROUP_DIM, (g + 1) * SSD_GROUP_DIM)
        yg = y[:, gsl]
        yg = yg * lax.rsqrt(jnp.mean(yg * yg, axis=-1, keepdims=True) + LN_EPS)
        o_ref[:, gsl] = (yg * nw_ref[:, gsl]).astype(o_ref.dtype)


def _ssd_core(proj, x, w_dt, batch, conv_w, conv_b, dt_bias, a_log, d_skip, norm_w):
    m = proj.shape[0]
    q = SSD_CHUNK
    ncb = m // batch // q
    di = SSD_D_INNER
    wdt = jnp.pad(w_dt.astype(MXU_DTYPE), ((0, 0), (0, LANES - SSD_HEADS)))
    cw = 0.5 * conv_w.astype(F32)
    cbias = 0.5 * conv_b.astype(F32)[None, :]
    padh = LANES - SSD_HEADS
    dtb = jnp.pad(dt_bias.astype(F32), (0, padh))[None, :]
    a = jnp.pad(-jnp.exp(a_log.astype(F32)), (0, padh))[None, :]
    dsk = jnp.repeat(d_skip.astype(F32), SSD_HEADDIM)[None, :]
    nw = norm_w.astype(F32)[None, :]
    expand = (np.arange(LANES)[:, None] == (np.arange(di) // SSD_HEADDIM)[None, :]).astype(np.float32)
    e2 = jnp.asarray(np.concatenate([expand, expand], axis=0), dtype=MXU_DTYPE)

    row = lambda b, c: b * ncb + c
    full = _resident
    return pl.pallas_call(
        _ssd_core_kernel,
        grid=(batch, ncb),
        in_specs=[pl.BlockSpec((q, di), lambda b, c: (row(b, c), 0)),
                  pl.BlockSpec((q, di), lambda b, c: (row(b, c), 1)),
                  pl.BlockSpec((q, SSD_BC), lambda b, c: (row(b, c), 2 * di // SSD_BC)),
                  pl.BlockSpec((q, D_MODEL), lambda b, c: (row(b, c), 0)),
                  full((D_MODEL, LANES)),
                  full((SSD_CONV, di)), full((SSD_CONV, SSD_BC)), full((1, di)), full((1, SSD_BC)),
                  full((1, LANES)), full((1, LANES)), full((1, di)), full((1, di)),
                  full((2 * LANES, di))],
        out_specs=pl.BlockSpec((q, di), lambda b, c: (row(b, c), 0)),
        out_shape=jax.ShapeDtypeStruct((m, di), MXU_DTYPE),
        scratch_shapes=[pltpu.VMEM((8, di), F32), pltpu.VMEM((8, SSD_BC), F32),
                        pltpu.VMEM((SSD_GROUPS, SSD_STATE, SSD_GROUP_DIM), F32),
                        pltpu.VMEM((q, di), F32)],
        compiler_params=_params("parallel", "arbitrary"),
        name="ssd_core",
    )(proj, proj, proj, x, wdt, cw[:, :di], cw[:, di:], cbias[:, :di], cbias[:, di:], dtb, a, dsk, nw, e2)


def _ssd_layer(x, batch, w_in_stack, layer, conv_w, conv_b, dt_bias, a_log, d_skip, norm_w, w_out, ln):
    n_main = SSD_D_INNER + SSD_CONV_DIM
    proj = _proj(x, jnp.swapaxes(w_in_stack, 1, 2), layer, n_main, tm=min(1024, x.shape[0]), tn=1024,
                 out_dtype=F32, silu_cols=SSD_D_INNER, w_transposed=True)
    w_dt = w_in_stack[layer, :, n_main:]
    y = _ssd_core(proj, x, w_dt, batch, conv_w, conv_b, dt_bias, a_log, d_skip, norm_w)
    return _out_ln(y, w_out.astype(MXU_DTYPE), x, ln, tm=512)


def _ret_core_kernel(cd_ref, q_ref, k_ref, v_ref, g_ref, cos_ref, sin_ref, dmat_ref, xi_ref, zeta_ref,
                     o_ref, st_ref):
    @pl.when(pl.program_id(1) == 0)
    def _():
        st_ref[...] = jnp.zeros_like(st_ref)

    cos = cos_ref[...]
    sin = sin_ref[...]
    half = RET_DK // 2

    def rotate(t):
        t1 = t[:, :half]
        t2 = t[:, half:]
        return jnp.concatenate([t1 * cos - t2 * sin, t1 * sin + t2 * cos], axis=1)

    for h in range(RET_HEADS):
        qh = rotate(q_ref[:, h * RET_DK:(h + 1) * RET_DK].astype(F32))
        kh = rotate(k_ref[:, h * RET_DK:(h + 1) * RET_DK].astype(F32) * (RET_DK ** -0.5))
        vh = v_ref[:, h * RET_DV:(h + 1) * RET_DV].astype(MXU_DTYPE)
        qb = qh.astype(MXU_DTYPE)
        scores = _mm_nt(qb, kh.astype(MXU_DTYPE)) * dmat_ref[h]
        inner = _mm(scores.astype(MXU_DTYPE), vh)
        st = st_ref[h]
        xi = jnp.concatenate([xi_ref[h]] * (RET_DV // LANES), axis=1)
        cross = _mm(qb, st.astype(MXU_DTYPE)) * xi
        kz = kh * jnp.concatenate([zeta_ref[h]] * (RET_DK // LANES), axis=1)
        st_ref[h] = cd_ref[h] * st + _mm_tn(kz.astype(MXU_DTYPE), vh)
        out = inner + cross
        mu = jnp.mean(out, axis=-1, keepdims=True)
        d = out - mu
        var = jnp.mean(d * d, axis=-1, keepdims=True)
        out = d * lax.rsqrt(var + LN_EPS)
        gate = _silu(g_ref[:, h * RET_DV:(h + 1) * RET_DV].astype(F32))
        o_ref[:, h * RET_DV:(h + 1) * RET_DV] = (gate * out).astype(o_ref.dtype)


def _ret_core(proj, batch):
    m = proj.shape[0]
    seq = m // batch
    q = RET_CHUNK
    ncb = seq // q
    theta = (1.0 / (RET_ROPE_BASE ** np.linspace(0.0, 1.0, RET_DK // 2))).astype(np.float32)
    ang = (np.arange(seq, dtype=np.float32)[:, None] * theta[None, :]).astype(np.float64)
    cos, sin = np.cos(ang).astype(np.float32), np.sin(ang).astype(np.float32)
    lg = np.log1p(-np.exp2(-5.0 - np.arange(RET_HEADS)))
    pos = np.arange(q, dtype=np.float64)
    diff = pos[:, None] - pos[None, :]
    dmat = np.where(diff >= 0, np.exp(np.maximum(diff, 0.0)[None] * lg[:, None, None]), 0.0).astype(np.float32)
    xi = np.exp((pos[None, :] + 1.0) * lg[:, None])
    zeta = np.exp((q - 1.0 - pos[None, :]) * lg[:, None])
    xi_b = np.broadcast_to(xi[:, :, None], (RET_HEADS, q, LANES)).astype(np.float32)
    zeta_b = np.broadcast_to(zeta[:, :, None], (RET_HEADS, q, LANES)).astype(np.float32)
    chunk_decay = np.exp(q * lg).astype(np.float32)

    row = lambda b, c: b * ncb + c
    d = D_MODEL
    return pl.pallas_call(
        _ret_core_kernel,
        grid=(batch, ncb),
        in_specs=[pl.BlockSpec(memory_space=pltpu.SMEM),
                  pl.BlockSpec((q, d), lambda b, c: (row(b, c), 0)),
                  pl.BlockSpec((q, d), lambda b, c: (row(b, c), 1)),
                  pl.BlockSpec((q, 2 * d), lambda b, c: (row(b, c), 1)),
                  pl.BlockSpec((q, 2 * d), lambda b, c: (row(b, c), 2)),
                  pl.BlockSpec((q, RET_DK // 2), lambda b, c: (c, 0)),
                  pl.BlockSpec((q, RET_DK // 2), lambda b, c: (c, 0)),
                  _resident((RET_HEADS, q, q)),
                  _resident((RET_HEADS, q, LANES)),
                  _resident((RET_HEADS, q, LANES))],
        out_specs=pl.BlockSpec((q, 2 * d), lambda b, c: (row(b, c), 0)),
        out_shape=jax.ShapeDtypeStruct((m, 2 * d), MXU_DTYPE),
        scratch_shapes=[pltpu.VMEM((RET_HEADS, RET_DK, RET_DV), F32)],
        compiler_params=_params("parallel", "arbitrary"),
        name="ret_core",
    )(chunk_decay, proj, proj, proj, proj, cos, sin, dmat, xi_b, zeta_b)


def _ret_layer(x, batch, w_in_stack, layer, w_out, ln):
    proj = _proj(x, w_in_stack, layer, RET_IN_DIM, tm=min(1024, x.shape[0]), tn=1024, out_dtype=MXU_DTYPE)
    y = _ret_core(proj, batch)
    return _out_ln(y, w_out.astype(MXU_DTYPE), x, ln, tm=512)


def kernel(x, ln1_g, ln1_b, ln2_g, ln2_b, mlp_w1, mlp_w2, s5_w_in, s5_lam_re, s5_lam_im, s5_log_dt, s5_b_re, s5_b_im, s5_c_re, s5_c_im, s5_d, s5_w_out, s5_w_gate, ssd_w_in, ssd_conv_w, ssd_conv_b, ssd_dt_bias, ssd_a_log, ssd_d, ssd_norm_w, ssd_w_out, ret_w_in, ret_w_out):
    batch, seq, d = x.shape
    h = x.reshape(batch * seq, d)
    ln1 = (ln1_g[:, None, :], ln1_b[:, None, :])
    ln2 = (ln2_g[:, None, :], ln2_b[:, None, :])
    s5_tables = jax.vmap(_s5_tables)(s5_lam_re, s5_lam_im, s5_log_dt, s5_b_re, s5_b_im, s5_c_re, s5_c_im, s5_d)
    s5_w_out_b = s5_w_out.astype(MXU_DTYPE)
    s5_w_gate_b = s5_w_gate.astype(MXU_DTYPE)
    for i in range(DEPTH):
        kind = i % 3
        j = i // 3
        if kind == 0:
            h = _s5_layer(h, batch, s5_w_in, s5_tables, j, s5_w_out_b, s5_w_gate_b, ln1 + (i,))
        elif kind == 1:
            h = _ssd_layer(h, batch, ssd_w_in, j, ssd_conv_w[j], ssd_conv_b[j], ssd_dt_bias[j], ssd_a_log[j],
                           ssd_d[j], ssd_norm_w[j], ssd_w_out[j], ln1 + (i,))
        else:
            h = _ret_layer(h, batch, ret_w_in, j, ret_w_out[j], ln1 + (i,))
        h = _mlp(h, mlp_w1, mlp_w2, i, ln2 + (i,), tm=min(1024, h.shape[0]), tf=512)
    return h.reshape(batch, seq, d)
```
